```python
import math
import jax
import jax.numpy as jnp
from jax import lax
import numpy as np

D_MODEL = 1024
BATCH = 8
SEQ = 8192
DEPTH = 2

GRID_W = 64
CTX_LEN = 256
N_BRANCH = 3
MLA_HEADS = 8
MLA_Q_RANK = 384
MLA_KV_RANK = 256
MLA_NOPE = 64
MLA_ROPE = 32
MLA_V = 64
DIFF_HEADS = 4
DIFF_QK = 64
DIFF_V = 128
RET_HEADS = 4
RET_K = 64
RET_V = 128
RET_CHUNK = 128
BRANCH_W = 512
D_FF = 2816
N_MOD = 9
ROPE_BASE = 10000.0
Q_BLOCK = 128
EPS = 1e-6
PROJ_SIZES = (MLA_Q_RANK, MLA_KV_RANK, MLA_ROPE,
              DIFF_HEADS * 2 * DIFF_QK, DIFF_HEADS * 2 * DIFF_QK, DIFF_HEADS * DIFF_V,
              RET_HEADS * RET_K, RET_HEADS * RET_K, RET_HEADS * RET_V, RET_HEADS * RET_V,
              N_BRANCH * D_MODEL)
PROJ_DIM = sum(PROJ_SIZES)

kernel_name = 'hybrid_mla_diff_retention_trunk'


def _rmsnorm(x, gain):
    xf = x.astype(jnp.float32)
    xf = xf * lax.rsqrt(jnp.mean(xf * xf, axis=-1, keepdims=True) + EPS)
    return (xf * gain.astype(jnp.float32)).astype(x.dtype)


def _modulate(x, gain, shift, scale):
    return _rmsnorm(x, gain) * (1.0 + scale) + shift


def _swiglu(h, w1, w3, w2):
    return (jax.nn.silu(h @ w1) * (h @ w3)) @ w2


def _rope_1d(x, pos):
    d = x.shape[-1]
    inv = ROPE_BASE ** (-jnp.arange(0, d, 2, dtype=jnp.float32) / d)
    ang = pos.astype(jnp.float32)[:, None] * inv[None, :]
    cos, sin = jnp.cos(ang).astype(x.dtype), jnp.sin(ang).astype(x.dtype)
    x1, x2 = x[..., : d // 2], x[..., d // 2:]
    return jnp.concatenate([x1 * cos - x2 * sin, x1 * sin + x2 * cos], axis=-1)


def _rope_2d(x, row, col):
    h = x.shape[-1] // 2
    return jnp.concatenate([_rope_1d(x[..., :h], row), _rope_1d(x[..., h:], col)], axis=-1)


def _heads(z, n_heads):
    b, s, _ = z.shape
    return z.reshape(b, s, n_heads, -1).transpose(0, 2, 1, 3)


def _merge_heads(z):
    b, h, s, d = z.shape
    return z.transpose(0, 2, 1, 3).reshape(b, s, h * d)


def _sweep_queries(fn, *qs):
    b, h, n, _ = qs[0].shape
    nb = n // Q_BLOCK
    blocks = tuple(jnp.moveaxis(q.reshape(b, h, nb, Q_BLOCK, q.shape[-1]), 2, 0) for q in qs)
    out = lax.map(lambda a: fn(*a), blocks)
    return jnp.moveaxis(out, 0, 2).reshape(b, h, n, out.shape[-1])


def _mla_attend(q_nope, q_rope, k_nope, k_rope, v):
    s = jnp.einsum('bhqd,bhkd->bhqk', q_nope, k_nope) + jnp.einsum('bhqr,bkr->bhqk', q_rope, k_rope)
    p = jax.nn.softmax(s.astype(jnp.float32), axis=-1).astype(v.dtype)
    return jnp.einsum('bhqk,bhkv->bhqv', p, v)


def _diff_attend(q1, q2, k1, k2, v, lam):
    s1 = jnp.einsum('bhqd,bhkd->bhqk', q1, k1).astype(jnp.float32)
    s2 = jnp.einsum('bhqd,bhkd->bhqk', q2, k2).astype(jnp.float32)
    a = jax.nn.softmax(s1, axis=-1) - lam * jax.nn.softmax(s2, axis=-1)
    return jnp.einsum('bhqk,bhkv->bhqv', a.astype(v.dtype), v)


def _retention(q, k, v, log_gamma, state0):
    b, h, s, _ = q.shape
    dv = v.shape[-1]
    nc = s // RET_CHUNK
    idx = jnp.arange(RET_CHUNK, dtype=jnp.float32)
    lg = log_gamma.astype(jnp.float32)[:, None]
    rel = idx[:, None] - idx[None, :]
    intra = jnp.where(rel >= 0, jnp.exp(lg[:, :, None] * jnp.maximum(rel, 0.0)), 0.0)
    q_dec = jnp.exp(lg * (idx + 1.0))[:, :, None]
    k_dec = jnp.exp(lg * (RET_CHUNK - 1.0 - idx))[:, :, None]
    c_dec = jnp.exp(lg * RET_CHUNK)[:, :, None]

    def chunks(z):
        return jnp.moveaxis(z.reshape(b, h, nc, RET_CHUNK, z.shape[-1]), 2, 0)

    def step(state, inp):
        qi, ki, vi = inp
        a = jnp.einsum('bhqd,bhkd->bhqk', qi, ki) * intra
        o = jnp.einsum('bhqk,bhkv->bhqv', a, vi) + jnp.einsum('bhqd,bhdv->bhqv', qi * q_dec, state)
        state = state * c_dec + jnp.einsum('bhkd,bhkv->bhdv', ki * k_dec, vi)
        return state, o

    state, out = lax.scan(step, state0, (chunks(q), chunks(k), chunks(v)))
    return jnp.moveaxis(out, 0, 2).reshape(b, h, s, dv), state


def _token_mixer(hx, hc, row, col, t, w_in, mla_q_norm, mla_w_qb, mla_kv_norm, mla_w_kvb,
                 diff_lambda, diff_norm, lambda_init, ret_decay, ret_norm, w_branch, w_out, ctx_out):
    f32 = jnp.float32
    splits = np.cumsum(PROJ_SIZES)[:-1].tolist()
    px = jnp.split(hx @ w_in, splits, axis=-1)
    pc = jnp.split(hc @ w_in, splits, axis=-1)
    rope_grid = lambda z: _rope_2d(z, row, col)
    rope_seq = lambda z: _rope_1d(z, t)

    def mla_proj(p, rope):
        q = _heads(_rmsnorm(p[0], mla_q_norm) @ mla_w_qb, MLA_HEADS) * (MLA_NOPE + MLA_ROPE) ** -0.5
        kv = _heads(_rmsnorm(p[1], mla_kv_norm) @ mla_w_kvb, MLA_HEADS)
        q_nope, q_rope = q[..., :MLA_NOPE], q[..., MLA_NOPE:]
        k_nope, v = kv[..., :MLA_NOPE], kv[..., MLA_NOPE:]
        k_rope = p[2]
        if rope is not None:
            q_rope, k_rope = rope(q_rope), rope(k_rope)
        return q_nope, q_rope, k_nope, k_rope, v

    xqn, xqr, xkn, xkr, xv = mla_proj(px, rope_grid)
    cqn, cqr, ckn, ckr, cv = mla_proj(pc, None)
    kn_all = jnp.concatenate([xkn, ckn], axis=2)
    kr_all = jnp.concatenate([xkr, ckr], axis=1)
    mv_all = jnp.concatenate([xv, cv], axis=2)
    mla_x = _merge_heads(_sweep_queries(
        lambda qn, qr: _mla_attend(qn, qr, kn_all, kr_all, mv_all), xqn, xqr))

    dl = diff_lambda.astype(f32)
    lam = jnp.exp(jnp.sum(dl[0] * dl[1])) - jnp.exp(jnp.sum(dl[2] * dl[3])) + lambda_init

    def diff_proj(p, rope):
        b, s, _ = p[3].shape
        q = p[3].reshape(b, s, DIFF_HEADS, 2, DIFF_QK).transpose(3, 0, 2, 1, 4) * DIFF_QK ** -0.5
        k = p[4].reshape(b, s, DIFF_HEADS, 2, DIFF_QK).transpose(3, 0, 2, 1, 4)
        v = _heads(p[5], DIFF_HEADS)
        if rope is not None:
            q, k = rope(q), rope(k)
        return q[0], q[1], k[0], k[1], v

    def diff_post(o):
        return _merge_heads(_rmsnorm(o, diff_norm) * (1.0 - lambda_init))

    xq1, xq2, xk1, xk2, xdv = diff_proj(px, rope_grid)
    cq1, cq2, ck1, ck2, cdv = diff_proj(pc, None)
    k1_all = jnp.concatenate([xk1, ck1], axis=2)
    k2_all = jnp.concatenate([xk2, ck2], axis=2)
    dv_all = jnp.concatenate([xdv, cdv], axis=2)
    diff_x = diff_post(_sweep_queries(
        lambda a1, a2: _diff_attend(a1, a2, k1_all, k2_all, dv_all, lam), xq1, xq2))

    log_gamma = -jnp.exp(ret_decay.astype(f32))

    def ret_proj(p, rope):
        q = _heads(p[6], RET_HEADS)
        k = _heads(p[7], RET_HEADS) * RET_K ** -0.5
        v = _heads(p[8], RET_HEADS)
        if rope is not None:
            q, k = rope(q), rope(k)
        return q, k, v

    def ret_post(o, g):
        return _merge_heads(_rmsnorm(o.astype(g.dtype), ret_norm)) * jax.nn.silu(g)

    flip = lambda z: jnp.flip(z, axis=2)
    rq_c, rk_c, rv_c = ret_proj(pc, None)
    rq_x, rk_x, rv_x = ret_proj(px, rope_seq)
    zero = jnp.zeros((hx.shape[0], RET_HEADS, RET_K, RET_V), f32)
    oc_f, st_f = _retention(rq_c, rk_c, rv_c, log_gamma[0], zero)
    oc_b, st_b = _retention(flip(rq_c), flip(rk_c), flip(rv_c), log_gamma[1], zero)
    ox_f, _ = _retention(rq_x, rk_x, rv_x, log_gamma[0], st_f)
    ox_b, _ = _retention(flip(rq_x), flip(rk_x), flip(rv_x), log_gamma[1], st_b)
    ret_x = ret_post(ox_f + flip(ox_b), px[9])

    def merge(y_mla, y_diff, y_ret, gate_logits):
        b, s, _ = gate_logits.shape
        g = jax.nn.sigmoid(gate_logits).reshape(b, s, N_BRANCH, D_MODEL)
        y = (g[:, :, 0] * (y_mla @ w_branch[0]) + g[:, :, 1] * (y_diff @ w_branch[1])
             + g[:, :, 2] * (y_ret @ w_branch[2]))
        return y @ w_out

    y_x = merge(mla_x, diff_x, ret_x, px[10])
    y_c = None
    if ctx_out:
        mla_c = _merge_heads(_mla_attend(cqn, cqr, ckn, ckr, cv))
        diff_c = diff_post(_diff_attend(cq1, cq2, ck1, ck2, cdv, lam))
        ret_c = ret_post(oc_f + flip(oc_b), pc[9])
        y_c = merge(mla_c, diff_c, ret_c, pc[10])
    return y_x, y_c


def setup_inputs(seed: int = 0) -> dict:
    key = jax.random.key(seed)
    ks = jax.random.split(key, 26)
    f32 = jnp.float32
    d, nl = D_MODEL, DEPTH

    def nrm(i, shape, scale):
        return jax.random.normal(ks[i], shape, f32) * scale

    def gain(i, shape):
        return 1.0 + 0.02 * jax.random.normal(ks[i], shape, f32)

    base_decay = jnp.log(-jnp.log1p(-jnp.exp2(-5.0 - jnp.arange(RET_HEADS, dtype=f32))))
    return {
        'x': nrm(0, (BATCH, SEQ, d), 1.0),
        'c': nrm(1, (BATCH, d), 1.0),
        'ctx': nrm(2, (BATCH, CTX_LEN, d), 1.0),
        'c_ctx': nrm(3, (d,), 1.0),
        'ada_w': nrm(4, (nl, d, N_MOD * d), 0.5 * d ** -0.5),
        'ada_b': nrm(5, (nl, N_MOD * d), 0.02),
        'norm_gain': gain(6, (nl, 3, d)),
        'ffn1_w1': nrm(7, (nl, d, D_FF), d ** -0.5),
        'ffn1_w3': nrm(8, (nl, d, D_FF), d ** -0.5),
        'ffn1_w2': nrm(9, (nl, D_FF, d), D_FF ** -0.5),
        'ffn2_w1': nrm(10, (nl, d, D_FF), d ** -0.5),
        'ffn2_w3': nrm(11, (nl, d, D_FF), d ** -0.5),
        'ffn2_w2': nrm(12, (nl, D_FF, d), D_FF ** -0.5),
        'w_in': nrm(13, (nl, d, PROJ_DIM), d ** -0.5),
        'mla_q_norm': gain(14, (nl, MLA_Q_RANK)),
        'mla_w_qb': nrm(15, (nl, MLA_Q_RANK, MLA_HEADS * (MLA_NOPE + MLA_ROPE)), MLA_Q_RANK ** -0.5),
        'mla_kv_norm': gain(16, (nl, MLA_KV_RANK)),
        'mla_w_kvb': nrm(17, (nl, MLA_KV_RANK, MLA_HEADS * (MLA_NOPE + MLA_V)), MLA_KV_RANK ** -0.5),
        'diff_lambda': nrm(18, (nl, 4, DIFF_QK), 0.1),
        'diff_norm': gain(19, (nl, DIFF_V)),
        'ret_decay': base_decay + 0.05 * jax.random.normal(ks[20], (nl, 2, RET_HEADS), f32),
        'ret_norm': gain(21, (nl, RET_V)),
        'w_branch': nrm(22, (nl, N_BRANCH, BRANCH_W, d), BRANCH_W ** -0.5),
        'w_out': nrm(23, (nl, d, d), d ** -0.5),
        'final_norm': gain(24, (d,)),
    }


def reference(x, c, ctx, c_ctx, ada_w, ada_b, norm_gain, ffn1_w1, ffn1_w3, ffn1_w2,
              ffn2_w1, ffn2_w3, ffn2_w2, w_in, mla_q_norm, mla_w_qb, mla_kv_norm, mla_w_kvb,
              diff_lambda, diff_norm, ret_decay, ret_norm, w_branch, w_out, final_norm):
    n = x.shape[1]
    rows = n // GRID_W
    t = jnp.arange(n)
    row = jnp.repeat(jnp.arange(rows), GRID_W)
    col = t - row * GRID_W
    sc_x = jax.nn.silu(c)
    sc_c = jax.nn.silu(c_ctx)
    xc = ctx
    for l in range(DEPTH):
        lambda_init = 0.8 - 0.6 * math.exp(-0.3 * l)
        ctx_out = l < DEPTH - 1
        mx = jnp.split((sc_x @ ada_w[l] + ada_b[l])[:, None, :], N_MOD, axis=-1)
        mc = jnp.split(sc_c @ ada_w[l] + ada_b[l], N_MOD, axis=-1)
        x = x + mx[2] * 0.5 * _swiglu(_modulate(x, norm_gain[l, 0], mx[0], mx[1]),
                                      ffn1_w1[l], ffn1_w3[l], ffn1_w2[l])
        xc = xc + mc[2] * 0.5 * _swiglu(_modulate(xc, norm_gain[l, 0], mc[0], mc[1]),
                                        ffn1_w1[l], ffn1_w3[l], ffn1_w2[l])
        y_x, y_c = _token_mixer(
            _modulate(x, norm_gain[l, 1], mx[3], mx[4]), _modulate(xc, norm_gain[l, 1], mc[3], mc[4]),
            row, col, t, w_in[l], mla_q_norm[l], mla_w_qb[l], mla_kv_norm[l], mla_w_kvb[l],
            diff_lambda[l], diff_norm[l], lambda_init, ret_decay[l], ret_norm[l], w_branch[l], w_out[l],
            ctx_out)
        x = x + mx[5] * y_x
        x = x + mx[8] * 0.5 * _swiglu(_modulate(x, norm_gain[l, 2], mx[6], mx[7]),
                                      ffn2_w1[l], ffn2_w3[l], ffn2_w2[l])
        if ctx_out:
            xc = xc + mc[5] * y_c
            xc = xc + mc[8] * 0.5 * _swiglu(_modulate(xc, norm_gain[l, 2], mc[6], mc[7]),
                                            ffn2_w1[l], ffn2_w3[l], ffn2_w2[l])
    return _rmsnorm(x, final_norm)
```

```python
import functools
import math

import jax
import jax.numpy as jnp
from jax import lax
from jax.experimental import pallas as pl
from jax.experimental.pallas import tpu as pltpu

D_MODEL = 1024
DEPTH = 2
GRID_W = 64
N_BRANCH = 3
MLA_HEADS = 8
MLA_Q_RANK = 384
MLA_KV_RANK = 256
MLA_NOPE = 64
MLA_ROPE = 32
MLA_V = 64
DIFF_HEADS = 4
DIFF_QK = 64
DIFF_V = 128
RET_HEADS = 4
RET_K = 64
RET_V = 128
RET_CHUNK = 128
BRANCH_W = 512
D_FF = 2816
N_MOD = 9
ROPE_BASE = 10000.0
EPS = 1e-6

LANES = 128
LOG2E = 1.4426950408889634
FF_CHUNK = 256
KEY_CHUNK = 256
VMEM_LIMIT = 56 * 1024 * 1024

BF16 = jnp.bfloat16
F32 = jnp.float32
_NT = (((1,), (1,)), ((), ()))
_TN = (((0,), (0,)), ((), ()))


def _params(*sem):
    return pltpu.CompilerParams(dimension_semantics=sem, vmem_limit_bytes=VMEM_LIMIT)


def _const(shape):
    zeros = (0,) * len(shape)
    return pl.BlockSpec(shape, lambda *_: zeros, pipeline_mode=pl.Buffered(1))


def _rms(x, gain):
    return x * lax.rsqrt(jnp.mean(x * x, axis=-1, keepdims=True) + EPS) * gain


def _modulated(x, gain, shift, scale):
    return _rms(x, gain) * (1.0 + scale) + shift


def _dot(a, b):
    return jnp.dot(a, b, preferred_element_type=F32)


def _ada_kernel(c_ref, w_ref, b_ref, o_ref):
    c = c_ref[...]
    o_ref[0] = _dot(c * jax.nn.sigmoid(c), w_ref[0]) + b_ref[0]


def _ada(cs, ada_w, ada_b):
    nl, d, n = ada_w.shape
    r = cs.shape[0]
    tn = n // N_MOD
    return pl.pallas_call(
        _ada_kernel,
        grid=(nl, n // tn),
        in_specs=[pl.BlockSpec((r, d), lambda l, j: (0, 0)),
                  pl.BlockSpec((1, d, tn), lambda l, j: (l, 0, j)),
                  pl.BlockSpec((1, 1, tn), lambda l, j: (l, 0, j))],
        out_specs=pl.BlockSpec((1, r, tn), lambda l, j: (l, 0, j)),
        out_shape=jax.ShapeDtypeStruct((nl, r, n), F32),
        compiler_params=_params("parallel", "parallel"),
        name="ada",
    )(cs, ada_w, ada_b.reshape(nl, 1, n))


def _ffn_kernel(x_ref, mod_ref, gain_ref, w1_ref, w3_ref, w2_ref, *rest, row0, gain_row, final):
    if final:
        fg_ref, o_ref, acc_ref = rest
    else:
        o_ref, acc_ref = rest
    x = x_ref[0]
    h = _modulated(x, gain_ref[gain_row:gain_row + 1, :], mod_ref[0, row0:row0 + 1, :],
                   mod_ref[0, row0 + 1:row0 + 2, :]).astype(BF16)
    acc_ref[...] = jnp.zeros_like(acc_ref)

    def body(c, carry):
        a = _dot(h, w1_ref[c])
        b = _dot(h, w3_ref[c])
        g = (a * jax.nn.sigmoid(a) * b).astype(BF16)
        acc_ref[...] += _dot(g, w2_ref[c])
        return carry

    lax.fori_loop(0, w1_ref.shape[0], body, 0)
    y = x + (mod_ref[0, row0 + 2:row0 + 3, :] * 0.5) * acc_ref[...]
    if final:
        y = _rms(y, fg_ref[...])
    o_ref[0] = y


def _ffn(x, mod, gain, w1, w3, w2, *, row0, gain_row, final_gain=None):
    b, s, d = x.shape
    tm = min(512, s)
    nc = w1.shape[0]
    final = final_gain is not None
    in_specs = [pl.BlockSpec((1, tm, d), lambda bi, i: (bi, i, 0)),
                pl.BlockSpec((1, N_MOD, d), lambda bi, i: (bi, 0, 0)),
                _const(gain.shape), _const(w1.shape), _const(w3.shape), _const(w2.shape)]
    args = [x, mod, gain, w1, w3, w2]
    if final:
        in_specs.append(_const((1, d)))
        args.append(final_gain.reshape(1, d))
    return pl.pallas_call(
        functools.partial(_ffn_kernel, row0=row0, gain_row=gain_row, final=final),
        grid=(b, s // tm),
        in_specs=in_specs,
        out_specs=pl.BlockSpec((1, tm, d), lambda bi, i: (bi, i, 0)),
        out_shape=jax.ShapeDtypeStruct(x.shape, F32),
        scratch_shapes=[pltpu.VMEM((tm, d), F32)],
        compiler_params=_params("parallel", "parallel"),
        name="ffn",
    )(*args)


_O_QLAT = 0
_O_KVLAT = _O_QLAT + MLA_Q_RANK
_O_KR = _O_KVLAT + MLA_KV_RANK
_O_DQ = _O_KR + 2 * LANES
_O_RQ = _O_DQ + 4 * BRANCH_W
_O_RV = _O_RQ + 4 * RET_HEADS * RET_K
_W_MAIN = _O_RV + 2 * BRANCH_W


def _proj_kernel(x_ref, mod_ref, gain_ref, wm_ref, wqa_ref, wqb_ref, wka_ref, wvt_ref, wdvt_ref,
                 qn_ref, kvn_ref, cm_ref, sm_ref, cd_ref, sd_ref, cr_ref, sr_ref,
                 mq_ref, mk_ref, mvt_ref, dq_ref, dk_ref, dvt_ref, rq_ref, rk_ref, rv_ref, rg_ref):
    x = x_ref[0]
    h = _modulated(x, gain_ref[1:2, :], mod_ref[0, 3:4, :], mod_ref[0, 4:5, :]).astype(BF16)

    def seg(lo, width):
        return _dot(h, wm_ref[:, lo:lo + width])

    qlat = _rms(seg(_O_QLAT, MLA_Q_RANK), qn_ref[...]).astype(BF16)
    kvlat = _rms(seg(_O_KVLAT, MLA_KV_RANK), kvn_ref[...]).astype(BF16)
    cm, sm = cm_ref[...], sm_ref[...]
    qa = _dot(qlat, wqa_ref[...])
    qb = _dot(qlat, wqb_ref[...])
    ka = _dot(kvlat, wka_ref[...])
    kr = seg(_O_KR, 2 * LANES)
    krr = kr[:, :LANES] * cm + kr[:, LANES:] * sm
    q_scale = (MLA_NOPE + MLA_ROPE) ** -0.5 * LOG2E
    for hd in range(MLA_HEADS):
        sl = slice(hd * LANES, (hd + 1) * LANES)
        mq_ref[0, :, sl] = ((qa[:, sl] * cm + qb[:, sl] * sm) * q_scale).astype(BF16)
        mk_ref[0, :, sl] = (ka[:, sl] + krr).astype(BF16)
    mvt_ref[0, 0] = lax.dot_general(wvt_ref[...], kvlat, _NT, preferred_element_type=F32).astype(BF16)

    cd, sd = cd_ref[...], sd_ref[...]
    dq = seg(_O_DQ, 2 * BRANCH_W)
    dk = seg(_O_DQ + 2 * BRANCH_W, 2 * BRANCH_W)
    d_scale = DIFF_QK ** -0.5 * LOG2E
    for j in range(BRANCH_W // LANES):
        sl = slice(j * LANES, (j + 1) * LANES)
        sr = slice(BRANCH_W + j * LANES, BRANCH_W + (j + 1) * LANES)
        dq_ref[0, :, sl] = ((dq[:, sl] * cd + dq[:, sr] * sd) * d_scale).astype(BF16)
        dk_ref[0, :, sl] = (dk[:, sl] * cd + dk[:, sr] * sd).astype(BF16)
    dvt_ref[0, 0] = lax.dot_general(wdvt_ref[...], h, _NT, preferred_element_type=F32).astype(BF16)

    cr, sr_t = cr_ref[...], sr_ref[...]
    hk = RET_HEADS * RET_K
    rqk = seg(_O_RQ, 4 * hk)
    for j in range(hk // LANES):
        sl = slice(j * LANES, (j + 1) * LANES)
        rq_ref[0, :, sl] = (rqk[:, sl] * cr + rqk[:, hk + j * LANES:hk + (j + 1) * LANES] * sr_t).astype(BF16)
        rk_ref[0, :, sl] = ((rqk[:, 2 * hk + j * LANES:2 * hk + (j + 1) * LANES] * cr
                             + rqk[:, 3 * hk + j * LANES:3 * hk + (j + 1) * LANES] * sr_t)
                            * RET_K ** -0.5).astype(BF16)
    rv_ref[0] = seg(_O_RV, BRANCH_W).astype(BF16)
    rg_ref[0] = seg(_O_RV + BRANCH_W, BRANCH_W).astype(BF16)


def _proj(x, mod, gain, pw, tables):
    b, s, d = x.shape
    tm = KEY_CHUNK
    nt = s // tm
    tok = lambda w: pl.BlockSpec((1, tm, w), lambda bi, i: (bi, i, 0))
    slab = pl.BlockSpec((1, 1, BRANCH_W, tm), lambda bi, i: (bi, i, 0, 0))
    tab = pl.BlockSpec((tm, LANES), lambda bi, i: (i, 0))
    weights = [pw["main"], pw["qa"], pw["qb"], pw["ka"], pw["vt"], pw["dvt"], pw["qn"], pw["kvn"]]
    hk = RET_HEADS * RET_K
    widths = [MLA_HEADS * LANES, MLA_HEADS * LANES, None, BRANCH_W, BRANCH_W, None, hk, hk, BRANCH_W, BRANCH_W]
    out_specs, out_shape = [], []
    for w in widths:
        if w is None:
            out_specs.append(slab)
            out_shape.append(jax.ShapeDtypeStruct((b, nt, BRANCH_W, tm), BF16))
        else:
            out_specs.append(tok(w))
            out_shape.append(jax.ShapeDtypeStruct((b, s, w), BF16))
    return pl.pallas_call(
        _proj_kernel,
        grid=(b, nt),
        in_specs=[pl.BlockSpec((1, tm, d), lambda bi, i: (bi, i, 0)),
                  pl.BlockSpec((1, N_MOD, d), lambda bi, i: (bi, 0, 0)),
                  _const(gain.shape)] + [_const(w.shape) for w in weights] + [tab] * 6,
        out_specs=out_specs,
        out_shape=out_shape,
        compiler_params=_params("parallel", "parallel"),
        name="proj",
    )(x, mod, gain, *weights, *tables)


def _attend(q_rhs, segments, lanes, rows, dv):
    n = q_rhs.shape[0]

    def step(k, vts, carry):
        m, l, acc = carry
        s = lax.dot_general(k, q_rhs, _NT, preferred_element_type=F32)
        m_new = jnp.maximum(m, jnp.max(s, axis=0, keepdims=True))
        alpha = jnp.exp2(m - m_new)
        p = jnp.exp2(s - m_new)
        l = alpha * l + jnp.sum(p, axis=0, keepdims=True)
        pb = p.astype(BF16)
        pv = None
        for j, vt in enumerate(vts):
            t = _dot(vt, pb[j * KEY_CHUNK:(j + 1) * KEY_CHUNK, :])
            pv = t if pv is None else pv + t
        return m_new, l, alpha * acc + pv

    carry = (jnp.full((1, n), -jnp.inf, F32), jnp.zeros((1, n), F32), jnp.zeros((dv, n), F32))
    for k_ref, vt_ref, n_keys in segments:
        per = 2 if n_keys % (2 * KEY_CHUNK) == 0 else 1
        tk = per * KEY_CHUNK

        def body(c, carry, k_ref=k_ref, vt_ref=vt_ref, per=per, tk=tk):
            start = c * tk if isinstance(c, int) else pl.multiple_of(c * tk, tk)
            k = k_ref[0, pl.ds(start, tk), lanes]
            vts = [vt_ref[0, c * per + j, rows, :] for j in range(per)]
            return step(k, vts, carry)

        if n_keys // tk == 1:
            carry = body(0, carry)
        else:
            carry = lax.fori_loop(0, n_keys // tk, body, carry)
    return carry[2], carry[1]


def _mla_kernel(q_ref, *refs, seg_lens):
    o_ref = refs[-1]
    segments = [(refs[2 * i], refs[2 * i + 1], n) for i, n in enumerate(seg_lens)]
    for g in range(2):
        lanes = slice(g * LANES, (g + 1) * LANES)
        rows = slice(g * MLA_V, (g + 1) * MLA_V)
        acc, l = _attend(q_ref[0, :, lanes], segments, lanes, rows, MLA_V)
        o_ref[0, rows, :] = (acc / l).astype(BF16)


def _mla_attention(q, segments):
    b, s, _ = q.shape
    tq = min(512, s)
    in_specs = [pl.BlockSpec((1, tq, 2 * LANES), lambda bi, hp, i: (bi, i, hp))]
    args = [q]
    for k, vt in segments:
        n = k.shape[1]
        in_specs.append(pl.BlockSpec((1, n, 2 * LANES), lambda bi, hp, i: (bi, 0, hp)))
        in_specs.append(pl.BlockSpec((1, n // KEY_CHUNK, 2 * MLA_V, KEY_CHUNK), lambda bi, hp, i: (bi, 0, hp, 0)))
        args += [k, vt]
    return pl.pallas_call(
        functools.partial(_mla_kernel, seg_lens=tuple(k.shape[1] for k, _ in segments)),
        grid=(b, MLA_HEADS // 2, s // tq),
        in_specs=in_specs,
        out_specs=pl.BlockSpec((1, 2 * MLA_V, tq), lambda bi, hp, i: (bi, hp, i)),
        out_shape=jax.ShapeDtypeStruct((b, MLA_HEADS * MLA_V, s), BF16),
        compiler_params=_params("parallel", "parallel", "parallel"),
        name="mla_attn",
    )(*args)


def _diff_kernel(lam_ref, q_ref, *refs, seg_lens, out_scale):
    gain_ref, o_ref = refs[-2], refs[-1]
    segments = [(refs[2 * i], refs[2 * i + 1], n) for i, n in enumerate(seg_lens)]
    q = q_ref[0]
    tq = q.shape[0]
    first = lax.broadcasted_iota(jnp.int32, q.shape, 1) < DIFF_QK
    zero = jnp.zeros_like(q)
    q_rhs = jnp.concatenate([jnp.where(first, q, zero), jnp.where(first, zero, q)], axis=0)
    acc, l = _attend(q_rhs, segments, slice(0, LANES), slice(0, DIFF_V), DIFF_V)
    o = acc / l
    o = o[:, :tq] - lam_ref[0] * o[:, tq:]
    o = o * lax.rsqrt(jnp.mean(o * o, axis=0, keepdims=True) + EPS) * gain_ref[...]
    o_ref[0] = (o * out_scale).astype(BF16)


def _diff_attention(q, segments, lam, gain, lambda_init):
    b, s, _ = q.shape
    tq = min(256, s)
    in_specs = [pl.BlockSpec(memory_space=pltpu.SMEM),
                pl.BlockSpec((1, tq, LANES), lambda bi, h, i: (bi, i, h))]
    args = [lam.reshape(1), q]
    for k, vt in segments:
        n = k.shape[1]
        in_specs.append(pl.BlockSpec((1, n, LANES), lambda bi, h, i: (bi, 0, h)))
        in_specs.append(pl.BlockSpec((1, n // KEY_CHUNK, DIFF_V, KEY_CHUNK), lambda bi, h, i: (bi, 0, h, 0)))
        args += [k, vt]
    in_specs.append(pl.BlockSpec((DIFF_V, 1), lambda bi, h, i: (0, 0)))
    args.append(gain.reshape(DIFF_V, 1))
    return pl.pallas_call(
        functools.partial(_diff_kernel, seg_lens=tuple(k.shape[1] for k, _ in segments),
                          out_scale=1.0 - lambda_init),
        grid=(b, DIFF_HEADS, s // tq),
        in_specs=in_specs,
        out_specs=pl.BlockSpec((1, DIFF_V, tq), lambda bi, h, i: (bi, h, i)),
        out_shape=jax.ShapeDtypeStruct((b, DIFF_HEADS * DIFF_V, s), BF16),
        compiler_params=_params("parallel", "parallel", "parallel"),
        name="diff_attn",
    )(*args)


def _ret_kernel(q_ref, k_ref, v_ref, s0_ref, intra_ref, qdec_ref, kdec_ref, cdec_ref, *rest, reverse, post):
    if post:
        of_ref, g_ref, gain_ref, o_ref, sout_ref, state = rest
    else:
        o_ref, sout_ref, state = rest
    i = pl.program_id(1)

    @pl.when(i == 0)
    def _():
        state[...] = s0_ref[0]

    n_chunks = q_ref.shape[1] // RET_CHUNK
    lane = lax.broadcasted_iota(jnp.int32, (RET_CHUNK, RET_HEADS * RET_K), 1)
    for cc in range(n_chunks):
        c = n_chunks - 1 - cc if reverse else cc
        rows = slice(c * RET_CHUNK, (c + 1) * RET_CHUNK)
        qc, kc, vc = q_ref[0, rows, :], k_ref[0, rows, :], v_ref[0, rows, :]
        st = state[...]
        stb = st.astype(BF16)
        qd = (qc.astype(F32) * qdec_ref[...]).astype(BF16)
        kdt = (kc.astype(F32) * kdec_ref[...]).T.astype(BF16)
        outs, deltas = [], []
        for hd in range(RET_HEADS):
            mine = (lane >= hd * RET_K) & (lane < (hd + 1) * RET_K)
            vh = vc[:, hd * RET_V:(hd + 1) * RET_V]
            a = lax.dot_general(jnp.where(mine, qc, jnp.zeros_like(qc)), kc, _NT,
                                preferred_element_type=F32) * intra_ref[hd]
            o_h = _dot(a.astype(BF16), vh) + _dot(jnp.where(mine, qd, jnp.zeros_like(qd)), stb)
            outs.append(o_h)
            deltas.append(_dot(kdt[hd * RET_K:(hd + 1) * RET_K, :], vh))
        state[...] = st * cdec_ref[...] + jnp.concatenate(deltas, axis=0)
        if post:
            of = of_ref[0, rows, :]
            g = g_ref[0, rows, :].astype(F32)
            for hd in range(RET_HEADS):
                sl = slice(hd * RET_V, (hd + 1) * RET_V)
                outs[hd] = _rms(outs[hd] + of[:, sl], gain_ref[...]) * (g[:, sl] * jax.nn.sigmoid(g[:, sl]))
        o_ref[0, rows, :] = jnp.concatenate(outs, axis=1).astype(o_ref.dtype)

    @pl.when(i == pl.num_programs(1) - 1)
    def _():
        sout_ref[0] = state[...]


def _retention(q, k, v, s0, tabs, *, reverse, post=None):
    b, n, _ = q.shape
    t = min(512, n)
    nt = n // t
    hk, hv = RET_HEADS * RET_K, RET_HEADS * RET_V
    tile = (lambda bi, i: (bi, nt - 1 - i, 0)) if reverse else (lambda bi, i: (bi, i, 0))
    st_spec = pl.BlockSpec((1, hk, RET_V), lambda bi, i: (bi, 0, 0))
    in_specs = [pl.BlockSpec((1, t, hk), tile), pl.BlockSpec((1, t, hk), tile), pl.BlockSpec((1, t, hv), tile),
                st_spec] + [_const(a.shape) for a in tabs]
    args = [q, k, v, s0, *tabs]
    if post is not None:
        o_f, g, gain = post
        in_specs += [pl.BlockSpec((1, t, hv), tile), pl.BlockSpec((1, t, hv), tile), _const((1, RET_V))]
        args += [o_f, g, gain.reshape(1, RET_V)]
    return pl.pallas_call(
        functools.partial(_ret_kernel, reverse=reverse, post=post is not None),
        grid=(b, nt),
        in_specs=in_specs,
        out_specs=[pl.BlockSpec((1, t, hv), tile), st_spec],
        out_shape=[jax.ShapeDtypeStruct((b, n, hv), F32 if post is None else BF16),
                   jax.ShapeDtypeStruct((b, hk, RET_V), F32)],
        scratch_shapes=[pltpu.VMEM((hk, RET_V), F32)],
        compiler_params=_params("parallel", "arbitrary"),
        name="ret_bwd" if reverse else "ret_fwd",
    )(*args)


def _decay_tables(log_gamma, reverse):
    idx = jnp.arange(RET_CHUNK, dtype=F32)
    lg = log_gamma.astype(F32)[:, None]
    rel = idx[:, None] - idx[None, :]
    if reverse:
        rel = -rel
        q_pow, k_pow = RET_CHUNK - idx, idx
    else:
        q_pow, k_pow = idx + 1.0, RET_CHUNK - 1.0 - idx
    intra = jnp.where(rel >= 0, jnp.exp(lg[:, :, None] * jnp.maximum(rel, 0.0)), 0.0)
    qdec = jnp.repeat(jnp.exp(lg * q_pow).T, RET_K, axis=1)
    kdec = jnp.repeat(jnp.exp(lg * k_pow).T, RET_K, axis=1)
    cdec = jnp.broadcast_to(jnp.repeat(jnp.exp(lg * RET_CHUNK), RET_K, axis=0), (RET_HEADS * RET_K, RET_V))
    return intra, qdec, kdec, cdec


def _merge_kernel(x_ref, mod_ref, gain_ref, wg_ref, wb_ref, wo_ref, mla_ref, dif_ref, ret_ref, o_ref):
    x = x_ref[0]
    d = x.shape[1]
    h = _modulated(x, gain_ref[1:2, :], mod_ref[0, 3:4, :], mod_ref[0, 4:5, :]).astype(BF16)
    branches = (lax.dot_general(mla_ref[0], wb_ref[0], _TN, preferred_element_type=F32),
                lax.dot_general(dif_ref[0], wb_ref[1], _TN, preferred_element_type=F32),
                _dot(ret_ref[0], wb_ref[2]))
    y = None
    for j, br in enumerate(branches):
        t = jax.nn.sigmoid(_dot(h, wg_ref[:, j * d:(j + 1) * d])) * br
        y = t if y is None else y + t
    o_ref[0] = x + mod_ref[0, 5:6, :] * _dot(y.astype(BF16), wo_ref[...])


def _merge(x, mod, gain, wg, wb, wo, mla_t, dif_t, ret):
    b, s, d = x.shape
    tm = min(512, s)
    tok = lambda w: pl.BlockSpec((1, tm, w), lambda bi, i: (bi, i, 0))
    feat = pl.BlockSpec((1, BRANCH_W, tm), lambda bi, i: (bi, 0, i))
    return pl.pallas_call(
        _merge_kernel,
        grid=(b, s // tm),
        in_specs=[tok(d), pl.BlockSpec((1, N_MOD, d), lambda bi, i: (bi, 0, 0)), _const(gain.shape),
                  _const(wg.shape), _const(wb.shape), _const(wo.shape), feat, feat, tok(BRANCH_W)],
        out_specs=tok(d),
        out_shape=jax.ShapeDtypeStruct(x.shape, F32),
        compiler_params=_params("parallel", "parallel"),
        name="merge",
    )(x, mod, gain, wg, wb, wo, mla_t, dif_t, ret)


def _rot_cols(w, group):
    r, n = w.shape
    w = w.reshape(r, n // group, 2, group // 2)
    return jnp.stack([-w[:, :, 1], w[:, :, 0]], axis=2).reshape(r, n)


def _head_slots(w, n_heads, width):
    r = w.shape[0]
    w = w.reshape(r, n_heads, width)
    return jnp.pad(w, ((0, 0), (0, 0), (0, LANES - width))).reshape(r, n_heads * LANES)


def _proj_weights(w_in, w_qb, w_kvb, q_norm, kv_norm):
    sizes = (MLA_Q_RANK, MLA_KV_RANK, MLA_ROPE, 2 * DIFF_HEADS * DIFF_QK, 2 * DIFF_HEADS * DIFF_QK,
             DIFF_HEADS * DIFF_V, RET_HEADS * RET_K, RET_HEADS * RET_K, RET_HEADS * RET_V, RET_HEADS * RET_V)
    offs = [0]
    for sz in sizes:
        offs.append(offs[-1] + sz)
    p = [w_in[:, offs[i]:offs[i + 1]] for i in range(len(sizes))]
    gates = w_in[:, offs[-1]:]
    d = w_in.shape[0]
    lead = jnp.zeros((d, MLA_NOPE), F32)
    tail = jnp.zeros((d, LANES - MLA_NOPE - MLA_ROPE), F32)
    main = jnp.concatenate([
        p[0], p[1],
        lead, p[2], tail, lead, _rot_cols(p[2], MLA_ROPE // 2), tail,
        p[3], _rot_cols(p[3], DIFF_QK // 2), p[4], _rot_cols(p[4], DIFF_QK // 2),
        p[6], _rot_cols(p[6], RET_K), p[7], _rot_cols(p[7], RET_K),
        p[8], p[9]], axis=1)
    assert main.shape[1] == _W_MAIN
    hd = MLA_NOPE + MLA_ROPE
    qb = w_qb.reshape(MLA_Q_RANK, MLA_HEADS, hd)
    q_rot = _rot_cols(qb[:, :, MLA_NOPE:].reshape(MLA_Q_RANK, MLA_HEADS * MLA_ROPE), MLA_ROPE // 2)
    q_rot = jnp.concatenate([jnp.zeros((MLA_Q_RANK, MLA_HEADS, MLA_NOPE), F32),
                             q_rot.reshape(MLA_Q_RANK, MLA_HEADS, MLA_ROPE)], axis=2)
    kvb = w_kvb.reshape(MLA_KV_RANK, MLA_HEADS, MLA_NOPE + MLA_V)
    return {
        "main": main.astype(BF16),
        "qa": _head_slots(w_qb, MLA_HEADS, hd).astype(BF16),
        "qb": _head_slots(q_rot.reshape(MLA_Q_RANK, MLA_HEADS * hd), MLA_HEADS, hd).astype(BF16),
        "ka": _head_slots(kvb[:, :, :MLA_NOPE].reshape(MLA_KV_RANK, -1), MLA_HEADS, MLA_NOPE).astype(BF16),
        "vt": kvb[:, :, MLA_NOPE:].reshape(MLA_KV_RANK, -1).T.astype(BF16),
        "dvt": p[5].T.astype(BF16),
        "qn": q_norm.reshape(1, -1),
        "kvn": kv_norm.reshape(1, -1),
    }, gates.astype(BF16)


def _angles(pos, dim):
    inv = ROPE_BASE ** (-jnp.arange(0, dim, 2, dtype=F32) / dim)
    ang = pos.astype(F32)[:, None] * inv[None, :]
    return jnp.cos(ang), jnp.sin(ang)


def _rope_tables(n):
    t = jnp.arange(n)
    row = jnp.repeat(jnp.arange(n // GRID_W), GRID_W)
    col = t - row * GRID_W

    def axial(dim):
        cr, sr = _angles(row, dim // 2)
        cc, sc = _angles(col, dim // 2)
        return jnp.concatenate([cr, cr, cc, cc], axis=1), jnp.concatenate([sr, sr, sc, sc], axis=1)

    cm, sm = axial(MLA_ROPE)
    ones = jnp.ones((n, MLA_NOPE), F32)
    pad = jnp.zeros((n, LANES - MLA_NOPE - MLA_ROPE), F32)
    cm = jnp.concatenate([ones, cm, pad], axis=1)
    sm = jnp.concatenate([0 * ones, sm, pad], axis=1)
    cd, sd = axial(DIFF_QK)
    cs, ss = _angles(t, RET_K)
    return (cm, sm, jnp.tile(cd, (1, 2)), jnp.tile(sd, (1, 2)),
            jnp.tile(cs, (1, 4)), jnp.tile(ss, (1, 4)))


def _identity_tables(n):
    one, zero = jnp.ones((n, LANES), F32), jnp.zeros((n, LANES), F32)
    return one, zero, one, zero, one, zero


def _chunked(w, axis):
    if axis == 1:
        r = w.shape[0]
        return w.reshape(r, D_FF // FF_CHUNK, FF_CHUNK).transpose(1, 0, 2).astype(BF16)
    return w.reshape(D_FF // FF_CHUNK, FF_CHUNK, w.shape[1]).astype(BF16)


def kernel(x, c, ctx, c_ctx, ada_w, ada_b, norm_gain, ffn1_w1, ffn1_w3, ffn1_w2, ffn2_w1, ffn2_w3, ffn2_w2,
           w_in, mla_q_norm, mla_w_qb, mla_kv_norm, mla_w_kvb, diff_lambda, diff_norm, ret_decay, ret_norm,
           w_branch, w_out, final_norm):
    b, n, d = x.shape
    n_ctx = ctx.shape[1]
    pad_rows = -(b + 1) % 8
    cs = jnp.concatenate([c, c_ctx[None, :], jnp.zeros((pad_rows, d), F32)], axis=0)
    mods = _ada(cs, ada_w, ada_b)
    x_tabs, c_tabs = _rope_tables(n), _identity_tables(n_ctx)
    zero_state = jnp.zeros((b, RET_HEADS * RET_K, RET_V), F32)
    xc = ctx
    for l in range(DEPTH):
        lambda_init = 0.8 - 0.6 * math.exp(-0.3 * l)
        ctx_out = l < DEPTH - 1
        mx = mods[l, :b].reshape(b, N_MOD, d)
        mc = jnp.broadcast_to(mods[l, b].reshape(1, N_MOD, d), (b, N_MOD, d))
        gain = norm_gain[l]
        f1 = (_chunked(ffn1_w1[l], 1), _chunked(ffn1_w3[l], 1), _chunked(ffn1_w2[l], 0))
        f2 = (_chunked(ffn2_w1[l], 1), _chunked(ffn2_w3[l], 1), _chunked(ffn2_w2[l], 0))
        pw, wg = _proj_weights(w_in[l], mla_w_qb[l], mla_w_kvb[l], mla_q_norm[l], mla_kv_norm[l])
        wb, wo = w_branch[l].astype(BF16), w_out[l].astype(BF16)
        dl = diff_lambda[l].astype(F32)
        lam = jnp.exp(jnp.sum(dl[0] * dl[1])) - jnp.exp(jnp.sum(dl[2] * dl[3])) + lambda_init
        log_gamma = -jnp.exp(ret_decay[l].astype(F32))
        tab_f, tab_b = _decay_tables(log_gamma[0], False), _decay_tables(log_gamma[1], True)

        x = _ffn(x, mx, gain, *f1, row0=0, gain_row=0)
        xc = _ffn(xc, mc, gain, *f1, row0=0, gain_row=0)

        mq, mk, mvt, dq, dk, dvt, rq, rk, rv, rg = _proj(x, mx, gain, pw, x_tabs)
        cmq, cmk, cmvt, cdq, cdk, cdvt, crq, crk, crv, crg = _proj(xc, mc, gain, pw, c_tabs)

        mla_t = _mla_attention(mq, [(mk, mvt), (cmk, cmvt)])
        dif_t = _diff_attention(dq, [(dk, dvt), (cdk, cdvt)], lam, diff_norm[l], lambda_init)

        oc_f, st_f = _retention(crq, crk, crv, zero_state, tab_f, reverse=False)
        ret_c, st_b = _retention(crq, crk, crv, zero_state, tab_b, reverse=True,
                                 post=(oc_f, crg, ret_norm[l]) if ctx_out else None)
        ox_f, _ = _retention(rq, rk, rv, st_f, tab_f, reverse=False)
        ret_x, _ = _retention(rq, rk, rv, st_b, tab_b, reverse=True, post=(ox_f, rg, ret_norm[l]))

        x = _merge(x, mx, gain, wg, wb, wo, mla_t, dif_t, ret_x)
        x = _ffn(x, mx, gain, *f2, row0=6, gain_row=2, final_gain=None if ctx_out else final_norm)
        if ctx_out:
            mla_c = _mla_attention(cmq, [(cmk, cmvt)])
            dif_c = _diff_attention(cdq, [(cdk, cdvt)], lam, diff_norm[l], lambda_init)
            xc = _merge(xc, mc, gain, wg, wb, wo, mla_c, dif_c, ret_c)
            xc = _ffn(xc, mc, gain, *f2, row0=6, gain_row=2)
    return x
```

```python
import functools
import math

import jax
import jax.numpy as jnp
from jax import lax
from jax.experimental import pallas as pl
from jax.experimental.pallas import tpu as pltpu

D_MODEL = 1024
DEPTH = 2
GRID_W = 64
N_BRANCH = 3
MLA_HEADS = 8
MLA_Q_RANK = 384
MLA_KV_RANK = 256
MLA_NOPE = 64
MLA_ROPE = 32
MLA_V = 64
DIFF_HEADS = 4
DIFF_QK = 64
DIFF_V = 128
RET_HEADS = 4
RET_K = 64
RET_V = 128
RET_CHUNK = 128
BRANCH_W = 512
D_FF = 2816
N_MOD = 9
ROPE_BASE = 10000.0
EPS = 1e-6

LANES = 128
LOG2E = 1.4426950408889634
FF_CHUNK = 256
KEY_CHUNK = 256
VMEM_LIMIT = 56 * 1024 * 1024

BF16 = jnp.bfloat16
F32 = jnp.float32
_NT = (((1,), (1,)), ((), ()))
_TN = (((0,), (0,)), ((), ()))


def _params(*sem):
    return pltpu.CompilerParams(dimension_semantics=sem, vmem_limit_bytes=VMEM_LIMIT)


def _const(shape):
    zeros = (0,) * len(shape)
    return pl.BlockSpec(shape, lambda *_: zeros, pipeline_mode=pl.Buffered(1))


def _rms(x, gain):
    return x * lax.rsqrt(jnp.mean(x * x, axis=-1, keepdims=True) + EPS) * gain


def _modulated(x, gain, shift, scale):
    return _rms(x, gain) * (1.0 + scale) + shift


def _dot(a, b):
    return jnp.dot(a, b, preferred_element_type=F32)


def _ada_kernel(c_ref, w_ref, b_ref, o_ref):
    c = c_ref[...]
    o_ref[0] = _dot(c * jax.nn.sigmoid(c), w_ref[0]) + b_ref[0]


def _ada(cs, ada_w, ada_b):
    nl, d, n = ada_w.shape
    r = cs.shape[0]
    tn = n // N_MOD
    return pl.pallas_call(
        _ada_kernel,
        grid=(nl, n // tn),
        in_specs=[pl.BlockSpec((r, d), lambda l, j: (0, 0)),
                  pl.BlockSpec((1, d, tn), lambda l, j: (l, 0, j)),
                  pl.BlockSpec((1, 1, tn), lambda l, j: (l, 0, j))],
        out_specs=pl.BlockSpec((1, r, tn), lambda l, j: (l, 0, j)),
        out_shape=jax.ShapeDtypeStruct((nl, r, n), F32),
        compiler_params=_params("parallel", "parallel"),
        name="ada",
    )(cs, ada_w, ada_b.reshape(nl, 1, n))


def _ffn_kernel(x_ref, mod_ref, gain_ref, w1_ref, w3_ref, w2_ref, *rest, row0, gain_row, final):
    if final:
        fg_ref, o_ref, acc_ref = rest
    else:
        o_ref, acc_ref = rest
    x = x_ref[0]
    h = _modulated(x, gain_ref[gain_row:gain_row + 1, :], mod_ref[0, row0:row0 + 1, :],
                   mod_ref[0, row0 + 1:row0 + 2, :]).astype(BF16)
    acc_ref[...] = jnp.zeros_like(acc_ref)

    def body(c, carry):
        a = _dot(h, w1_ref[c])
        b = _dot(h, w3_ref[c])
        g = (a * jax.nn.sigmoid(a) * b).astype(BF16)
        acc_ref[...] += _dot(g, w2_ref[c])
        return carry

    lax.fori_loop(0, w1_ref.shape[0], body, 0)
    y = x + (mod_ref[0, row0 + 2:row0 + 3, :] * 0.5) * acc_ref[...]
    if final:
        y = _rms(y, fg_ref[...])
    o_ref[0] = y


def _ffn(x, mod, gain, w1, w3, w2, *, row0, gain_row, final_gain=None):
    b, s, d = x.shape
    tm = min(512, s)
    nc = w1.shape[0]
    final = final_gain is not None
    in_specs = [pl.BlockSpec((1, tm, d), lambda bi, i: (bi, i, 0)),
                pl.BlockSpec((1, N_MOD, d), lambda bi, i: (bi, 0, 0)),
                _const(gain.shape), _const(w1.shape), _const(w3.shape), _const(w2.shape)]
    args = [x, mod, gain, w1, w3, w2]
    if final:
        in_specs.append(_const((1, d)))
        args.append(final_gain.reshape(1, d))
    return pl.pallas_call(
        functools.partial(_ffn_kernel, row0=row0, gain_row=gain_row, final=final),
        grid=(b, s // tm),
        in_specs=in_specs,
        out_specs=pl.BlockSpec((1, tm, d), lambda bi, i: (bi, i, 0)),
        out_shape=jax.ShapeDtypeStruct(x.shape, F32),
        scratch_shapes=[pltpu.VMEM((tm, d), F32)],
        compiler_params=_params("parallel", "parallel"),
        name="ffn",
    )(*args)


_O_QLAT = 0
_O_KVLAT = _O_QLAT + MLA_Q_RANK
_O_KR = _O_KVLAT + MLA_KV_RANK
_O_DQ = _O_KR + 2 * LANES
_O_RQ = _O_DQ + 4 * BRANCH_W
_O_RV = _O_RQ + 4 * RET_HEADS * RET_K
_W_MAIN = _O_RV + 2 * BRANCH_W


def _proj_kernel(x_ref, mod_ref, gain_ref, wm_ref, wqa_ref, wqb_ref, wka_ref, wvt_ref, wdvt_ref,
                 qn_ref, kvn_ref, cm_ref, sm_ref, cd_ref, sd_ref, cr_ref, sr_ref,
                 mq_ref, mk_ref, mvt_ref, dq_ref, dk_ref, dvt_ref, rq_ref, rk_ref, rv_ref, rg_ref):
    x = x_ref[0]
    h = _modulated(x, gain_ref[1:2, :], mod_ref[0, 3:4, :], mod_ref[0, 4:5, :]).astype(BF16)

    def seg(lo, width):
        return _dot(h, wm_ref[:, lo:lo + width])

    qlat = _rms(seg(_O_QLAT, MLA_Q_RANK), qn_ref[...]).astype(BF16)
    kvlat = _rms(seg(_O_KVLAT, MLA_KV_RANK), kvn_ref[...]).astype(BF16)
    cm, sm = cm_ref[...], sm_ref[...]
    qa = _dot(qlat, wqa_ref[...])
    qb = _dot(qlat, wqb_ref[...])
    ka = _dot(kvlat, wka_ref[...])
    kr = seg(_O_KR, 2 * LANES)
    krr = kr[:, :LANES] * cm + kr[:, LANES:] * sm
    q_scale = (MLA_NOPE + MLA_ROPE) ** -0.5 * LOG2E
    for hd in range(MLA_HEADS):
        sl = slice(hd * LANES, (hd + 1) * LANES)
        mq_ref[0, :, sl] = ((qa[:, sl] * cm + qb[:, sl] * sm) * q_scale).astype(BF16)
        mk_ref[0, :, sl] = (ka[:, sl] + krr).astype(BF16)
    mvt_ref[0, 0] = lax.dot_general(wvt_ref[...], kvlat, _NT, preferred_element_type=F32).astype(BF16)

    cd, sd = cd_ref[...], sd_ref[...]
    dq = seg(_O_DQ, 2 * BRANCH_W)
    dk = seg(_O_DQ + 2 * BRANCH_W, 2 * BRANCH_W)
    d_scale = DIFF_QK ** -0.5 * LOG2E
    for j in range(BRANCH_W // LANES):
        sl = slice(j * LANES, (j + 1) * LANES)
        sr = slice(BRANCH_W + j * LANES, BRANCH_W + (j + 1) * LANES)
        dq_ref[0, :, sl] = ((dq[:, sl] * cd + dq[:, sr] * sd) * d_scale).astype(BF16)
        dk_ref[0, :, sl] = (dk[:, sl] * cd + dk[:, sr] * sd).astype(BF16)
    dvt_ref[0, 0] = lax.dot_general(wdvt_ref[...], h, _NT, preferred_element_type=F32).astype(BF16)

    cr, sr_t = cr_ref[...], sr_ref[...]
    hk = RET_HEADS * RET_K
    rqk = seg(_O_RQ, 4 * hk)
    for j in range(hk // LANES):
        sl = slice(j * LANES, (j + 1) * LANES)
        rq_ref[0, :, sl] = (rqk[:, sl] * cr + rqk[:, hk + j * LANES:hk + (j + 1) * LANES] * sr_t).astype(BF16)
        rk_ref[0, :, sl] = ((rqk[:, 2 * hk + j * LANES:2 * hk + (j + 1) * LANES] * cr
                             + rqk[:, 3 * hk + j * LANES:3 * hk + (j + 1) * LANES] * sr_t)
                            * RET_K ** -0.5).astype(BF16)
    rv_ref[0] = seg(_O_RV, BRANCH_W).astype(BF16)
    rg_ref[0] = seg(_O_RV + BRANCH_W, BRANCH_W).astype(BF16)


def _proj(x, mod, gain, pw, tables):
    b, s, d = x.shape
    tm = KEY_CHUNK
    nt = s // tm
    tok = lambda w: pl.BlockSpec((1, tm, w), lambda bi, i: (bi, i, 0))
    slab = pl.BlockSpec((1, 1, BRANCH_W, tm), lambda bi, i: (bi, i, 0, 0))
    tab = pl.BlockSpec((tm, LANES), lambda bi, i: (i, 0))
    weights = [pw["main"], pw["qa"], pw["qb"], pw["ka"], pw["vt"], pw["dvt"], pw["qn"], pw["kvn"]]
    hk = RET_HEADS * RET_K
    widths = [MLA_HEADS * LANES, MLA_HEADS * LANES, None, BRANCH_W, BRANCH_W, None, hk, hk, BRANCH_W, BRANCH_W]
    out_specs, out_shape = [], []
    for w in widths:
        if w is None:
            out_specs.append(slab)
            out_shape.append(jax.ShapeDtypeStruct((b, nt, BRANCH_W, tm), BF16))
        else:
            out_specs.append(tok(w))
            out_shape.append(jax.ShapeDtypeStruct((b, s, w), BF16))
    return pl.pallas_call(
        _proj_kernel,
        grid=(b, nt),
        in_specs=[pl.BlockSpec((1, tm, d), lambda bi, i: (bi, i, 0)),
                  pl.BlockSpec((1, N_MOD, d), lambda bi, i: (bi, 0, 0)),
                  _const(gain.shape)] + [_const(w.shape) for w in weights] + [tab] * 6,
        out_specs=out_specs,
        out_shape=out_shape,
        compiler_params=_params("parallel", "parallel"),
        name="proj",
    )(x, mod, gain, *weights, *tables)


MLA_GROUP = 4
SUM_ROWS = 16


def _attend(chains, segments, dv):
    n = chains[0][0].shape[0]
    ones = jnp.ones((SUM_ROWS, KEY_CHUNK), BF16)

    carry = tuple((jnp.full((1, n), -jnp.inf, F32), jnp.zeros((dv + SUM_ROWS, n), F32)) for _ in chains)
    for k_ref, vt_ref, n_keys in segments:
        per = 2 if n_keys % (2 * KEY_CHUNK) == 0 else 1
        tk = per * KEY_CHUNK

        def body(c, carry, k_ref=k_ref, vt_ref=vt_ref, per=per, tk=tk):
            start = c * tk if isinstance(c, int) else pl.multiple_of(c * tk, tk)
            scores = [lax.dot_general(k_ref[0, pl.ds(start, tk), lanes], q_rhs, _NT, preferred_element_type=F32)
                      for q_rhs, lanes, _ in chains]
            out = []
            for (_, _, rows), (m, acc), s in zip(chains, carry, scores):
                m_new = jnp.maximum(m, jnp.max(s, axis=0, keepdims=True))
                alpha = jnp.exp2(m - m_new)
                pb = jnp.exp2(s - m_new).astype(BF16)
                pv = None
                for j in range(per):
                    vt = jnp.concatenate([vt_ref[0, c * per + j, rows, :], ones], axis=0)
                    t = _dot(vt, pb[j * KEY_CHUNK:(j + 1) * KEY_CHUNK, :])
                    pv = t if pv is None else pv + t
                out.append((m_new, alpha * acc + pv))
            return tuple(out)

        if n_keys // tk == 1:
            carry = body(0, carry)
        else:
            carry = lax.fori_loop(0, n_keys // tk, body, carry)
    return [(acc[:dv], acc[dv:dv + 1]) for _, acc in carry]


def _kv_specs(segments, k_width, v_rows, index):
    specs, args = [], []
    for k, vt in segments:
        n = k.shape[1]
        specs.append(pl.BlockSpec((1, n, k_width), lambda *g: (g[0], 0, index(g)), pipeline_mode=pl.Buffered(1)))
        specs.append(pl.BlockSpec((1, n // KEY_CHUNK, v_rows, KEY_CHUNK), lambda *g: (g[0], 0, index(g), 0),
                                  pipeline_mode=pl.Buffered(1)))
        args += [k, vt]
    return specs, args


def _mla_kernel(q_ref, *refs, seg_lens):
    o_ref = refs[-1]
    segments = [(refs[2 * i], refs[2 * i + 1], n) for i, n in enumerate(seg_lens)]
    chains = []
    for g in range(MLA_GROUP):
        lanes = slice(g * LANES, (g + 1) * LANES)
        chains.append((q_ref[0, :, lanes], lanes, slice(g * MLA_V, (g + 1) * MLA_V)))
    for (_, _, rows), (acc, l) in zip(chains, _attend(chains, segments, MLA_V)):
        o_ref[0, rows, :] = (acc / l).astype(BF16)


def _mla_attention(q, segments):
    b, s, _ = q.shape
    tq = min(512, s)
    kv_specs, kv_args = _kv_specs(segments, MLA_GROUP * LANES, MLA_GROUP * MLA_V, lambda g: g[1])
    return pl.pallas_call(
        functools.partial(_mla_kernel, seg_lens=tuple(k.shape[1] for k, _ in segments)),
        grid=(b, MLA_HEADS // MLA_GROUP, s // tq),
        in_specs=[pl.BlockSpec((1, tq, MLA_GROUP * LANES), lambda bi, hg, i: (bi, i, hg))] + kv_specs,
        out_specs=pl.BlockSpec((1, MLA_GROUP * MLA_V, tq), lambda bi, hg, i: (bi, hg, i)),
        out_shape=jax.ShapeDtypeStruct((b, MLA_HEADS * MLA_V, s), BF16),
        compiler_params=_params("parallel", "parallel", "parallel"),
        name="mla_attn",
    )(q, *kv_args)


def _diff_kernel(lam_ref, q_ref, *refs, seg_lens, out_scale):
    gain_ref, o_ref = refs[-2], refs[-1]
    segments = [(refs[2 * i], refs[2 * i + 1], n) for i, n in enumerate(seg_lens)]
    tq = q_ref.shape[1]
    first = lax.broadcasted_iota(jnp.int32, (tq, LANES), 1) < DIFF_QK
    zero = jnp.zeros((tq, LANES), BF16)
    chains = []
    for hd in range(DIFF_HEADS):
        lanes = slice(hd * LANES, (hd + 1) * LANES)
        q = q_ref[0, :, lanes]
        q_rhs = jnp.concatenate([jnp.where(first, q, zero), jnp.where(first, zero, q)], axis=0)
        chains.append((q_rhs, lanes, slice(hd * DIFF_V, (hd + 1) * DIFF_V)))
    for (_, _, rows), (acc, l) in zip(chains, _attend(chains, segments, DIFF_V)):
        o = acc / l
        o = o[:, :tq] - lam_ref[0] * o[:, tq:]
        o = o * lax.rsqrt(jnp.mean(o * o, axis=0, keepdims=True) + EPS) * gain_ref[...]
        o_ref[0, rows, :] = (o * out_scale).astype(BF16)


def _diff_attention(q, segments, lam, gain, lambda_init):
    b, s, w = q.shape
    tq = min(256, s)
    kv_specs, kv_args = _kv_specs(segments, w, DIFF_HEADS * DIFF_V, lambda g: 0)
    return pl.pallas_call(
        functools.partial(_diff_kernel, seg_lens=tuple(k.shape[1] for k, _ in segments),
                          out_scale=1.0 - lambda_init),
        grid=(b, s // tq),
        in_specs=[pl.BlockSpec(memory_space=pltpu.SMEM),
                  pl.BlockSpec((1, tq, w), lambda bi, i: (bi, i, 0))] + kv_specs
                 + [pl.BlockSpec((DIFF_V, 1), lambda bi, i: (0, 0))],
        out_specs=pl.BlockSpec((1, DIFF_HEADS * DIFF_V, tq), lambda bi, i: (bi, 0, i)),
        out_shape=jax.ShapeDtypeStruct((b, DIFF_HEADS * DIFF_V, s), BF16),
        compiler_params=_params("parallel", "parallel"),
        name="diff_attn",
    )(lam.reshape(1), q, *kv_args, gain.reshape(DIFF_V, 1))


def _ret_kernel(q_ref, k_ref, v_ref, s0_ref, intra_ref, qdec_ref, kdec_ref, cdec_ref, *rest, reverse, post):
    if post:
        of_ref, g_ref, gain_ref, o_ref, sout_ref, state = rest
    else:
        o_ref, sout_ref, state = rest
    i = pl.program_id(1)

    @pl.when(i == 0)
    def _():
        state[...] = s0_ref[0]

    n_chunks = q_ref.shape[1] // RET_CHUNK
    lane = lax.broadcasted_iota(jnp.int32, (RET_CHUNK, RET_HEADS * RET_K), 1)
    for cc in range(n_chunks):
        c = n_chunks - 1 - cc if reverse else cc
        rows = slice(c * RET_CHUNK, (c + 1) * RET_CHUNK)
        qc, kc, vc = q_ref[0, rows, :], k_ref[0, rows, :], v_ref[0, rows, :]
        st = state[...]
        stb = st.astype(BF16)
        qd = (qc.astype(F32) * qdec_ref[...]).astype(BF16)
        kdt = (kc.astype(F32) * kdec_ref[...]).T.astype(BF16)
        outs, deltas = [], []
        for hd in range(RET_HEADS):
            mine = (lane >= hd * RET_K) & (lane < (hd + 1) * RET_K)
            vh = vc[:, hd * RET_V:(hd + 1) * RET_V]
            a = lax.dot_general(jnp.where(mine, qc, jnp.zeros_like(qc)), kc, _NT,
                                preferred_element_type=F32) * intra_ref[hd]
            o_h = _dot(a.astype(BF16), vh) + _dot(jnp.where(mine, qd, jnp.zeros_like(qd)), stb)
            outs.append(o_h)
            deltas.append(_dot(kdt[hd * RET_K:(hd + 1) * RET_K, :], vh))
        state[...] = st * cdec_ref[...] + jnp.concatenate(deltas, axis=0)
        if post:
            of = of_ref[0, rows, :]
            g = g_ref[0, rows, :].astype(F32)
            for hd in range(RET_HEADS):
                sl = slice(hd * RET_V, (hd + 1) * RET_V)
                outs[hd] = _rms(outs[hd] + of[:, sl], gain_ref[...]) * (g[:, sl] * jax.nn.sigmoid(g[:, sl]))
        o_ref[0, rows, :] = jnp.concatenate(outs, axis=1).astype(o_ref.dtype)

    @pl.when(i == pl.num_programs(1) - 1)
    def _():
        sout_ref[0] = state[...]


def _retention(q, k, v, s0, tabs, *, reverse, post=None):
    b, n, _ = q.shape
    t = min(512, n)
    nt = n // t
    hk, hv = RET_HEADS * RET_K, RET_HEADS * RET_V
    tile = (lambda bi, i: (bi, nt - 1 - i, 0)) if reverse else (lambda bi, i: (bi, i, 0))
    st_spec = pl.BlockSpec((1, hk, RET_V), lambda bi, i: (bi, 0, 0))
    in_specs = [pl.BlockSpec((1, t, hk), tile), pl.BlockSpec((1, t, hk), tile), pl.BlockSpec((1, t, hv), tile),
                st_spec] + [_const(a.shape) for a in tabs]
    args = [q, k, v, s0, *tabs]
    if post is not None:
        o_f, g, gain = post
        in_specs += [pl.BlockSpec((1, t, hv), tile), pl.BlockSpec((1, t, hv), tile), _const((1, RET_V))]
        args += [o_f, g, gain.reshape(1, RET_V)]
    return pl.pallas_call(
        functools.partial(_ret_kernel, reverse=reverse, post=post is not None),
        grid=(b, nt),
        in_specs=in_specs,
        out_specs=[pl.BlockSpec((1, t, hv), tile), st_spec],
        out_shape=[jax.ShapeDtypeStruct((b, n, hv), F32 if post is None else BF16),
                   jax.ShapeDtypeStruct((b, hk, RET_V), F32)],
        scratch_shapes=[pltpu.VMEM((hk, RET_V), F32)],
        compiler_params=_params("parallel", "arbitrary"),
        name="ret_bwd" if reverse else "ret_fwd",
    )(*args)


def _decay_tables(log_gamma, reverse):
    idx = jnp.arange(RET_CHUNK, dtype=F32)
    lg = log_gamma.astype(F32)[:, None]
    rel = idx[:, None] - idx[None, :]
    if reverse:
        rel = -rel
        q_pow, k_pow = RET_CHUNK - idx, idx
    else:
        q_pow, k_pow = idx + 1.0, RET_CHUNK - 1.0 - idx
    intra = jnp.where(rel >= 0, jnp.exp(lg[:, :, None] * jnp.maximum(rel, 0.0)), 0.0)
    qdec = jnp.repeat(jnp.exp(lg * q_pow).T, RET_K, axis=1)
    kdec = jnp.repeat(jnp.exp(lg * k_pow).T, RET_K, axis=1)
    cdec = jnp.broadcast_to(jnp.repeat(jnp.exp(lg * RET_CHUNK), RET_K, axis=0), (RET_HEADS * RET_K, RET_V))
    return intra, qdec, kdec, cdec


def _merge_kernel(x_ref, mod_ref, gain_ref, wg_ref, wb_ref, wo_ref, mla_ref, dif_ref, ret_ref, o_ref):
    x = x_ref[0]
    d = x.shape[1]
    h = _modulated(x, gain_ref[1:2, :], mod_ref[0, 3:4, :], mod_ref[0, 4:5, :]).astype(BF16)
    branches = (lax.dot_general(mla_ref[0], wb_ref[0], _TN, preferred_element_type=F32),
                lax.dot_general(dif_ref[0], wb_ref[1], _TN, preferred_element_type=F32),
                _dot(ret_ref[0], wb_ref[2]))
    y = None
    for j, br in enumerate(branches):
        t = jax.nn.sigmoid(_dot(h, wg_ref[:, j * d:(j + 1) * d])) * br
        y = t if y is None else y + t
    o_ref[0] = x + mod_ref[0, 5:6, :] * _dot(y.astype(BF16), wo_ref[...])


def _merge(x, mod, gain, wg, wb, wo, mla_t, dif_t, ret):
    b, s, d = x.shape
    tm = min(512, s)
    tok = lambda w: pl.BlockSpec((1, tm, w), lambda bi, i: (bi, i, 0))
    feat = pl.BlockSpec((1, BRANCH_W, tm), lambda bi, i: (bi, 0, i))
    return pl.pallas_call(
        _merge_kernel,
        grid=(b, s // tm),
        in_specs=[tok(d), pl.BlockSpec((1, N_MOD, d), lambda bi, i: (bi, 0, 0)), _const(gain.shape),
                  _const(wg.shape), _const(wb.shape), _const(wo.shape), feat, feat, tok(BRANCH_W)],
        out_specs=tok(d),
        out_shape=jax.ShapeDtypeStruct(x.shape, F32),
        compiler_params=_params("parallel", "parallel"),
        name="merge",
    )(x, mod, gain, wg, wb, wo, mla_t, dif_t, ret)


def _rot_cols(w, group):
    r, n = w.shape
    w = w.reshape(r, n // group, 2, group // 2)
    return jnp.stack([-w[:, :, 1], w[:, :, 0]], axis=2).reshape(r, n)


def _head_slots(w, n_heads, width):
    r = w.shape[0]
    w = w.reshape(r, n_heads, width)
    return jnp.pad(w, ((0, 0), (0, 0), (0, LANES - width))).reshape(r, n_heads * LANES)


def _proj_weights(w_in, w_qb, w_kvb, q_norm, kv_norm):
    sizes = (MLA_Q_RANK, MLA_KV_RANK, MLA_ROPE, 2 * DIFF_HEADS * DIFF_QK, 2 * DIFF_HEADS * DIFF_QK,
             DIFF_HEADS * DIFF_V, RET_HEADS * RET_K, RET_HEADS * RET_K, RET_HEADS * RET_V, RET_HEADS * RET_V)
    offs = [0]
    for sz in sizes:
        offs.append(offs[-1] + sz)
    p = [w_in[:, offs[i]:offs[i + 1]] for i in range(len(sizes))]
    gates = w_in[:, offs[-1]:]
    d = w_in.shape[0]
    lead = jnp.zeros((d, MLA_NOPE), F32)
    tail = jnp.zeros((d, LANES - MLA_NOPE - MLA_ROPE), F32)
    main = jnp.concatenate([
        p[0], p[1],
        lead, p[2], tail, lead, _rot_cols(p[2], MLA_ROPE // 2), tail,
        p[3], _rot_cols(p[3], DIFF_QK // 2), p[4], _rot_cols(p[4], DIFF_QK // 2),
        p[6], _rot_cols(p[6], RET_K), p[7], _rot_cols(p[7], RET_K),
        p[8], p[9]], axis=1)
    assert main.shape[1] == _W_MAIN
    hd = MLA_NOPE + MLA_ROPE
    qb = w_qb.reshape(MLA_Q_RANK, MLA_HEADS, hd)
    q_rot = _rot_cols(qb[:, :, MLA_NOPE:].reshape(MLA_Q_RANK, MLA_HEADS * MLA_ROPE), MLA_ROPE // 2)
    q_rot = jnp.concatenate([jnp.zeros((MLA_Q_RANK, MLA_HEADS, MLA_NOPE), F32),
                             q_rot.reshape(MLA_Q_RANK, MLA_HEADS, MLA_ROPE)], axis=2)
    kvb = w_kvb.reshape(MLA_KV_RANK, MLA_HEADS, MLA_NOPE + MLA_V)
    return {
        "main": main.astype(BF16),
        "qa": _head_slots(w_qb, MLA_HEADS, hd).astype(BF16),
        "qb": _head_slots(q_rot.reshape(MLA_Q_RANK, MLA_HEADS * hd), MLA_HEADS, hd).astype(BF16),
        "ka": _head_slots(kvb[:, :, :MLA_NOPE].reshape(MLA_KV_RANK, -1), MLA_HEADS, MLA_NOPE).astype(BF16),
        "vt": kvb[:, :, MLA_NOPE:].reshape(MLA_KV_RANK, -1).T.astype(BF16),
        "dvt": p[5].T.astype(BF16),
        "qn": q_norm.reshape(1, -1),
        "kvn": kv_norm.reshape(1, -1),
    }, gates.astype(BF16)


def _angles(pos, dim):
    inv = ROPE_BASE ** (-jnp.arange(0, dim, 2, dtype=F32) / dim)
    ang = pos.astype(F32)[:, None] * inv[None, :]
    return jnp.cos(ang), jnp.sin(ang)


def _rope_tables(n):
    t = jnp.arange(n)
    row = jnp.repeat(jnp.arange(n // GRID_W), GRID_W)
    col = t - row * GRID_W

    def axial(dim):
        cr, sr = _angles(row, dim // 2)
        cc, sc = _angles(col, dim // 2)
        return jnp.concatenate([cr, cr, cc, cc], axis=1), jnp.concatenate([sr, sr, sc, sc], axis=1)

    cm, sm = axial(MLA_ROPE)
    ones = jnp.ones((n, MLA_NOPE), F32)
    pad = jnp.zeros((n, LANES - MLA_NOPE - MLA_ROPE), F32)
    cm = jnp.concatenate([ones, cm, pad], axis=1)
    sm = jnp.concatenate([0 * ones, sm, pad], axis=1)
    cd, sd = axial(DIFF_QK)
    cs, ss = _angles(t, RET_K)
    return (cm, sm, jnp.tile(cd, (1, 2)), jnp.tile(sd, (1, 2)),
            jnp.tile(cs, (1, 4)), jnp.tile(ss, (1, 4)))


def _identity_tables(n):
    one, zero = jnp.ones((n, LANES), F32), jnp.zeros((n, LANES), F32)
    return one, zero, one, zero, one, zero


def _chunked(w, axis):
    if axis == 1:
        r = w.shape[0]
        return w.reshape(r, D_FF // FF_CHUNK, FF_CHUNK).transpose(1, 0, 2).astype(BF16)
    return w.reshape(D_FF // FF_CHUNK, FF_CHUNK, w.shape[1]).astype(BF16)


def kernel(x, c, ctx, c_ctx, ada_w, ada_b, norm_gain, ffn1_w1, ffn1_w3, ffn1_w2, ffn2_w1, ffn2_w3, ffn2_w2,
           w_in, mla_q_norm, mla_w_qb, mla_kv_norm, mla_w_kvb, diff_lambda, diff_norm, ret_decay, ret_norm,
           w_branch, w_out, final_norm):
    b, n, d = x.shape
    n_ctx = ctx.shape[1]
    pad_rows = -(b + 1) % 8
    cs = jnp.concatenate([c, c_ctx[None, :], jnp.zeros((pad_rows, d), F32)], axis=0)
    mods = _ada(cs, ada_w, ada_b)
    x_tabs, c_tabs = _rope_tables(n), _identity_tables(n_ctx)
    zero_state = jnp.zeros((b, RET_HEADS * RET_K, RET_V), F32)
    xc = ctx
    for l in range(DEPTH):
        lambda_init = 0.8 - 0.6 * math.exp(-0.3 * l)
        ctx_out = l < DEPTH - 1
        mx = mods[l, :b].reshape(b, N_MOD, d)
        mc = jnp.broadcast_to(mods[l, b].reshape(1, N_MOD, d), (b, N_MOD, d))
        gain = norm_gain[l]
        f1 = (_chunked(ffn1_w1[l], 1), _chunked(ffn1_w3[l], 1), _chunked(ffn1_w2[l], 0))
        f2 = (_chunked(ffn2_w1[l], 1), _chunked(ffn2_w3[l], 1), _chunked(ffn2_w2[l], 0))
        pw, wg = _proj_weights(w_in[l], mla_w_qb[l], mla_w_kvb[l], mla_q_norm[l], mla_kv_norm[l])
        wb, wo = w_branch[l].astype(BF16), w_out[l].astype(BF16)
        dl = diff_lambda[l].astype(F32)
        lam = jnp.exp(jnp.sum(dl[0] * dl[1])) - jnp.exp(jnp.sum(dl[2] * dl[3])) + lambda_init
        log_gamma = -jnp.exp(ret_decay[l].astype(F32))
        tab_f, tab_b = _decay_tables(log_gamma[0], False), _decay_tables(log_gamma[1], True)

        x = _ffn(x, mx, gain, *f1, row0=0, gain_row=0)
        xc = _ffn(xc, mc, gain, *f1, row0=0, gain_row=0)

        mq, mk, mvt, dq, dk, dvt, rq, rk, rv, rg = _proj(x, mx, gain, pw, x_tabs)
        cmq, cmk, cmvt, cdq, cdk, cdvt, crq, crk, crv, crg = _proj(xc, mc, gain, pw, c_tabs)

        mla_t = _mla_attention(mq, [(mk, mvt), (cmk, cmvt)])
        dif_t = _diff_attention(dq, [(dk, dvt), (cdk, cdvt)], lam, diff_norm[l], lambda_init)

        oc_f, st_f = _retention(crq, crk, crv, zero_state, tab_f, reverse=False)
        ret_c, st_b = _retention(crq, crk, crv, zero_state, tab_b, reverse=True,
                                 post=(oc_f, crg, ret_norm[l]) if ctx_out else None)
        ox_f, _ = _retention(rq, rk, rv, st_f, tab_f, reverse=False)
        ret_x, _ = _retention(rq, rk, rv, st_b, tab_b, reverse=True, post=(ox_f, rg, ret_norm[l]))

        x = _merge(x, mx, gain, wg, wb, wo, mla_t, dif_t, ret_x)
        x = _ffn(x, mx, gain, *f2, row0=6, gain_row=2, final_gain=None if ctx_out else final_norm)
        if ctx_out:
            mla_c = _mla_attention(cmq, [(cmk, cmvt)])
            dif_c = _diff_attention(cdq, [(cdk, cdvt)], lam, diff_norm[l], lambda_init)
            xc = _merge(xc, mc, gain, wg, wb, wo, mla_c, dif_c, ret_c)
            xc = _ffn(xc, mc, gain, *f2, row0=6, gain_row=2)
    return x
```

```python
import functools
import math

import jax
import jax.numpy as jnp
from jax import lax
from jax.experimental import pallas as pl
from jax.experimental.pallas import tpu as pltpu

D_MODEL = 1024
DEPTH = 2
GRID_W = 64
N_BRANCH = 3
MLA_HEADS = 8
MLA_Q_RANK = 384
MLA_KV_RANK = 256
MLA_NOPE = 64
MLA_ROPE = 32
MLA_V = 64
DIFF_HEADS = 4
DIFF_QK = 64
DIFF_V = 128
RET_HEADS = 4
RET_K = 64
RET_V = 128
RET_CHUNK = 128
BRANCH_W = 512
D_FF = 2816
N_MOD = 9
ROPE_BASE = 10000.0
EPS = 1e-6

LANES = 128
LOG2E = 1.4426950408889634
FF_CHUNK = 256
KEY_CHUNK = 256
VMEM_LIMIT = 56 * 1024 * 1024

BF16 = jnp.bfloat16
F32 = jnp.float32
_NT = (((1,), (1,)), ((), ()))
_TN = (((0,), (0,)), ((), ()))


def _params(*sem):
    return pltpu.CompilerParams(dimension_semantics=sem, vmem_limit_bytes=VMEM_LIMIT)


def _const(shape):
    zeros = (0,) * len(shape)
    return pl.BlockSpec(shape, lambda *_: zeros, pipeline_mode=pl.Buffered(1))


def _rms(x, gain):
    return x * lax.rsqrt(jnp.mean(x * x, axis=-1, keepdims=True) + EPS) * gain


def _modulated(x, gain, shift, scale):
    return _rms(x, gain) * (1.0 + scale) + shift


def _dot(a, b):
    return jnp.dot(a, b, preferred_element_type=F32)


def _ada_kernel(c_ref, w_ref, b_ref, o_ref):
    c = c_ref[...]
    o_ref[0] = _dot(c * jax.nn.sigmoid(c), w_ref[0]) + b_ref[0]


def _ada(cs, ada_w, ada_b):
    nl, d, n = ada_w.shape
    r = cs.shape[0]
    tn = n // N_MOD
    return pl.pallas_call(
        _ada_kernel,
        grid=(nl, n // tn),
        in_specs=[pl.BlockSpec((r, d), lambda l, j: (0, 0)),
                  pl.BlockSpec((1, d, tn), lambda l, j: (l, 0, j)),
                  pl.BlockSpec((1, 1, tn), lambda l, j: (l, 0, j))],
        out_specs=pl.BlockSpec((1, r, tn), lambda l, j: (l, 0, j)),
        out_shape=jax.ShapeDtypeStruct((nl, r, n), F32),
        compiler_params=_params("parallel", "parallel"),
        name="ada",
    )(cs, ada_w, ada_b.reshape(nl, 1, n))


def _ffn_kernel(x_ref, mod_ref, gain_ref, w1_ref, w3_ref, w2_ref, *rest, row0, gain_row, final):
    if final:
        fg_ref, o_ref, acc_ref = rest
    else:
        o_ref, acc_ref = rest
    x = x_ref[0]
    h = _modulated(x, gain_ref[gain_row:gain_row + 1, :], mod_ref[0, row0:row0 + 1, :],
                   mod_ref[0, row0 + 1:row0 + 2, :]).astype(BF16)
    n_chunks = w1_ref.shape[0]
    up = (_dot(h, w1_ref[0]), _dot(h, w3_ref[0]))
    for c in range(n_chunks):
        a, b = up
        if c + 1 < n_chunks:
            up = (_dot(h, w1_ref[c + 1]), _dot(h, w3_ref[c + 1]))
        t = _dot((a * jax.nn.sigmoid(a) * b).astype(BF16), w2_ref[c])
        if c == 0:
            acc_ref[...] = t
        else:
            acc_ref[...] += t
    y = x + (mod_ref[0, row0 + 2:row0 + 3, :] * 0.5) * acc_ref[...]
    if final:
        y = _rms(y, fg_ref[...])
    o_ref[0] = y


def _ffn(x, mod, gain, w1, w3, w2, *, row0, gain_row, final_gain=None):
    b, s, d = x.shape
    tm = min(512, s)
    nc = w1.shape[0]
    final = final_gain is not None
    in_specs = [pl.BlockSpec((1, tm, d), lambda bi, i: (bi, i, 0)),
                pl.BlockSpec((1, N_MOD, d), lambda bi, i: (bi, 0, 0)),
                _const(gain.shape), _const(w1.shape), _const(w3.shape), _const(w2.shape)]
    args = [x, mod, gain, w1, w3, w2]
    if final:
        in_specs.append(_const((1, d)))
        args.append(final_gain.reshape(1, d))
    return pl.pallas_call(
        functools.partial(_ffn_kernel, row0=row0, gain_row=gain_row, final=final),
        grid=(b, s // tm),
        in_specs=in_specs,
        out_specs=pl.BlockSpec((1, tm, d), lambda bi, i: (bi, i, 0)),
        out_shape=jax.ShapeDtypeStruct(x.shape, F32),
        scratch_shapes=[pltpu.VMEM((tm, d), F32)],
        compiler_params=_params("parallel", "parallel"),
        name="ffn",
    )(*args)


_O_QLAT = 0
_O_KVLAT = _O_QLAT + MLA_Q_RANK
_O_KR = _O_KVLAT + MLA_KV_RANK
_O_DQ = _O_KR + 2 * LANES
_O_RQ = _O_DQ + 4 * BRANCH_W
_O_RV = _O_RQ + 4 * RET_HEADS * RET_K
_W_MAIN = _O_RV + 2 * BRANCH_W


_KEY_OUTPUTS = (1, 2, 4, 5)


def _proj_kernel(x_ref, mod_ref, gain_ref, wm_ref, wqa_ref, wqb_ref, wka_ref, wvt_ref, wdvt_ref,
                 qn_ref, kvn_ref, cm_ref, sm_ref, cd_ref, sd_ref, cr_ref, sr_ref, *rest):
    mq_ref, mk_ref, mvt_ref, dq_ref, dk_ref, dvt_ref, rq_ref, rk_ref, rv_ref, rg_ref = rest[-10:]
    x = x_ref[0]
    h = _modulated(x, gain_ref[1:2, :], mod_ref[0, 3:4, :], mod_ref[0, 4:5, :]).astype(BF16)

    def seg(lo, width):
        return _dot(h, wm_ref[:, lo:lo + width])

    qlat = _rms(seg(_O_QLAT, MLA_Q_RANK), qn_ref[...]).astype(BF16)
    kvlat = _rms(seg(_O_KVLAT, MLA_KV_RANK), kvn_ref[...]).astype(BF16)
    cm, sm = cm_ref[...], sm_ref[...]
    qa = _dot(qlat, wqa_ref[...])
    qb = _dot(qlat, wqb_ref[...])
    ka = _dot(kvlat, wka_ref[...])
    kr = seg(_O_KR, 2 * LANES)
    krr = kr[:, :LANES] * cm + kr[:, LANES:] * sm
    q_scale = (MLA_NOPE + MLA_ROPE) ** -0.5 * LOG2E
    for hd in range(MLA_HEADS):
        sl = slice(hd * LANES, (hd + 1) * LANES)
        mq_ref[0, :, sl] = ((qa[:, sl] * cm + qb[:, sl] * sm) * q_scale).astype(BF16)
        mk_ref[0, :, sl] = (ka[:, sl] + krr).astype(BF16)
    mvt_ref[0, 0] = lax.dot_general(wvt_ref[...], kvlat, _NT, preferred_element_type=F32).astype(BF16)

    cd, sd = cd_ref[...], sd_ref[...]
    dq = seg(_O_DQ, 2 * BRANCH_W)
    dk = seg(_O_DQ + 2 * BRANCH_W, 2 * BRANCH_W)
    d_scale = DIFF_QK ** -0.5 * LOG2E
    for j in range(BRANCH_W // LANES):
        sl = slice(j * LANES, (j + 1) * LANES)
        sr = slice(BRANCH_W + j * LANES, BRANCH_W + (j + 1) * LANES)
        dq_ref[0, :, sl] = ((dq[:, sl] * cd + dq[:, sr] * sd) * d_scale).astype(BF16)
        dk_ref[0, :, sl] = (dk[:, sl] * cd + dk[:, sr] * sd).astype(BF16)
    dvt_ref[0, 0] = lax.dot_general(wdvt_ref[...], h, _NT, preferred_element_type=F32).astype(BF16)

    cr, sr_t = cr_ref[...], sr_ref[...]
    hk = RET_HEADS * RET_K
    rqk = seg(_O_RQ, 4 * hk)
    for j in range(hk // LANES):
        sl = slice(j * LANES, (j + 1) * LANES)
        rq_ref[0, :, sl] = (rqk[:, sl] * cr + rqk[:, hk + j * LANES:hk + (j + 1) * LANES] * sr_t).astype(BF16)
        rk_ref[0, :, sl] = ((rqk[:, 2 * hk + j * LANES:2 * hk + (j + 1) * LANES] * cr
                             + rqk[:, 3 * hk + j * LANES:3 * hk + (j + 1) * LANES] * sr_t)
                            * RET_K ** -0.5).astype(BF16)
    rv_ref[0] = seg(_O_RV, BRANCH_W).astype(BF16)
    rg_ref[0] = seg(_O_RV + BRANCH_W, BRANCH_W).astype(BF16)


def _proj(x, mod, gain, pw, tables, *, n_keys, key_off=0, into=None):
    b, s, d = x.shape
    tm = KEY_CHUNK
    nt = s // tm
    tok = lambda w, off=0: pl.BlockSpec((1, tm, w), lambda bi, i: (bi, i + off, 0))
    slab = pl.BlockSpec((1, 1, BRANCH_W, tm), lambda bi, i: (bi, i + key_off, 0, 0))
    tab = pl.BlockSpec((tm, LANES), lambda bi, i: (i, 0))
    weights = [pw["main"], pw["qa"], pw["qb"], pw["ka"], pw["vt"], pw["dvt"], pw["qn"], pw["kvn"]]
    hk = RET_HEADS * RET_K
    widths = [MLA_HEADS * LANES, MLA_HEADS * LANES, None, BRANCH_W, BRANCH_W, None, hk, hk, BRANCH_W, BRANCH_W]
    out_specs, out_shape = [], []
    for pos, w in enumerate(widths):
        if w is None:
            out_specs.append(slab)
            out_shape.append(jax.ShapeDtypeStruct((b, n_keys // tm, BRANCH_W, tm), BF16))
        elif pos in _KEY_OUTPUTS:
            out_specs.append(tok(w, key_off))
            out_shape.append(jax.ShapeDtypeStruct((b, n_keys, w), BF16))
        else:
            out_specs.append(tok(w))
            out_shape.append(jax.ShapeDtypeStruct((b, s, w), BF16))
    in_specs = [pl.BlockSpec((1, tm, d), lambda bi, i: (bi, i, 0)),
                pl.BlockSpec((1, N_MOD, d), lambda bi, i: (bi, 0, 0)),
                _const(gain.shape)] + [_const(w.shape) for w in weights] + [tab] * 6
    args = [x, mod, gain, *weights, *tables]
    aliases = {}
    if into is not None:
        for buf, pos in zip(into, _KEY_OUTPUTS):
            aliases[len(args)] = pos
            in_specs.append(pl.BlockSpec(memory_space=pl.ANY))
            args.append(buf)
    return pl.pallas_call(
        _proj_kernel,
        grid=(b, nt),
        in_specs=in_specs,
        out_specs=out_specs,
        out_shape=out_shape,
        input_output_aliases=aliases,
        compiler_params=_params("parallel", "parallel"),
        name="proj",
    )(*args)


MLA_GROUP = 4
SUM_ROWS = 16


def _slabs_per_chunk(n_slabs):
    return next(p for p in (3, 2, 1) if n_slabs % p == 0)


def _attend(chains, k_ref, vt_ref, s_ref, dv):
    n = chains[0][0].shape[0]
    n_slabs = vt_ref.shape[1]
    per = _slabs_per_chunk(n_slabs)
    tk = per * KEY_CHUNK
    n_chunks = n_slabs // per
    ones = jnp.ones((SUM_ROWS, KEY_CHUNK), BF16)

    def scores(c, buf):
        start = c * tk if isinstance(c, int) else pl.multiple_of(c * tk, tk)
        for g, (q_rhs, lanes, _) in enumerate(chains):
            s_ref[buf, g] = lax.dot_general(k_ref[0, pl.ds(start, tk), lanes], q_rhs, _NT,
                                            preferred_element_type=F32)

    def consume(c, buf, carry, prefetch=False):
        nxt = (c + 1) * tk
        out = []
        for g, ((q_rhs, lanes, rows), (m, acc)) in enumerate(zip(chains, carry)):
            s = s_ref[buf, g]
            m_new = jnp.maximum(m, jnp.max(s, axis=0, keepdims=True))
            alpha = jnp.exp2(m - m_new)
            pb = jnp.exp2(s - m_new).astype(BF16)
            pv = None
            for j in range(per):
                sl = slice(j * KEY_CHUNK, (j + 1) * KEY_CHUNK)
                vt = jnp.concatenate([vt_ref[0, c * per + j, rows, :], ones], axis=0)
                t = _dot(vt, pb[sl, :])
                pv = t if pv is None else pv + t
                if prefetch:
                    start = nxt + j * KEY_CHUNK
                    if not isinstance(start, int):
                        start = pl.multiple_of(start, KEY_CHUNK)
                    s_ref[1 - buf, g, sl, :] = lax.dot_general(
                        k_ref[0, pl.ds(start, KEY_CHUNK), lanes], q_rhs, _NT, preferred_element_type=F32)
            out.append((m_new, alpha * acc + pv))
        return tuple(out)

    def step(c, buf, carry):
        return consume(c, buf, carry, prefetch=True)

    carry = tuple((jnp.full((1, n), -jnp.inf, F32), jnp.zeros((dv + SUM_ROWS, n), F32)) for _ in chains)
    scores(0, 0)
    n_pairs = (n_chunks - 1) // 2
    if n_pairs > 0:
        carry = lax.fori_loop(0, n_pairs, lambda j, cr: step(2 * j + 1, 1, step(2 * j, 0, cr)), carry)
    if (n_chunks - 1) % 2:
        carry = step(n_chunks - 2, n_chunks % 2, carry)
    carry = consume(n_chunks - 1, (n_chunks - 1) % 2, carry)
    return [(acc[:dv], acc[dv:dv + 1]) for _, acc in carry]


def _kv_specs(k, vt, k_width, v_rows, index, key_range):
    lo, hi = key_range
    n_slabs = hi - lo
    assert lo % n_slabs == 0
    blk = lo // n_slabs
    return [pl.BlockSpec((1, n_slabs * KEY_CHUNK, k_width), lambda *g: (g[0], blk, index(g)),
                         pipeline_mode=pl.Buffered(1)),
            pl.BlockSpec((1, n_slabs, v_rows, KEY_CHUNK), lambda *g: (g[0], blk, index(g), 0),
                         pipeline_mode=pl.Buffered(1))]


def _score_scratch(n_chains, n_slabs, n):
    return pltpu.VMEM((2, n_chains, _slabs_per_chunk(n_slabs) * KEY_CHUNK, n), F32)


def _mla_kernel(q_ref, k_ref, vt_ref, o_ref, s_ref):
    chains = []
    for g in range(MLA_GROUP):
        lanes = slice(g * LANES, (g + 1) * LANES)
        chains.append((q_ref[0, :, lanes], lanes, slice(g * MLA_V, (g + 1) * MLA_V)))
    for (_, _, rows), (acc, l) in zip(chains, _attend(chains, k_ref, vt_ref, s_ref, MLA_V)):
        o_ref[0, rows, :] = (acc / l).astype(BF16)


def _mla_attention(q, k, vt, key_range):
    b, s, _ = q.shape
    tq = min(512, s)
    n_slabs = key_range[1] - key_range[0]
    return pl.pallas_call(
        _mla_kernel,
        grid=(b, MLA_HEADS // MLA_GROUP, s // tq),
        in_specs=[pl.BlockSpec((1, tq, MLA_GROUP * LANES), lambda bi, hg, i: (bi, i, hg))]
                 + _kv_specs(k, vt, MLA_GROUP * LANES, MLA_GROUP * MLA_V, lambda g: g[1], key_range),
        out_specs=pl.BlockSpec((1, MLA_GROUP * MLA_V, tq), lambda bi, hg, i: (bi, hg, i)),
        out_shape=jax.ShapeDtypeStruct((b, MLA_HEADS * MLA_V, s), BF16),
        scratch_shapes=[_score_scratch(MLA_GROUP, n_slabs, tq)],
        compiler_params=_params("parallel", "parallel", "parallel"),
        name="mla_attn",
    )(q, k, vt)


def _diff_kernel(lam_ref, q_ref, k_ref, vt_ref, gain_ref, o_ref, s_ref, *, out_scale):
    tq = q_ref.shape[1]
    first = lax.broadcasted_iota(jnp.int32, (tq, LANES), 1) < DIFF_QK
    zero = jnp.zeros((tq, LANES), BF16)
    chains = []
    for hd in range(DIFF_HEADS):
        lanes = slice(hd * LANES, (hd + 1) * LANES)
        q = q_ref[0, :, lanes]
        q_rhs = jnp.concatenate([jnp.where(first, q, zero), jnp.where(first, zero, q)], axis=0)
        chains.append((q_rhs, lanes, slice(hd * DIFF_V, (hd + 1) * DIFF_V)))
    for (_, _, rows), (acc, l) in zip(chains, _attend(chains, k_ref, vt_ref, s_ref, DIFF_V)):
        o = acc / l
        o = o[:, :tq] - lam_ref[0] * o[:, tq:]
        o = o * lax.rsqrt(jnp.mean(o * o, axis=0, keepdims=True) + EPS) * gain_ref[...]
        o_ref[0, rows, :] = (o * out_scale).astype(BF16)


def _diff_attention(q, k, vt, key_range, lam, gain, lambda_init):
    b, s, w = q.shape
    tq = min(256, s)
    n_slabs = key_range[1] - key_range[0]
    return pl.pallas_call(
        functools.partial(_diff_kernel, out_scale=1.0 - lambda_init),
        grid=(b, s // tq),
        in_specs=[pl.BlockSpec(memory_space=pltpu.SMEM),
                  pl.BlockSpec((1, tq, w), lambda bi, i: (bi, i, 0))]
                 + _kv_specs(k, vt, w, DIFF_HEADS * DIFF_V, lambda g: 0, key_range)
                 + [pl.BlockSpec((DIFF_V, 1), lambda bi, i: (0, 0))],
        out_specs=pl.BlockSpec((1, DIFF_HEADS * DIFF_V, tq), lambda bi, i: (bi, 0, i)),
        out_shape=jax.ShapeDtypeStruct((b, DIFF_HEADS * DIFF_V, s), BF16),
        scratch_shapes=[_score_scratch(DIFF_HEADS, n_slabs, 2 * tq)],
        compiler_params=_params("parallel", "parallel"),
        name="diff_attn",
    )(lam.reshape(1), q, k, vt, gain.reshape(DIFF_V, 1))


def _ret_kernel(q_ref, k_ref, v_ref, s0_ref, intra_ref, qdec_ref, kdec_ref, cdec_ref, *rest, reverse, post):
    if post:
        of_ref, g_ref, gain_ref, o_ref, sout_ref, state = rest
    else:
        o_ref, sout_ref, state = rest
    i = pl.program_id(1)

    @pl.when(i == 0)
    def _():
        state[...] = s0_ref[0]

    n_chunks = q_ref.shape[1] // RET_CHUNK
    lane = lax.broadcasted_iota(jnp.int32, (RET_CHUNK, RET_HEADS * RET_K), 1)
    for cc in range(n_chunks):
        c = n_chunks - 1 - cc if reverse else cc
        rows = slice(c * RET_CHUNK, (c + 1) * RET_CHUNK)
        qc, kc, vc = q_ref[0, rows, :], k_ref[0, rows, :], v_ref[0, rows, :]
        st = state[...]
        stb = st.astype(BF16)
        qd = (qc.astype(F32) * qdec_ref[...]).astype(BF16)
        kdt = (kc.astype(F32) * kdec_ref[...]).T.astype(BF16)
        outs, deltas = [], []
        for hd in range(RET_HEADS):
            mine = (lane >= hd * RET_K) & (lane < (hd + 1) * RET_K)
            vh = vc[:, hd * RET_V:(hd + 1) * RET_V]
            a = lax.dot_general(jnp.where(mine, qc, jnp.zeros_like(qc)), kc, _NT,
                                preferred_element_type=F32) * intra_ref[hd]
            o_h = _dot(a.astype(BF16), vh) + _dot(jnp.where(mine, qd, jnp.zeros_like(qd)), stb)
            outs.append(o_h)
            deltas.append(_dot(kdt[hd * RET_K:(hd + 1) * RET_K, :], vh))
        state[...] = st * cdec_ref[...] + jnp.concatenate(deltas, axis=0)
        if post:
            of = of_ref[0, rows, :]
            g = g_ref[0, rows, :].astype(F32)
            for hd in range(RET_HEADS):
                sl = slice(hd * RET_V, (hd + 1) * RET_V)
                outs[hd] = _rms(outs[hd] + of[:, sl], gain_ref[...]) * (g[:, sl] * jax.nn.sigmoid(g[:, sl]))
        o_ref[0, rows, :] = jnp.concatenate(outs, axis=1).astype(o_ref.dtype)

    @pl.when(i == pl.num_programs(1) - 1)
    def _():
        sout_ref[0] = state[...]


def _retention(q, k, v, s0, tabs, *, reverse, post=None):
    b, n, _ = q.shape
    t = min(512, n)
    nt = n // t
    hk, hv = RET_HEADS * RET_K, RET_HEADS * RET_V
    tile = (lambda bi, i: (bi, nt - 1 - i, 0)) if reverse else (lambda bi, i: (bi, i, 0))
    st_spec = pl.BlockSpec((1, hk, RET_V), lambda bi, i: (bi, 0, 0))
    in_specs = [pl.BlockSpec((1, t, hk), tile), pl.BlockSpec((1, t, hk), tile), pl.BlockSpec((1, t, hv), tile),
                st_spec] + [_const(a.shape) for a in tabs]
    args = [q, k, v, s0, *tabs]
    if post is not None:
        o_f, g, gain = post
        in_specs += [pl.BlockSpec((1, t, hv), tile), pl.BlockSpec((1, t, hv), tile), _const((1, RET_V))]
        args += [o_f, g, gain.reshape(1, RET_V)]
    return pl.pallas_call(
        functools.partial(_ret_kernel, reverse=reverse, post=post is not None),
        grid=(b, nt),
        in_specs=in_specs,
        out_specs=[pl.BlockSpec((1, t, hv), tile), st_spec],
        out_shape=[jax.ShapeDtypeStruct((b, n, hv), F32 if post is None else BF16),
                   jax.ShapeDtypeStruct((b, hk, RET_V), F32)],
        scratch_shapes=[pltpu.VMEM((hk, RET_V), F32)],
        compiler_params=_params("parallel", "arbitrary"),
        name="ret_bwd" if reverse else "ret_fwd",
    )(*args)


def _decay_tables(log_gamma, reverse):
    idx = jnp.arange(RET_CHUNK, dtype=F32)
    lg = log_gamma.astype(F32)[:, None]
    rel = idx[:, None] - idx[None, :]
    if reverse:
        rel = -rel
        q_pow, k_pow = RET_CHUNK - idx, idx
    else:
        q_pow, k_pow = idx + 1.0, RET_CHUNK - 1.0 - idx
    intra = jnp.where(rel >= 0, jnp.exp(lg[:, :, None] * jnp.maximum(rel, 0.0)), 0.0)
    qdec = jnp.repeat(jnp.exp(lg * q_pow).T, RET_K, axis=1)
    kdec = jnp.repeat(jnp.exp(lg * k_pow).T, RET_K, axis=1)
    cdec = jnp.broadcast_to(jnp.repeat(jnp.exp(lg * RET_CHUNK), RET_K, axis=0), (RET_HEADS * RET_K, RET_V))
    return intra, qdec, kdec, cdec


def _merge_kernel(x_ref, mod_ref, gain_ref, wg_ref, wb_ref, wo_ref, mla_ref, dif_ref, ret_ref, o_ref):
    x = x_ref[0]
    d = x.shape[1]
    h = _modulated(x, gain_ref[1:2, :], mod_ref[0, 3:4, :], mod_ref[0, 4:5, :]).astype(BF16)
    branches = (lax.dot_general(mla_ref[0], wb_ref[0], _TN, preferred_element_type=F32),
                lax.dot_general(dif_ref[0], wb_ref[1], _TN, preferred_element_type=F32),
                _dot(ret_ref[0], wb_ref[2]))
    y = None
    for j, br in enumerate(branches):
        t = jax.nn.sigmoid(_dot(h, wg_ref[:, j * d:(j + 1) * d])) * br
        y = t if y is None else y + t
    o_ref[0] = x + mod_ref[0, 5:6, :] * _dot(y.astype(BF16), wo_ref[...])


def _merge(x, mod, gain, wg, wb, wo, mla_t, dif_t, ret):
    b, s, d = x.shape
    tm = min(512, s)
    tok = lambda w: pl.BlockSpec((1, tm, w), lambda bi, i: (bi, i, 0))
    feat = pl.BlockSpec((1, BRANCH_W, tm), lambda bi, i: (bi, 0, i))
    return pl.pallas_call(
        _merge_kernel,
        grid=(b, s // tm),
        in_specs=[tok(d), pl.BlockSpec((1, N_MOD, d), lambda bi, i: (bi, 0, 0)), _const(gain.shape),
                  _const(wg.shape), _const(wb.shape), _const(wo.shape), feat, feat, tok(BRANCH_W)],
        out_specs=tok(d),
        out_shape=jax.ShapeDtypeStruct(x.shape, F32),
        compiler_params=_params("parallel", "parallel"),
        name="merge",
    )(x, mod, gain, wg, wb, wo, mla_t, dif_t, ret)


def _rot_cols(w, group):
    r, n = w.shape
    w = w.reshape(r, n // group, 2, group // 2)
    return jnp.stack([-w[:, :, 1], w[:, :, 0]], axis=2).reshape(r, n)


def _head_slots(w, n_heads, width):
    r = w.shape[0]
    w = w.reshape(r, n_heads, width)
    return jnp.pad(w, ((0, 0), (0, 0), (0, LANES - width))).reshape(r, n_heads * LANES)


def _proj_weights(w_in, w_qb, w_kvb, q_norm, kv_norm):
    sizes = (MLA_Q_RANK, MLA_KV_RANK, MLA_ROPE, 2 * DIFF_HEADS * DIFF_QK, 2 * DIFF_HEADS * DIFF_QK,
             DIFF_HEADS * DIFF_V, RET_HEADS * RET_K, RET_HEADS * RET_K, RET_HEADS * RET_V, RET_HEADS * RET_V)
    offs = [0]
    for sz in sizes:
        offs.append(offs[-1] + sz)
    p = [w_in[:, offs[i]:offs[i + 1]] for i in range(len(sizes))]
    gates = w_in[:, offs[-1]:]
    d = w_in.shape[0]
    lead = jnp.zeros((d, MLA_NOPE), F32)
    tail = jnp.zeros((d, LANES - MLA_NOPE - MLA_ROPE), F32)
    main = jnp.concatenate([
        p[0], p[1],
        lead, p[2], tail, lead, _rot_cols(p[2], MLA_ROPE // 2), tail,
        p[3], _rot_cols(p[3], DIFF_QK // 2), p[4], _rot_cols(p[4], DIFF_QK // 2),
        p[6], _rot_cols(p[6], RET_K), p[7], _rot_cols(p[7], RET_K),
        p[8], p[9]], axis=1)
    assert main.shape[1] == _W_MAIN
    hd = MLA_NOPE + MLA_ROPE
    qb = w_qb.reshape(MLA_Q_RANK, MLA_HEADS, hd)
    q_rot = _rot_cols(qb[:, :, MLA_NOPE:].reshape(MLA_Q_RANK, MLA_HEADS * MLA_ROPE), MLA_ROPE // 2)
    q_rot = jnp.concatenate([jnp.zeros((MLA_Q_RANK, MLA_HEADS, MLA_NOPE), F32),
                             q_rot.reshape(MLA_Q_RANK, MLA_HEADS, MLA_ROPE)], axis=2)
    kvb = w_kvb.reshape(MLA_KV_RANK, MLA_HEADS, MLA_NOPE + MLA_V)
    return {
        "main": main.astype(BF16),
        "qa": _head_slots(w_qb, MLA_HEADS, hd).astype(BF16),
        "qb": _head_slots(q_rot.reshape(MLA_Q_RANK, MLA_HEADS * hd), MLA_HEADS, hd).astype(BF16),
        "ka": _head_slots(kvb[:, :, :MLA_NOPE].reshape(MLA_KV_RANK, -1), MLA_HEADS, MLA_NOPE).astype(BF16),
        "vt": kvb[:, :, MLA_NOPE:].reshape(MLA_KV_RANK, -1).T.astype(BF16),
        "dvt": p[5].T.astype(BF16),
        "qn": q_norm.reshape(1, -1),
        "kvn": kv_norm.reshape(1, -1),
    }, gates.astype(BF16)


def _angles(pos, dim):
    inv = ROPE_BASE ** (-jnp.arange(0, dim, 2, dtype=F32) / dim)
    ang = pos.astype(F32)[:, None] * inv[None, :]
    return jnp.cos(ang), jnp.sin(ang)


def _rope_tables(n):
    t = jnp.arange(n)
    row = jnp.repeat(jnp.arange(n // GRID_W), GRID_W)
    col = t - row * GRID_W

    def axial(dim):
        cr, sr = _angles(row, dim // 2)
        cc, sc = _angles(col, dim // 2)
        return jnp.concatenate([cr, cr, cc, cc], axis=1), jnp.concatenate([sr, sr, sc, sc], axis=1)

    cm, sm = axial(MLA_ROPE)
    ones = jnp.ones((n, MLA_NOPE), F32)
    pad = jnp.zeros((n, LANES - MLA_NOPE - MLA_ROPE), F32)
    cm = jnp.concatenate([ones, cm, pad], axis=1)
    sm = jnp.concatenate([0 * ones, sm, pad], axis=1)
    cd, sd = axial(DIFF_QK)
    cs, ss = _angles(t, RET_K)
    return (cm, sm, jnp.tile(cd, (1, 2)), jnp.tile(sd, (1, 2)),
            jnp.tile(cs, (1, 4)), jnp.tile(ss, (1, 4)))


def _identity_tables(n):
    one, zero = jnp.ones((n, LANES), F32), jnp.zeros((n, LANES), F32)
    return one, zero, one, zero, one, zero


def _chunked(w, axis):
    if axis == 1:
        r = w.shape[0]
        return w.reshape(r, D_FF // FF_CHUNK, FF_CHUNK).transpose(1, 0, 2).astype(BF16)
    return w.reshape(D_FF // FF_CHUNK, FF_CHUNK, w.shape[1]).astype(BF16)


def kernel(x, c, ctx, c_ctx, ada_w, ada_b, norm_gain, ffn1_w1, ffn1_w3, ffn1_w2, ffn2_w1, ffn2_w3, ffn2_w2,
           w_in, mla_q_norm, mla_w_qb, mla_kv_norm, mla_w_kvb, diff_lambda, diff_norm, ret_decay, ret_norm,
           w_branch, w_out, final_norm):
    b, n, d = x.shape
    n_ctx = ctx.shape[1]
    pad_rows = -(b + 1) % 8
    cs = jnp.concatenate([c, c_ctx[None, :], jnp.zeros((pad_rows, d), F32)], axis=0)
    mods = _ada(cs, ada_w, ada_b)
    x_tabs, c_tabs = _rope_tables(n), _identity_tables(n_ctx)
    zero_state = jnp.zeros((b, RET_HEADS * RET_K, RET_V), F32)
    n_x, n_all = n // KEY_CHUNK, (n + n_ctx) // KEY_CHUNK
    xc = ctx
    for l in range(DEPTH):
        lambda_init = 0.8 - 0.6 * math.exp(-0.3 * l)
        ctx_out = l < DEPTH - 1
        mx = mods[l, :b].reshape(b, N_MOD, d)
        mc = jnp.broadcast_to(mods[l, b].reshape(1, N_MOD, d), (b, N_MOD, d))
        gain = norm_gain[l]
        f1 = (_chunked(ffn1_w1[l], 1), _chunked(ffn1_w3[l], 1), _chunked(ffn1_w2[l], 0))
        f2 = (_chunked(ffn2_w1[l], 1), _chunked(ffn2_w3[l], 1), _chunked(ffn2_w2[l], 0))
        pw, wg = _proj_weights(w_in[l], mla_w_qb[l], mla_w_kvb[l], mla_q_norm[l], mla_kv_norm[l])
        wb, wo = w_branch[l].astype(BF16), w_out[l].astype(BF16)
        dl = diff_lambda[l].astype(F32)
        lam = jnp.exp(jnp.sum(dl[0] * dl[1])) - jnp.exp(jnp.sum(dl[2] * dl[3])) + lambda_init
        log_gamma = -jnp.exp(ret_decay[l].astype(F32))
        tab_f, tab_b = _decay_tables(log_gamma[0], False), _decay_tables(log_gamma[1], True)

        x = _ffn(x, mx, gain, *f1, row0=0, gain_row=0)
        xc = _ffn(xc, mc, gain, *f1, row0=0, gain_row=0)

        mq, mk, mvt, dq, dk, dvt, rq, rk, rv, rg = _proj(x, mx, gain, pw, x_tabs, n_keys=n + n_ctx)
        cmq, mk, mvt, cdq, dk, dvt, crq, crk, crv, crg = _proj(
            xc, mc, gain, pw, c_tabs, n_keys=n + n_ctx, key_off=n_x, into=(mk, mvt, dk, dvt))

        mla_t = _mla_attention(mq, mk, mvt, (0, n_all))
        dif_t = _diff_attention(dq, dk, dvt, (0, n_all), lam, diff_norm[l], lambda_init)

        oc_f, st_f = _retention(crq, crk, crv, zero_state, tab_f, reverse=False)
        ret_c, st_b = _retention(crq, crk, crv, zero_state, tab_b, reverse=True,
                                 post=(oc_f, crg, ret_norm[l]) if ctx_out else None)
        ox_f, _ = _retention(rq, rk, rv, st_f, tab_f, reverse=False)
        ret_x, _ = _retention(rq, rk, rv, st_b, tab_b, reverse=True, post=(ox_f, rg, ret_norm[l]))

        x = _merge(x, mx, gain, wg, wb, wo, mla_t, dif_t, ret_x)
        x = _ffn(x, mx, gain, *f2, row0=6, gain_row=2, final_gain=None if ctx_out else final_norm)
        if ctx_out:
            mla_c = _mla_attention(cmq, mk, mvt, (n_x, n_all))
            dif_c = _diff_attention(cdq, dk, dvt, (n_x, n_all), lam, diff_norm[l], lambda_init)
            xc = _merge(xc, mc, gain, wg, wb, wo, mla_c, dif_c, ret_c)
            xc = _ffn(xc, mc, gain, *f2, row0=6, gain_row=2)
    return x
```

```python
import functools
import math

import jax
import jax.numpy as jnp
from jax import lax
from jax.experimental import pallas as pl
from jax.experimental.pallas import tpu as pltpu

D_MODEL = 1024
DEPTH = 2
GRID_W = 64
N_BRANCH = 3
MLA_HEADS = 8
MLA_Q_RANK = 384
MLA_KV_RANK = 256
MLA_NOPE = 64
MLA_ROPE = 32
MLA_V = 64
DIFF_HEADS = 4
DIFF_QK = 64
DIFF_V = 128
RET_HEADS = 4
RET_K = 64
RET_V = 128
RET_CHUNK = 128
BRANCH_W = 512
D_FF = 2816
N_MOD = 9
ROPE_BASE = 10000.0
EPS = 1e-6

LANES = 128
SUBLANES = 8
LOG2E = 1.4426950408889634
FF_CHUNK = 256
KEY_CHUNK = 256
VMEM_LIMIT = 56 * 1024 * 1024

BF16 = jnp.bfloat16
F32 = jnp.float32
_NT = (((1,), (1,)), ((), ()))
_TN = (((0,), (0,)), ((), ()))


def _params(*sem):
    return pltpu.CompilerParams(dimension_semantics=sem, vmem_limit_bytes=VMEM_LIMIT)


def _const(shape):
    zeros = (0,) * len(shape)
    return pl.BlockSpec(shape, lambda *_: zeros, pipeline_mode=pl.Buffered(1))


def _rms(x, gain):
    return x * lax.rsqrt(jnp.mean(x * x, axis=-1, keepdims=True) + EPS) * gain


def _modulated(x, gain, shift, scale):
    return _rms(x, gain) * (1.0 + scale) + shift


def _dot(a, b):
    return jnp.dot(a, b, preferred_element_type=F32)


def _ada_kernel(c_ref, w_ref, b_ref, o_ref):
    c = c_ref[...]
    o_ref[0] = _dot(c * jax.nn.sigmoid(c), w_ref[0]) + b_ref[0]


def _ada(cs, ada_w, ada_b):
    nl, d, n = ada_w.shape
    r = cs.shape[0]
    tn = n // N_MOD
    return pl.pallas_call(
        _ada_kernel,
        grid=(nl, n // tn),
        in_specs=[pl.BlockSpec((r, d), lambda l, j: (0, 0)),
                  pl.BlockSpec((1, d, tn), lambda l, j: (l, 0, j)),
                  pl.BlockSpec((1, 1, tn), lambda l, j: (l, 0, j))],
        out_specs=pl.BlockSpec((1, r, tn), lambda l, j: (l, 0, j)),
        out_shape=jax.ShapeDtypeStruct((nl, r, n), F32),
        compiler_params=_params("parallel", "parallel"),
        name="ada",
    )(cs, ada_w, ada_b.reshape(nl, 1, n))


def _ffn_kernel(x_ref, mod_ref, gain_ref, w1_ref, w3_ref, w2_ref, *rest, row0, gain_row, final):
    if final:
        fg_ref, o_ref, acc_ref = rest
    else:
        o_ref, acc_ref = rest
    x = x_ref[0]
    h = _modulated(x, gain_ref[gain_row:gain_row + 1, :], mod_ref[0, row0:row0 + 1, :],
                   mod_ref[0, row0 + 1:row0 + 2, :]).astype(BF16)
    n_chunks = w1_ref.shape[0]
    up = (_dot(h, w1_ref[0]), _dot(h, w3_ref[0]))
    for c in range(n_chunks):
        a, b = up
        if c + 1 < n_chunks:
            up = (_dot(h, w1_ref[c + 1]), _dot(h, w3_ref[c + 1]))
        t = _dot((a * jax.nn.sigmoid(a) * b).astype(BF16), w2_ref[c])
        if c == 0:
            acc_ref[...] = t
        else:
            acc_ref[...] += t
    y = x + (mod_ref[0, row0 + 2:row0 + 3, :] * 0.5) * acc_ref[...]
    if final:
        y = _rms(y, fg_ref[...])
    o_ref[0] = y


def _ffn(x, mod, gain, w1, w3, w2, *, row0, gain_row, final_gain=None):
    b, s, d = x.shape
    tm = min(512, s)
    nc = w1.shape[0]
    final = final_gain is not None
    in_specs = [pl.BlockSpec((1, tm, d), lambda bi, i: (bi, i, 0)),
                pl.BlockSpec((1, N_MOD, d), lambda bi, i: (bi, 0, 0)),
                _const(gain.shape), _const(w1.shape), _const(w3.shape), _const(w2.shape)]
    args = [x, mod, gain, w1, w3, w2]
    if final:
        in_specs.append(_const((1, d)))
        args.append(final_gain.reshape(1, d))
    return pl.pallas_call(
        functools.partial(_ffn_kernel, row0=row0, gain_row=gain_row, final=final),
        grid=(b, s // tm),
        in_specs=in_specs,
        out_specs=pl.BlockSpec((1, tm, d), lambda bi, i: (bi, i, 0)),
        out_shape=jax.ShapeDtypeStruct(x.shape, F32),
        scratch_shapes=[pltpu.VMEM((tm, d), F32)],
        compiler_params=_params("parallel", "parallel"),
        name="ffn",
    )(*args)


_O_QLAT = 0
_O_KVLAT = _O_QLAT + MLA_Q_RANK
_O_KR = _O_KVLAT + MLA_KV_RANK
_O_DQ = _O_KR + 2 * LANES
_O_RQ = _O_DQ + 4 * BRANCH_W
_O_RV = _O_RQ + 4 * RET_HEADS * RET_K
_W_MAIN = _O_RV + 2 * BRANCH_W


_KEY_OUTPUTS = (1, 2, 4, 5)


def _proj_kernel(x_ref, mod_ref, gain_ref, wm_ref, wqa_ref, wqb_ref, wka_ref, wvt_ref, wdvt_ref,
                 qn_ref, kvn_ref, cm_ref, sm_ref, cd_ref, sd_ref, cr_ref, sr_ref, *rest):
    mq_ref, mk_ref, mvt_ref, dq_ref, dk_ref, dvt_ref, rq_ref, rk_ref, rv_ref, rg_ref = rest[-10:]
    x = x_ref[0]
    h = _modulated(x, gain_ref[1:2, :], mod_ref[0, 3:4, :], mod_ref[0, 4:5, :]).astype(BF16)

    def seg(lo, width):
        return _dot(h, wm_ref[:, lo:lo + width])

    qlat = _rms(seg(_O_QLAT, MLA_Q_RANK), qn_ref[...]).astype(BF16)
    kvlat = _rms(seg(_O_KVLAT, MLA_KV_RANK), kvn_ref[...]).astype(BF16)
    cm, sm = cm_ref[...], sm_ref[...]
    qa = _dot(qlat, wqa_ref[...])
    qb = _dot(qlat, wqb_ref[...])
    ka = _dot(kvlat, wka_ref[...])
    kr = seg(_O_KR, 2 * LANES)
    krr = kr[:, :LANES] * cm + kr[:, LANES:] * sm
    q_scale = (MLA_NOPE + MLA_ROPE) ** -0.5 * LOG2E
    for hd in range(MLA_HEADS):
        sl = slice(hd * LANES, (hd + 1) * LANES)
        mq_ref[0, :, sl] = ((qa[:, sl] * cm + qb[:, sl] * sm) * q_scale).astype(BF16)
        mk_ref[0, :, sl] = (ka[:, sl] + krr).astype(BF16)
    mvt_ref[0, 0] = lax.dot_general(wvt_ref[...], kvlat, _NT, preferred_element_type=F32).astype(BF16)

    cd, sd = cd_ref[...], sd_ref[...]
    dq = seg(_O_DQ, 2 * BRANCH_W)
    dk = seg(_O_DQ + 2 * BRANCH_W, 2 * BRANCH_W)
    d_scale = DIFF_QK ** -0.5 * LOG2E
    for j in range(BRANCH_W // LANES):
        sl = slice(j * LANES, (j + 1) * LANES)
        sr = slice(BRANCH_W + j * LANES, BRANCH_W + (j + 1) * LANES)
        dq_ref[0, :, sl] = ((dq[:, sl] * cd + dq[:, sr] * sd) * d_scale).astype(BF16)
        dk_ref[0, :, sl] = (dk[:, sl] * cd + dk[:, sr] * sd).astype(BF16)
    dvt_ref[0, 0] = lax.dot_general(wdvt_ref[...], h, _NT, preferred_element_type=F32).astype(BF16)

    cr, sr_t = cr_ref[...], sr_ref[...]
    hk = RET_HEADS * RET_K
    rqk = seg(_O_RQ, 4 * hk)
    for j in range(hk // LANES):
        sl = slice(j * LANES, (j + 1) * LANES)
        rq_ref[0, :, sl] = (rqk[:, sl] * cr + rqk[:, hk + j * LANES:hk + (j + 1) * LANES] * sr_t).astype(BF16)
        rk_ref[0, :, sl] = ((rqk[:, 2 * hk + j * LANES:2 * hk + (j + 1) * LANES] * cr
                             + rqk[:, 3 * hk + j * LANES:3 * hk + (j + 1) * LANES] * sr_t)
                            * RET_K ** -0.5).astype(BF16)
    rv_ref[0] = seg(_O_RV, BRANCH_W).astype(BF16)
    rg_ref[0] = seg(_O_RV + BRANCH_W, BRANCH_W).astype(BF16)


def _proj(x, mod, gain, pw, tables, *, n_keys, key_off=0, into=None):
    b, s, d = x.shape
    tm = KEY_CHUNK
    nt = s // tm
    tok = lambda w, off=0: pl.BlockSpec((1, tm, w), lambda bi, i: (bi, i + off, 0))
    slab = pl.BlockSpec((1, 1, BRANCH_W, tm), lambda bi, i: (bi, i + key_off, 0, 0))
    tab = pl.BlockSpec((tm, LANES), lambda bi, i: (i, 0))
    weights = [pw["main"], pw["qa"], pw["qb"], pw["ka"], pw["vt"], pw["dvt"], pw["qn"], pw["kvn"]]
    hk = RET_HEADS * RET_K
    widths = [MLA_HEADS * LANES, MLA_HEADS * LANES, None, BRANCH_W, BRANCH_W, None, hk, hk, BRANCH_W, BRANCH_W]
    out_specs, out_shape = [], []
    for pos, w in enumerate(widths):
        if w is None:
            out_specs.append(slab)
            out_shape.append(jax.ShapeDtypeStruct((b, n_keys // tm, BRANCH_W, tm), BF16))
        elif pos in _KEY_OUTPUTS:
            out_specs.append(tok(w, key_off))
            out_shape.append(jax.ShapeDtypeStruct((b, n_keys, w), BF16))
        else:
            out_specs.append(tok(w))
            out_shape.append(jax.ShapeDtypeStruct((b, s, w), BF16))
    in_specs = [pl.BlockSpec((1, tm, d), lambda bi, i: (bi, i, 0)),
                pl.BlockSpec((1, N_MOD, d), lambda bi, i: (bi, 0, 0)),
                _const(gain.shape)] + [_const(w.shape) for w in weights] + [tab] * 6
    args = [x, mod, gain, *weights, *tables]
    aliases = {}
    if into is not None:
        for buf, pos in zip(into, _KEY_OUTPUTS):
            aliases[len(args)] = pos
            in_specs.append(pl.BlockSpec(memory_space=pl.ANY))
            args.append(buf)
    return pl.pallas_call(
        _proj_kernel,
        grid=(b, nt),
        in_specs=in_specs,
        out_specs=out_specs,
        out_shape=out_shape,
        input_output_aliases=aliases,
        compiler_params=_params("parallel", "parallel"),
        name="proj",
    )(*args)


MLA_GROUP = 4
SUM_ROWS = 16


def _slabs_per_chunk(n_slabs):
    return next(p for p in (3, 2, 1) if n_slabs % p == 0)


def _attend(chains, k_ref, vt_ref, s_ref, mx_ref, dv):
    n = chains[0][0].shape[0]
    n_slabs = vt_ref.shape[1]
    per = _slabs_per_chunk(n_slabs)
    tk = per * KEY_CHUNK
    n_chunks = n_slabs // per
    ones = jnp.ones((SUM_ROWS, KEY_CHUNK), BF16)

    def score_slab(c, j, buf, g):
        q_rhs, lanes, _ = chains[g]
        start = c * tk + j * KEY_CHUNK
        if not isinstance(start, int):
            start = pl.multiple_of(start, KEY_CHUNK)
        s = lax.dot_general(k_ref[0, pl.ds(start, KEY_CHUNK), lanes], q_rhs, _NT, preferred_element_type=F32)
        s_ref[buf, g, j * KEY_CHUNK:(j + 1) * KEY_CHUNK, :] = s
        mx_ref[buf, g, j] = jnp.max(s.reshape(KEY_CHUNK // SUBLANES, SUBLANES, n), axis=0)

    def consume(c, buf, carry, prefetch=False):
        out = []
        for g, ((_, _, rows), (m, acc)) in enumerate(zip(chains, carry)):
            mx = mx_ref[buf, g, 0]
            for j in range(1, per):
                mx = jnp.maximum(mx, mx_ref[buf, g, j])
            m_new = jnp.maximum(m, jnp.max(mx, axis=0, keepdims=True))
            alpha = jnp.exp2(m - m_new)
            pb = jnp.exp2(s_ref[buf, g] - m_new).astype(BF16)
            pv = None
            for j in range(per):
                vt = jnp.concatenate([vt_ref[0, c * per + j, rows, :], ones], axis=0)
                t = _dot(vt, pb[j * KEY_CHUNK:(j + 1) * KEY_CHUNK, :])
                pv = t if pv is None else pv + t
                if prefetch:
                    score_slab(c + 1, j, 1 - buf, g)
            out.append((m_new, alpha * acc + pv))
        return tuple(out)

    def step(c, buf, carry):
        return consume(c, buf, carry, prefetch=True)

    carry = tuple((jnp.full((1, n), -jnp.inf, F32), jnp.zeros((dv + SUM_ROWS, n), F32)) for _ in chains)
    for g in range(len(chains)):
        for j in range(per):
            score_slab(0, j, 0, g)
    n_pairs = (n_chunks - 1) // 2
    if n_pairs > 0:
        carry = lax.fori_loop(0, n_pairs, lambda j, cr: step(2 * j + 1, 1, step(2 * j, 0, cr)), carry)
    if (n_chunks - 1) % 2:
        carry = step(n_chunks - 2, n_chunks % 2, carry)
    carry = consume(n_chunks - 1, (n_chunks - 1) % 2, carry)
    return [(acc[:dv], acc[dv:dv + 1]) for _, acc in carry]


def _kv_specs(k, vt, k_width, v_rows, index, key_range):
    lo, hi = key_range
    n_slabs = hi - lo
    assert lo % n_slabs == 0
    blk = lo // n_slabs
    return [pl.BlockSpec((1, n_slabs * KEY_CHUNK, k_width), lambda *g: (g[0], blk, index(g)),
                         pipeline_mode=pl.Buffered(1)),
            pl.BlockSpec((1, n_slabs, v_rows, KEY_CHUNK), lambda *g: (g[0], blk, index(g), 0),
                         pipeline_mode=pl.Buffered(1))]


def _score_scratch(n_chains, n_slabs, n):
    per = _slabs_per_chunk(n_slabs)
    return [pltpu.VMEM((2, n_chains, per * KEY_CHUNK, n), F32),
            pltpu.VMEM((2, n_chains, per, SUBLANES, n), F32)]


def _mla_kernel(q_ref, k_ref, vt_ref, o_ref, s_ref, mx_ref):
    chains = []
    for g in range(MLA_GROUP):
        lanes = slice(g * LANES, (g + 1) * LANES)
        chains.append((q_ref[0, :, lanes], lanes, slice(g * MLA_V, (g + 1) * MLA_V)))
    for (_, _, rows), (acc, l) in zip(chains, _attend(chains, k_ref, vt_ref, s_ref, mx_ref, MLA_V)):
        o_ref[0, rows, :] = (acc / l).astype(BF16)


def _mla_attention(q, k, vt, key_range):
    b, s, _ = q.shape
    tq = min(512, s)
    n_slabs = key_range[1] - key_range[0]
    return pl.pallas_call(
        _mla_kernel,
        grid=(b, MLA_HEADS // MLA_GROUP, s // tq),
        in_specs=[pl.BlockSpec((1, tq, MLA_GROUP * LANES), lambda bi, hg, i: (bi, i, hg))]
                 + _kv_specs(k, vt, MLA_GROUP * LANES, MLA_GROUP * MLA_V, lambda g: g[1], key_range),
        out_specs=pl.BlockSpec((1, MLA_GROUP * MLA_V, tq), lambda bi, hg, i: (bi, hg, i)),
        out_shape=jax.ShapeDtypeStruct((b, MLA_HEADS * MLA_V, s), BF16),
        scratch_shapes=_score_scratch(MLA_GROUP, n_slabs, tq),
        compiler_params=_params("parallel", "parallel", "parallel"),
        name="mla_attn",
    )(q, k, vt)


def _diff_kernel(lam_ref, q_ref, k_ref, vt_ref, gain_ref, o_ref, s_ref, mx_ref, *, out_scale):
    tq = q_ref.shape[1]
    first = lax.broadcasted_iota(jnp.int32, (tq, LANES), 1) < DIFF_QK
    zero = jnp.zeros((tq, LANES), BF16)
    chains = []
    for hd in range(DIFF_HEADS):
        lanes = slice(hd * LANES, (hd + 1) * LANES)
        q = q_ref[0, :, lanes]
        q_rhs = jnp.concatenate([jnp.where(first, q, zero), jnp.where(first, zero, q)], axis=0)
        chains.append((q_rhs, lanes, slice(hd * DIFF_V, (hd + 1) * DIFF_V)))
    for (_, _, rows), (acc, l) in zip(chains, _attend(chains, k_ref, vt_ref, s_ref, mx_ref, DIFF_V)):
        o = acc / l
        o = o[:, :tq] - lam_ref[0] * o[:, tq:]
        o = o * lax.rsqrt(jnp.mean(o * o, axis=0, keepdims=True) + EPS) * gain_ref[...]
        o_ref[0, rows, :] = (o * out_scale).astype(BF16)


def _diff_attention(q, k, vt, key_range, lam, gain, lambda_init):
    b, s, w = q.shape
    tq = min(256, s)
    n_slabs = key_range[1] - key_range[0]
    return pl.pallas_call(
        functools.partial(_diff_kernel, out_scale=1.0 - lambda_init),
        grid=(b, s // tq),
        in_specs=[pl.BlockSpec(memory_space=pltpu.SMEM),
                  pl.BlockSpec((1, tq, w), lambda bi, i: (bi, i, 0))]
                 + _kv_specs(k, vt, w, DIFF_HEADS * DIFF_V, lambda g: 0, key_range)
                 + [pl.BlockSpec((DIFF_V, 1), lambda bi, i: (0, 0))],
        out_specs=pl.BlockSpec((1, DIFF_HEADS * DIFF_V, tq), lambda bi, i: (bi, 0, i)),
        out_shape=jax.ShapeDtypeStruct((b, DIFF_HEADS * DIFF_V, s), BF16),
        scratch_shapes=_score_scratch(DIFF_HEADS, n_slabs, 2 * tq),
        compiler_params=_params("parallel", "parallel"),
        name="diff_attn",
    )(lam.reshape(1), q, k, vt, gain.reshape(DIFF_V, 1))


def _ret_kernel(q_ref, k_ref, v_ref, s0_ref, intra_ref, qdec_ref, kdec_ref, cdec_ref, *rest, reverse, post):
    if post:
        of_ref, g_ref, gain_ref, o_ref, sout_ref, state = rest
    else:
        o_ref, sout_ref, state = rest
    i = pl.program_id(1)

    @pl.when(i == 0)
    def _():
        state[...] = s0_ref[0]

    n_chunks = q_ref.shape[1] // RET_CHUNK
    lane = lax.broadcasted_iota(jnp.int32, (RET_CHUNK, RET_HEADS * RET_K), 1)
    masks = [(lane >= hd * RET_K) & (lane < (hd + 1) * RET_K) for hd in range(RET_HEADS)]
    order = [n_chunks - 1 - cc if reverse else cc for cc in range(n_chunks)]
    heads = range(RET_HEADS)

    def head_v(c, hd):
        return v_ref[0, c * RET_CHUNK:(c + 1) * RET_CHUNK, hd * RET_V:(hd + 1) * RET_V]

    intra, deltas = {}, {}
    for c in order:
        rows = slice(c * RET_CHUNK, (c + 1) * RET_CHUNK)
        qc, kc = q_ref[0, rows, :], k_ref[0, rows, :]
        kdt = (kc.astype(F32) * kdec_ref[...]).T.astype(BF16)
        for hd in heads:
            a = lax.dot_general(jnp.where(masks[hd], qc, jnp.zeros_like(qc)), kc, _NT,
                                preferred_element_type=F32)
            intra[c, hd] = (a * intra_ref[hd]).astype(BF16)
            deltas[c, hd] = _dot(kdt[hd * RET_K:(hd + 1) * RET_K, :], head_v(c, hd))
    st = state[...]
    entering = {}
    for c in order:
        entering[c] = st.astype(BF16)
        st = st * cdec_ref[...] + jnp.concatenate([deltas[c, hd] for hd in heads], axis=0)
    state[...] = st
    for c in order:
        rows = slice(c * RET_CHUNK, (c + 1) * RET_CHUNK)
        qd = (q_ref[0, rows, :].astype(F32) * qdec_ref[...]).astype(BF16)
        outs = [_dot(intra[c, hd], head_v(c, hd))
                + _dot(jnp.where(masks[hd], qd, jnp.zeros_like(qd)), entering[c]) for hd in heads]
        if post:
            of = of_ref[0, rows, :]
            g = g_ref[0, rows, :].astype(F32)
            for hd in range(RET_HEADS):
                sl = slice(hd * RET_V, (hd + 1) * RET_V)
                outs[hd] = _rms(outs[hd] + of[:, sl], gain_ref[...]) * (g[:, sl] * jax.nn.sigmoid(g[:, sl]))
        o_ref[0, rows, :] = jnp.concatenate(outs, axis=1).astype(o_ref.dtype)

    @pl.when(i == pl.num_programs(1) - 1)
    def _():
        sout_ref[0] = state[...]


def _retention(q, k, v, s0, tabs, *, reverse, post=None):
    b, n, _ = q.shape
    t = min(512, n)
    nt = n // t
    hk, hv = RET_HEADS * RET_K, RET_HEADS * RET_V
    tile = (lambda bi, i: (bi, nt - 1 - i, 0)) if reverse else (lambda bi, i: (bi, i, 0))
    st_spec = pl.BlockSpec((1, hk, RET_V), lambda bi, i: (bi, 0, 0))
    in_specs = [pl.BlockSpec((1, t, hk), tile), pl.BlockSpec((1, t, hk), tile), pl.BlockSpec((1, t, hv), tile),
                st_spec] + [_const(a.shape) for a in tabs]
    args = [q, k, v, s0, *tabs]
    if post is not None:
        o_f, g, gain = post
        in_specs += [pl.BlockSpec((1, t, hv), tile), pl.BlockSpec((1, t, hv), tile), _const((1, RET_V))]
        args += [o_f, g, gain.reshape(1, RET_V)]
    return pl.pallas_call(
        functools.partial(_ret_kernel, reverse=reverse, post=post is not None),
        grid=(b, nt),
        in_specs=in_specs,
        out_specs=[pl.BlockSpec((1, t, hv), tile), st_spec],
        out_shape=[jax.ShapeDtypeStruct((b, n, hv), F32 if post is None else BF16),
                   jax.ShapeDtypeStruct((b, hk, RET_V), F32)],
        scratch_shapes=[pltpu.VMEM((hk, RET_V), F32)],
        compiler_params=_params("parallel", "arbitrary"),
        name="ret_bwd" if reverse else "ret_fwd",
    )(*args)


def _decay_tables(log_gamma, reverse):
    idx = jnp.arange(RET_CHUNK, dtype=F32)
    lg = log_gamma.astype(F32)[:, None]
    rel = idx[:, None] - idx[None, :]
    if reverse:
        rel = -rel
        q_pow, k_pow = RET_CHUNK - idx, idx
    else:
        q_pow, k_pow = idx + 1.0, RET_CHUNK - 1.0 - idx
    intra = jnp.where(rel >= 0, jnp.exp(lg[:, :, None] * jnp.maximum(rel, 0.0)), 0.0)
    qdec = jnp.repeat(jnp.exp(lg * q_pow).T, RET_K, axis=1)
    kdec = jnp.repeat(jnp.exp(lg * k_pow).T, RET_K, axis=1)
    cdec = jnp.broadcast_to(jnp.repeat(jnp.exp(lg * RET_CHUNK), RET_K, axis=0), (RET_HEADS * RET_K, RET_V))
    return intra, qdec, kdec, cdec


def _merge_kernel(x_ref, mod_ref, gain_ref, wg_ref, wb_ref, wo_ref, mla_ref, dif_ref, ret_ref, o_ref):
    x = x_ref[0]
    d = x.shape[1]
    h = _modulated(x, gain_ref[1:2, :], mod_ref[0, 3:4, :], mod_ref[0, 4:5, :]).astype(BF16)
    branches = (lax.dot_general(mla_ref[0], wb_ref[0], _TN, preferred_element_type=F32),
                lax.dot_general(dif_ref[0], wb_ref[1], _TN, preferred_element_type=F32),
                _dot(ret_ref[0], wb_ref[2]))
    y = None
    for j, br in enumerate(branches):
        t = jax.nn.sigmoid(_dot(h, wg_ref[:, j * d:(j + 1) * d])) * br
        y = t if y is None else y + t
    o_ref[0] = x + mod_ref[0, 5:6, :] * _dot(y.astype(BF16), wo_ref[...])


def _merge(x, mod, gain, wg, wb, wo, mla_t, dif_t, ret):
    b, s, d = x.shape
    tm = min(512, s)
    tok = lambda w: pl.BlockSpec((1, tm, w), lambda bi, i: (bi, i, 0))
    feat = pl.BlockSpec((1, BRANCH_W, tm), lambda bi, i: (bi, 0, i))
    return pl.pallas_call(
        _merge_kernel,
        grid=(b, s // tm),
        in_specs=[tok(d), pl.BlockSpec((1, N_MOD, d), lambda bi, i: (bi, 0, 0)), _const(gain.shape),
                  _const(wg.shape), _const(wb.shape), _const(wo.shape), feat, feat, tok(BRANCH_W)],
        out_specs=tok(d),
        out_shape=jax.ShapeDtypeStruct(x.shape, F32),
        compiler_params=_params("parallel", "parallel"),
        name="merge",
    )(x, mod, gain, wg, wb, wo, mla_t, dif_t, ret)


def _rot_cols(w, group):
    r, n = w.shape
    w = w.reshape(r, n // group, 2, group // 2)
    return jnp.stack([-w[:, :, 1], w[:, :, 0]], axis=2).reshape(r, n)


def _head_slots(w, n_heads, width):
    r = w.shape[0]
    w = w.reshape(r, n_heads, width)
    return jnp.pad(w, ((0, 0), (0, 0), (0, LANES - width))).reshape(r, n_heads * LANES)


def _proj_weights(w_in, w_qb, w_kvb, q_norm, kv_norm):
    sizes = (MLA_Q_RANK, MLA_KV_RANK, MLA_ROPE, 2 * DIFF_HEADS * DIFF_QK, 2 * DIFF_HEADS * DIFF_QK,
             DIFF_HEADS * DIFF_V, RET_HEADS * RET_K, RET_HEADS * RET_K, RET_HEADS * RET_V, RET_HEADS * RET_V)
    offs = [0]
    for sz in sizes:
        offs.append(offs[-1] + sz)
    p = [w_in[:, offs[i]:offs[i + 1]] for i in range(len(sizes))]
    gates = w_in[:, offs[-1]:]
    d = w_in.shape[0]
    lead = jnp.zeros((d, MLA_NOPE), F32)
    tail = jnp.zeros((d, LANES - MLA_NOPE - MLA_ROPE), F32)
    main = jnp.concatenate([
        p[0], p[1],
        lead, p[2], tail, lead, _rot_cols(p[2], MLA_ROPE // 2), tail,
        p[3], _rot_cols(p[3], DIFF_QK // 2), p[4], _rot_cols(p[4], DIFF_QK // 2),
        p[6], _rot_cols(p[6], RET_K), p[7], _rot_cols(p[7], RET_K),
        p[8], p[9]], axis=1)
    assert main.shape[1] == _W_MAIN
    hd = MLA_NOPE + MLA_ROPE
    qb = w_qb.reshape(MLA_Q_RANK, MLA_HEADS, hd)
    q_rot = _rot_cols(qb[:, :, MLA_NOPE:].reshape(MLA_Q_RANK, MLA_HEADS * MLA_ROPE), MLA_ROPE // 2)
    q_rot = jnp.concatenate([jnp.zeros((MLA_Q_RANK, MLA_HEADS, MLA_NOPE), F32),
                             q_rot.reshape(MLA_Q_RANK, MLA_HEADS, MLA_ROPE)], axis=2)
    kvb = w_kvb.reshape(MLA_KV_RANK, MLA_HEADS, MLA_NOPE + MLA_V)
    return {
        "main": main.astype(BF16),
        "qa": _head_slots(w_qb, MLA_HEADS, hd).astype(BF16),
        "qb": _head_slots(q_rot.reshape(MLA_Q_RANK, MLA_HEADS * hd), MLA_HEADS, hd).astype(BF16),
        "ka": _head_slots(kvb[:, :, :MLA_NOPE].reshape(MLA_KV_RANK, -1), MLA_HEADS, MLA_NOPE).astype(BF16),
        "vt": kvb[:, :, MLA_NOPE:].reshape(MLA_KV_RANK, -1).T.astype(BF16),
        "dvt": p[5].T.astype(BF16),
        "qn": q_norm.reshape(1, -1),
        "kvn": kv_norm.reshape(1, -1),
    }, gates.astype(BF16)


def _angles(pos, dim):
    inv = ROPE_BASE ** (-jnp.arange(0, dim, 2, dtype=F32) / dim)
    ang = pos.astype(F32)[:, None] * inv[None, :]
    return jnp.cos(ang), jnp.sin(ang)


def _rope_tables(n):
    t = jnp.arange(n)
    row = jnp.repeat(jnp.arange(n // GRID_W), GRID_W)
    col = t - row * GRID_W

    def axial(dim):
        cr, sr = _angles(row, dim // 2)
        cc, sc = _angles(col, dim // 2)
        return jnp.concatenate([cr, cr, cc, cc], axis=1), jnp.concatenate([sr, sr, sc, sc], axis=1)

    cm, sm = axial(MLA_ROPE)
    ones = jnp.ones((n, MLA_NOPE), F32)
    pad = jnp.zeros((n, LANES - MLA_NOPE - MLA_ROPE), F32)
    cm = jnp.concatenate([ones, cm, pad], axis=1)
    sm = jnp.concatenate([0 * ones, sm, pad], axis=1)
    cd, sd = axial(DIFF_QK)
    cs, ss = _angles(t, RET_K)
    return (cm, sm, jnp.tile(cd, (1, 2)), jnp.tile(sd, (1, 2)),
            jnp.tile(cs, (1, 4)), jnp.tile(ss, (1, 4)))


def _identity_tables(n):
    one, zero = jnp.ones((n, LANES), F32), jnp.zeros((n, LANES), F32)
    return one, zero, one, zero, one, zero


def _chunked(w, axis):
    if axis == 1:
        r = w.shape[0]
        return w.reshape(r, D_FF // FF_CHUNK, FF_CHUNK).transpose(1, 0, 2).astype(BF16)
    return w.reshape(D_FF // FF_CHUNK, FF_CHUNK, w.shape[1]).astype(BF16)


def kernel(x, c, ctx, c_ctx, ada_w, ada_b, norm_gain, ffn1_w1, ffn1_w3, ffn1_w2, ffn2_w1, ffn2_w3, ffn2_w2,
           w_in, mla_q_norm, mla_w_qb, mla_kv_norm, mla_w_kvb, diff_lambda, diff_norm, ret_decay, ret_norm,
           w_branch, w_out, final_norm):
    b, n, d = x.shape
    n_ctx = ctx.shape[1]
    pad_rows = -(b + 1) % 8
    cs = jnp.concatenate([c, c_ctx[None, :], jnp.zeros((pad_rows, d), F32)], axis=0)
    mods = _ada(cs, ada_w, ada_b)
    x_tabs, c_tabs = _rope_tables(n), _identity_tables(n_ctx)
    zero_state = jnp.zeros((b, RET_HEADS * RET_K, RET_V), F32)
    n_x, n_all = n // KEY_CHUNK, (n + n_ctx) // KEY_CHUNK
    xc = ctx
    for l in range(DEPTH):
        lambda_init = 0.8 - 0.6 * math.exp(-0.3 * l)
        ctx_out = l < DEPTH - 1
        mx = mods[l, :b].reshape(b, N_MOD, d)
        mc = jnp.broadcast_to(mods[l, b].reshape(1, N_MOD, d), (b, N_MOD, d))
        gain = norm_gain[l]
        f1 = (_chunked(ffn1_w1[l], 1), _chunked(ffn1_w3[l], 1), _chunked(ffn1_w2[l], 0))
        f2 = (_chunked(ffn2_w1[l], 1), _chunked(ffn2_w3[l], 1), _chunked(ffn2_w2[l], 0))
        pw, wg = _proj_weights(w_in[l], mla_w_qb[l], mla_w_kvb[l], mla_q_norm[l], mla_kv_norm[l])
        wb, wo = w_branch[l].astype(BF16), w_out[l].astype(BF16)
        dl = diff_lambda[l].astype(F32)
        lam = jnp.exp(jnp.sum(dl[0] * dl[1])) - jnp.exp(jnp.sum(dl[2] * dl[3])) + lambda_init
        log_gamma = -jnp.exp(ret_decay[l].astype(F32))
        tab_f, tab_b = _decay_tables(log_gamma[0], False), _decay_tables(log_gamma[1], True)

        x = _ffn(x, mx, gain, *f1, row0=0, gain_row=0)
        xc = _ffn(xc, mc, gain, *f1, row0=0, gain_row=0)

        mq, mk, mvt, dq, dk, dvt, rq, rk, rv, rg = _proj(x, mx, gain, pw, x_tabs, n_keys=n + n_ctx)
        cmq, mk, mvt, cdq, dk, dvt, crq, crk, crv, crg = _proj(
            xc, mc, gain, pw, c_tabs, n_keys=n + n_ctx, key_off=n_x, into=(mk, mvt, dk, dvt))

        mla_t = _mla_attention(mq, mk, mvt, (0, n_all))
        dif_t = _diff_attention(dq, dk, dvt, (0, n_all), lam, diff_norm[l], lambda_init)

        oc_f, st_f = _retention(crq, crk, crv, zero_state, tab_f, reverse=False)
        ret_c, st_b = _retention(crq, crk, crv, zero_state, tab_b, reverse=True,
                                 post=(oc_f, crg, ret_norm[l]) if ctx_out else None)
        ox_f, _ = _retention(rq, rk, rv, st_f, tab_f, reverse=False)
        ret_x, _ = _retention(rq, rk, rv, st_b, tab_b, reverse=True, post=(ox_f, rg, ret_norm[l]))

        x = _merge(x, mx, gain, wg, wb, wo, mla_t, dif_t, ret_x)
        x = _ffn(x, mx, gain, *f2, row0=6, gain_row=2, final_gain=None if ctx_out else final_norm)
        if ctx_out:
            mla_c = _mla_attention(cmq, mk, mvt, (n_x, n_all))
            dif_c = _diff_attention(cdq, dk, dvt, (n_x, n_all), lam, diff_norm[l], lambda_init)
            xc = _merge(xc, mc, gain, wg, wb, wo, mla_c, dif_c, ret_c)
            xc = _ffn(xc, mc, gain, *f2, row0=6, gain_row=2)
    return x
```

```python
import functools
import math

import jax
import jax.numpy as jnp
from jax import lax
from jax.experimental import pallas as pl
from jax.experimental.pallas import tpu as pltpu

D_MODEL = 1024
DEPTH = 2
GRID_W = 64
N_BRANCH = 3
MLA_HEADS = 8
MLA_Q_RANK = 384
MLA_KV_RANK = 256
MLA_NOPE = 64
MLA_ROPE = 32
MLA_V = 64
DIFF_HEADS = 4
DIFF_QK = 64
DIFF_V = 128
RET_HEADS = 4
RET_K = 64
RET_V = 128
RET_CHUNK = 128
BRANCH_W = 512
D_FF = 2816
N_MOD = 9
ROPE_BASE = 10000.0
EPS = 1e-6

LANES = 128
SUBLANES = 8
LOG2E = 1.4426950408889634
FF_CHUNK = 256
KEY_CHUNK = 256
TOKEN_TILE = 512
MLA_Q_TILE = 512
DIFF_Q_TILE = 256
VMEM_LIMIT = 56 * 1024 * 1024

BF16 = jnp.bfloat16
F32 = jnp.float32
_NT = (((1,), (1,)), ((), ()))
_TN = (((0,), (0,)), ((), ()))


def _params(*sem):
    return pltpu.CompilerParams(dimension_semantics=sem, vmem_limit_bytes=VMEM_LIMIT)


def _const(shape):
    zeros = (0,) * len(shape)
    return pl.BlockSpec(shape, lambda *_: zeros, pipeline_mode=pl.Buffered(1))


def _rms(x, gain):
    return x * lax.rsqrt(jnp.mean(x * x, axis=-1, keepdims=True) + EPS) * gain


def _modulated(x, gain, shift, scale):
    return _rms(x, gain) * (1.0 + scale) + shift


def _dot(a, b):
    return jnp.dot(a, b, preferred_element_type=F32)


def _ada_kernel(c_ref, w_ref, b_ref, o_ref):
    c = c_ref[...]
    o_ref[0] = _dot(c * jax.nn.sigmoid(c), w_ref[0]) + b_ref[0]


def _ada(cs, ada_w, ada_b):
    nl, d, n = ada_w.shape
    r = cs.shape[0]
    tn = n // N_MOD
    return pl.pallas_call(
        _ada_kernel,
        grid=(nl, n // tn),
        in_specs=[pl.BlockSpec((r, d), lambda l, j: (0, 0)),
                  pl.BlockSpec((1, d, tn), lambda l, j: (l, 0, j)),
                  pl.BlockSpec((1, 1, tn), lambda l, j: (l, 0, j))],
        out_specs=pl.BlockSpec((1, r, tn), lambda l, j: (l, 0, j)),
        out_shape=jax.ShapeDtypeStruct((nl, r, n), F32),
        compiler_params=_params("parallel", "parallel"),
        name="ada",
    )(cs, ada_w, ada_b.reshape(nl, 1, n))


def _ffn_kernel(x_ref, mod_ref, gain_ref, w1_ref, w3_ref, w2_ref, *rest, row0, gain_row, final):
    if final:
        fg_ref, o_ref, acc_ref = rest
    else:
        o_ref, acc_ref = rest
    x = x_ref[0]
    h = _modulated(x, gain_ref[gain_row:gain_row + 1, :], mod_ref[0, row0:row0 + 1, :],
                   mod_ref[0, row0 + 1:row0 + 2, :]).astype(BF16)
    n_chunks = w1_ref.shape[0]
    up = (_dot(h, w1_ref[0]), _dot(h, w3_ref[0]))
    for c in range(n_chunks):
        a, b = up
        if c + 1 < n_chunks:
            up = (_dot(h, w1_ref[c + 1]), _dot(h, w3_ref[c + 1]))
        t = _dot((a * jax.nn.sigmoid(a) * b).astype(BF16), w2_ref[c])
        if c == 0:
            acc_ref[...] = t
        else:
            acc_ref[...] += t
    y = x + (mod_ref[0, row0 + 2:row0 + 3, :] * 0.5) * acc_ref[...]
    if final:
        y = _rms(y, fg_ref[...])
    o_ref[0] = y


def _ffn(x, mod, gain, w1, w3, w2, *, row0, gain_row, final_gain=None):
    b, s, d = x.shape
    tm = min(TOKEN_TILE, s)
    final = final_gain is not None
    in_specs = [pl.BlockSpec((1, tm, d), lambda bi, i: (bi, i, 0)),
                pl.BlockSpec((1, N_MOD, d), lambda bi, i: (bi, 0, 0)),
                _const(gain.shape), _const(w1.shape), _const(w3.shape), _const(w2.shape)]
    args = [x, mod, gain, w1, w3, w2]
    if final:
        in_specs.append(_const((1, d)))
        args.append(final_gain.reshape(1, d))
    return pl.pallas_call(
        functools.partial(_ffn_kernel, row0=row0, gain_row=gain_row, final=final),
        grid=(b, s // tm),
        in_specs=in_specs,
        out_specs=pl.BlockSpec((1, tm, d), lambda bi, i: (bi, i, 0)),
        out_shape=jax.ShapeDtypeStruct(x.shape, F32),
        scratch_shapes=[pltpu.VMEM((tm, d), F32)],
        compiler_params=_params("parallel", "parallel"),
        name="ffn",
    )(*args)


_O_QLAT = 0
_O_KVLAT = _O_QLAT + MLA_Q_RANK
_O_KR = _O_KVLAT + MLA_KV_RANK
_O_DQ = _O_KR + 2 * LANES
_O_RQ = _O_DQ + 2 * BRANCH_W
_O_RV = _O_RQ + 2 * RET_HEADS * RET_K
_W_MAIN = _O_RV + 2 * BRANCH_W


def _rotary(x, cos, sin_signed, half):
    lane = lax.broadcasted_iota(jnp.int32, x.shape, 1)
    first = (lane & (2 * half - 1)) < half
    partner = jnp.where(first, pltpu.roll(x, LANES - half, 1), pltpu.roll(x, half, 1))
    return x * cos + partner * sin_signed


_KEY_OUTPUTS = (1, 2, 4, 5)


def _proj_kernel(x_ref, mod_ref, gain_ref, wm_ref, wqa_ref, wqb_ref, wka_ref, wvt_ref, wdvt_ref,
                 qn_ref, kvn_ref, cm_ref, sm_ref, cd_ref, sd_ref, cr_ref, sr_ref, cmt_ref, smt_ref, *rest):
    mqt_ref, mk_ref, mvt_ref, dq_ref, dk_ref, dvt_ref, rq_ref, rk_ref, rv_ref, rg_ref = rest[-10:]
    x = x_ref[0]
    h = _modulated(x, gain_ref[1:2, :], mod_ref[0, 3:4, :], mod_ref[0, 4:5, :]).astype(BF16)

    def seg(lo, width):
        return _dot(h, wm_ref[:, lo:lo + width])

    qlat = _rms(seg(_O_QLAT, MLA_Q_RANK), qn_ref[...]).astype(BF16)
    kvlat = _rms(seg(_O_KVLAT, MLA_KV_RANK), kvn_ref[...]).astype(BF16)
    cm, sm = cm_ref[...], sm_ref[...]
    cmt, smt = cmt_ref[...], smt_ref[...]
    qat = lax.dot_general(wqa_ref[...], qlat, _NT, preferred_element_type=F32)
    qbt = lax.dot_general(wqb_ref[...], qlat, _NT, preferred_element_type=F32)
    ka = _dot(kvlat, wka_ref[...])
    kr = seg(_O_KR, 2 * LANES)
    krr = kr[:, :LANES] * cm + kr[:, LANES:] * sm
    q_scale = (MLA_NOPE + MLA_ROPE) ** -0.5 * LOG2E
    for hd in range(MLA_HEADS):
        sl = slice(hd * LANES, (hd + 1) * LANES)
        mqt_ref[0, sl, :] = ((qat[sl, :] * cmt + qbt[sl, :] * smt) * q_scale).astype(BF16)
        mk_ref[0, :, sl] = (ka[:, sl] + krr).astype(BF16)
    mvt_ref[0, 0] = lax.dot_general(wvt_ref[...], kvlat, _NT, preferred_element_type=F32).astype(BF16)

    cd, sd = cd_ref[...], sd_ref[...]
    dqk = seg(_O_DQ, 2 * BRANCH_W)
    d_scale = DIFF_QK ** -0.5 * LOG2E
    for j in range(BRANCH_W // LANES):
        sl = slice(j * LANES, (j + 1) * LANES)
        sk = slice(BRANCH_W + j * LANES, BRANCH_W + (j + 1) * LANES)
        dq_ref[0, :, sl] = (_rotary(dqk[:, sl], cd, sd, DIFF_QK // 4) * d_scale).astype(BF16)
        dk_ref[0, :, sl] = _rotary(dqk[:, sk], cd, sd, DIFF_QK // 4).astype(BF16)
    dvt_ref[0, 0] = lax.dot_general(wdvt_ref[...], h, _NT, preferred_element_type=F32).astype(BF16)

    cr, sr_t = cr_ref[...], sr_ref[...]
    hk = RET_HEADS * RET_K
    rqk = seg(_O_RQ, 2 * hk)
    for j in range(hk // LANES):
        sl = slice(j * LANES, (j + 1) * LANES)
        sk = slice(hk + j * LANES, hk + (j + 1) * LANES)
        rq_ref[0, :, sl] = _rotary(rqk[:, sl], cr, sr_t, RET_K // 2).astype(BF16)
        rk_ref[0, :, sl] = (_rotary(rqk[:, sk], cr, sr_t, RET_K // 2) * RET_K ** -0.5).astype(BF16)
    rv_ref[0] = seg(_O_RV, BRANCH_W).astype(BF16)
    rg_ref[0] = seg(_O_RV + BRANCH_W, BRANCH_W).astype(BF16)


def _proj(x, mod, gain, pw, tables, *, n_keys, key_off=0, into=None):
    b, s, d = x.shape
    tm = KEY_CHUNK
    nt = s // tm
    tok = lambda w, off=0: pl.BlockSpec((1, tm, w), lambda bi, i: (bi, i + off, 0))
    slab = pl.BlockSpec((1, 1, BRANCH_W, tm), lambda bi, i: (bi, i + key_off, 0, 0))
    tab = pl.BlockSpec((tm, LANES), lambda bi, i: (i, 0))
    tab_t = pl.BlockSpec((LANES, tm), lambda bi, i: (0, i))
    weights = [pw["main"], pw["qa"], pw["qb"], pw["ka"], pw["vt"], pw["dvt"], pw["qn"], pw["kvn"]]
    hk = RET_HEADS * RET_K
    widths = [MLA_HEADS * LANES, MLA_HEADS * LANES, None, BRANCH_W, BRANCH_W, None, hk, hk, BRANCH_W, BRANCH_W]
    out_specs, out_shape = [], []
    for pos, w in enumerate(widths):
        if pos == 0:
            out_specs.append(pl.BlockSpec((1, w, tm), lambda bi, i: (bi, 0, i)))
            out_shape.append(jax.ShapeDtypeStruct((b, w, s), BF16))
        elif w is None:
            out_specs.append(slab)
            out_shape.append(jax.ShapeDtypeStruct((b, n_keys // tm, BRANCH_W, tm), BF16))
        elif pos in _KEY_OUTPUTS:
            out_specs.append(tok(w, key_off))
            out_shape.append(jax.ShapeDtypeStruct((b, n_keys, w), BF16))
        else:
            out_specs.append(tok(w))
            out_shape.append(jax.ShapeDtypeStruct((b, s, w), BF16))
    in_specs = [pl.BlockSpec((1, tm, d), lambda bi, i: (bi, i, 0)),
                pl.BlockSpec((1, N_MOD, d), lambda bi, i: (bi, 0, 0)),
                _const(gain.shape)] + [_const(w.shape) for w in weights] + [tab] * 6 + [tab_t] * 2
    args = [x, mod, gain, *weights, *tables]
    aliases = {}
    if into is not None:
        for buf, pos in zip(into, _KEY_OUTPUTS):
            aliases[len(args)] = pos
            in_specs.append(pl.BlockSpec(memory_space=pl.ANY))
            args.append(buf)
    return pl.pallas_call(
        _proj_kernel,
        grid=(b, nt),
        in_specs=in_specs,
        out_specs=out_specs,
        out_shape=out_shape,
        input_output_aliases=aliases,
        compiler_params=_params("parallel", "parallel"),
        name="proj",
    )(*args)


MLA_GROUP = 4
SUM_ROWS = 16


def _slabs_per_chunk(n_slabs):
    return next(p for p in (3, 2, 1) if n_slabs % p == 0)


def _attend(chains, k_ref, vt_ref, s_ref, mx_ref, dv, q_transposed):
    n = chains[0][0].shape[1 if q_transposed else 0]
    n_slabs = vt_ref.shape[1]
    per = _slabs_per_chunk(n_slabs)
    tk = per * KEY_CHUNK
    n_chunks = n_slabs // per
    ones = jnp.ones((SUM_ROWS, KEY_CHUNK), BF16)

    def score_slab(c, j, buf, g):
        q_rhs, lanes, _ = chains[g]
        start = c * tk + j * KEY_CHUNK
        if not isinstance(start, int):
            start = pl.multiple_of(start, KEY_CHUNK)
        k = k_ref[0, pl.ds(start, KEY_CHUNK), lanes]
        s = _dot(k, q_rhs) if q_transposed else lax.dot_general(k, q_rhs, _NT, preferred_element_type=F32)
        s_ref[buf, g, j * KEY_CHUNK:(j + 1) * KEY_CHUNK, :] = s
        mx_ref[buf, g, j] = jnp.max(s.reshape(KEY_CHUNK // SUBLANES, SUBLANES, n), axis=0)

    def consume(c, buf, carry, prefetch=False):
        out = []
        for g, ((_, _, rows), (m, acc)) in enumerate(zip(chains, carry)):
            mx = mx_ref[buf, g, 0]
            for j in range(1, per):
                mx = jnp.maximum(mx, mx_ref[buf, g, j])
            m_new = jnp.maximum(m, jnp.max(mx, axis=0, keepdims=True))
            alpha = jnp.exp2(m - m_new)
            pb = jnp.exp2(s_ref[buf, g] - m_new).astype(BF16)
            pv = None
            for j in range(per):
                vt = jnp.concatenate([vt_ref[0, c * per + j, rows, :], ones], axis=0)
                t = _dot(vt, pb[j * KEY_CHUNK:(j + 1) * KEY_CHUNK, :])
                pv = t if pv is None else pv + t
                if prefetch:
                    score_slab(c + 1, j, 1 - buf, g)
            out.append((m_new, alpha * acc + pv))
        return tuple(out)

    def step(c, buf, carry):
        return consume(c, buf, carry, prefetch=True)

    carry = tuple((jnp.full((1, n), -jnp.inf, F32), jnp.zeros((dv + SUM_ROWS, n), F32)) for _ in chains)
    for g in range(len(chains)):
        for j in range(per):
            score_slab(0, j, 0, g)
    n_pairs = (n_chunks - 1) // 2
    if n_pairs > 0:
        carry = lax.fori_loop(0, n_pairs, lambda j, cr: step(2 * j + 1, 1, step(2 * j, 0, cr)), carry)
    if (n_chunks - 1) % 2:
        carry = step(n_chunks - 2, n_chunks % 2, carry)
    carry = consume(n_chunks - 1, (n_chunks - 1) % 2, carry)
    return [(acc[:dv], acc[dv:dv + 1]) for _, acc in carry]


def _kv_specs(k, vt, k_width, v_rows, index, key_range):
    lo, hi = key_range
    n_slabs = hi - lo
    assert lo % n_slabs == 0
    blk = lo // n_slabs
    return [pl.BlockSpec((1, n_slabs * KEY_CHUNK, k_width), lambda *g: (g[0], blk, index(g)),
                         pipeline_mode=pl.Buffered(1)),
            pl.BlockSpec((1, n_slabs, v_rows, KEY_CHUNK), lambda *g: (g[0], blk, index(g), 0),
                         pipeline_mode=pl.Buffered(1))]


def _score_scratch(n_chains, n_slabs, n):
    per = _slabs_per_chunk(n_slabs)
    return [pltpu.VMEM((2, n_chains, per * KEY_CHUNK, n), F32),
            pltpu.VMEM((2, n_chains, per, SUBLANES, n), F32)]


def _mla_kernel(q_ref, k_ref, vt_ref, o_ref, s_ref, mx_ref):
    chains = []
    for g in range(MLA_GROUP):
        lanes = slice(g * LANES, (g + 1) * LANES)
        chains.append((q_ref[0, lanes, :], lanes, slice(g * MLA_V, (g + 1) * MLA_V)))
    for (_, _, rows), (acc, l) in zip(chains, _attend(chains, k_ref, vt_ref, s_ref, mx_ref, MLA_V, True)):
        o_ref[0, rows, :] = (acc / l).astype(BF16)


def _mla_attention(q, k, vt, key_range):
    b, _, s = q.shape
    tq = min(MLA_Q_TILE, s)
    n_slabs = key_range[1] - key_range[0]
    return pl.pallas_call(
        _mla_kernel,
        grid=(b, MLA_HEADS // MLA_GROUP, s // tq),
        in_specs=[pl.BlockSpec((1, MLA_GROUP * LANES, tq), lambda bi, hg, i: (bi, hg, i))]
                 + _kv_specs(k, vt, MLA_GROUP * LANES, MLA_GROUP * MLA_V, lambda g: g[1], key_range),
        out_specs=pl.BlockSpec((1, MLA_GROUP * MLA_V, tq), lambda bi, hg, i: (bi, hg, i)),
        out_shape=jax.ShapeDtypeStruct((b, MLA_HEADS * MLA_V, s), BF16),
        scratch_shapes=_score_scratch(MLA_GROUP, n_slabs, tq),
        compiler_params=_params("parallel", "parallel", "parallel"),
        name="mla_attn",
    )(q, k, vt)


def _diff_kernel(lam_ref, q_ref, k_ref, vt_ref, gain_ref, o_ref, s_ref, mx_ref, *, out_scale):
    tq = q_ref.shape[1]
    first = lax.broadcasted_iota(jnp.int32, (tq, LANES), 1) < DIFF_QK
    zero = jnp.zeros((tq, LANES), BF16)
    chains = []
    for hd in range(DIFF_HEADS):
        lanes = slice(hd * LANES, (hd + 1) * LANES)
        q = q_ref[0, :, lanes]
        q_rhs = jnp.concatenate([jnp.where(first, q, zero), jnp.where(first, zero, q)], axis=0)
        chains.append((q_rhs, lanes, slice(hd * DIFF_V, (hd + 1) * DIFF_V)))
    for (_, _, rows), (acc, l) in zip(chains, _attend(chains, k_ref, vt_ref, s_ref, mx_ref, DIFF_V, False)):
        o = acc / l
        o = o[:, :tq] - lam_ref[0] * o[:, tq:]
        o = o * lax.rsqrt(jnp.mean(o * o, axis=0, keepdims=True) + EPS) * gain_ref[...]
        o_ref[0, rows, :] = (o * out_scale).astype(BF16)


def _diff_attention(q, k, vt, key_range, lam, gain, lambda_init):
    b, s, w = q.shape
    tq = min(DIFF_Q_TILE, s)
    n_slabs = key_range[1] - key_range[0]
    return pl.pallas_call(
        functools.partial(_diff_kernel, out_scale=1.0 - lambda_init),
        grid=(b, s // tq),
        in_specs=[pl.BlockSpec(memory_space=pltpu.SMEM),
                  pl.BlockSpec((1, tq, w), lambda bi, i: (bi, i, 0))]
                 + _kv_specs(k, vt, w, DIFF_HEADS * DIFF_V, lambda g: 0, key_range)
                 + [pl.BlockSpec((DIFF_V, 1), lambda bi, i: (0, 0))],
        out_specs=pl.BlockSpec((1, DIFF_HEADS * DIFF_V, tq), lambda bi, i: (bi, 0, i)),
        out_shape=jax.ShapeDtypeStruct((b, DIFF_HEADS * DIFF_V, s), BF16),
        scratch_shapes=_score_scratch(DIFF_HEADS, n_slabs, 2 * tq),
        compiler_params=_params("parallel", "parallel"),
        name="diff_attn",
    )(lam.reshape(1), q, k, vt, gain.reshape(DIFF_V, 1))


def _ret_kernel(q_ref, k_ref, v_ref, s0_ref, intra_ref, qdec_ref, kdec_ref, cdec_ref, *rest, reverse, post):
    if post:
        of_ref, g_ref, gain_ref, o_ref, sout_ref, state = rest
    else:
        o_ref, sout_ref, state = rest
    i = pl.program_id(1)

    @pl.when(i == 0)
    def _():
        state[...] = s0_ref[0]

    n_chunks = q_ref.shape[1] // RET_CHUNK
    lane = lax.broadcasted_iota(jnp.int32, (RET_CHUNK, RET_HEADS * RET_K), 1)
    masks = [(lane >= hd * RET_K) & (lane < (hd + 1) * RET_K) for hd in range(RET_HEADS)]
    order = [n_chunks - 1 - cc if reverse else cc for cc in range(n_chunks)]
    heads = range(RET_HEADS)

    def head_v(c, hd):
        return v_ref[0, c * RET_CHUNK:(c + 1) * RET_CHUNK, hd * RET_V:(hd + 1) * RET_V]

    intra, deltas = {}, {}
    for c in order:
        rows = slice(c * RET_CHUNK, (c + 1) * RET_CHUNK)
        qc, kc = q_ref[0, rows, :], k_ref[0, rows, :]
        kdt = (kc.astype(F32) * kdec_ref[...]).T.astype(BF16)
        for hd in heads:
            a = lax.dot_general(jnp.where(masks[hd], qc, jnp.zeros_like(qc)), kc, _NT,
                                preferred_element_type=F32)
            intra[c, hd] = (a * intra_ref[hd]).astype(BF16)
            deltas[c, hd] = _dot(kdt[hd * RET_K:(hd + 1) * RET_K, :], head_v(c, hd))
    st = state[...]
    entering = {}
    for c in order:
        entering[c] = st.astype(BF16)
        st = st * cdec_ref[...] + jnp.concatenate([deltas[c, hd] for hd in heads], axis=0)
    state[...] = st
    for c in order:
        rows = slice(c * RET_CHUNK, (c + 1) * RET_CHUNK)
        qd = (q_ref[0, rows, :].astype(F32) * qdec_ref[...]).astype(BF16)
        outs = [_dot(intra[c, hd], head_v(c, hd))
                + _dot(jnp.where(masks[hd], qd, jnp.zeros_like(qd)), entering[c]) for hd in heads]
        if post:
            of = of_ref[0, rows, :]
            g = g_ref[0, rows, :].astype(F32)
            for hd in range(RET_HEADS):
                sl = slice(hd * RET_V, (hd + 1) * RET_V)
                outs[hd] = _rms(outs[hd] + of[:, sl], gain_ref[...]) * (g[:, sl] * jax.nn.sigmoid(g[:, sl]))
        o_ref[0, rows, :] = jnp.concatenate(outs, axis=1).astype(o_ref.dtype)

    @pl.when(i == pl.num_programs(1) - 1)
    def _():
        sout_ref[0] = state[...]


def _retention(q, k, v, s0, tabs, *, reverse, post=None):
    b, n, _ = q.shape
    t = min(TOKEN_TILE, n)
    nt = n // t
    hk, hv = RET_HEADS * RET_K, RET_HEADS * RET_V
    tile = (lambda bi, i: (bi, nt - 1 - i, 0)) if reverse else (lambda bi, i: (bi, i, 0))
    st_spec = pl.BlockSpec((1, hk, RET_V), lambda bi, i: (bi, 0, 0))
    in_specs = [pl.BlockSpec((1, t, hk), tile), pl.BlockSpec((1, t, hk), tile), pl.BlockSpec((1, t, hv), tile),
                st_spec] + [_const(a.shape) for a in tabs]
    args = [q, k, v, s0, *tabs]
    if post is not None:
        o_f, g, gain = post
        in_specs += [pl.BlockSpec((1, t, hv), tile), pl.BlockSpec((1, t, hv), tile), _const((1, RET_V))]
        args += [o_f, g, gain.reshape(1, RET_V)]
    return pl.pallas_call(
        functools.partial(_ret_kernel, reverse=reverse, post=post is not None),
        grid=(b, nt),
        in_specs=in_specs,
        out_specs=[pl.BlockSpec((1, t, hv), tile), st_spec],
        out_shape=[jax.ShapeDtypeStruct((b, n, hv), F32 if post is None else BF16),
                   jax.ShapeDtypeStruct((b, hk, RET_V), F32)],
        scratch_shapes=[pltpu.VMEM((hk, RET_V), F32)],
        compiler_params=_params("parallel", "arbitrary"),
        name="ret_bwd" if reverse else "ret_fwd",
    )(*args)


def _decay_tables(log_gamma, reverse):
    idx = jnp.arange(RET_CHUNK, dtype=F32)
    lg = log_gamma.astype(F32)[:, None]
    rel = idx[:, None] - idx[None, :]
    if reverse:
        rel = -rel
        q_pow, k_pow = RET_CHUNK - idx, idx
    else:
        q_pow, k_pow = idx + 1.0, RET_CHUNK - 1.0 - idx
    intra = jnp.where(rel >= 0, jnp.exp(lg[:, :, None] * jnp.maximum(rel, 0.0)), 0.0)
    qdec = jnp.repeat(jnp.exp(lg * q_pow).T, RET_K, axis=1)
    kdec = jnp.repeat(jnp.exp(lg * k_pow).T, RET_K, axis=1)
    cdec = jnp.broadcast_to(jnp.repeat(jnp.exp(lg * RET_CHUNK), RET_K, axis=0), (RET_HEADS * RET_K, RET_V))
    return intra, qdec, kdec, cdec


def _merge_kernel(x_ref, mod_ref, gain_ref, wg_ref, wb_ref, wo_ref, mla_ref, dif_ref, ret_ref, o_ref):
    x = x_ref[0]
    d = x.shape[1]
    h = _modulated(x, gain_ref[1:2, :], mod_ref[0, 3:4, :], mod_ref[0, 4:5, :]).astype(BF16)
    branches = (lax.dot_general(mla_ref[0], wb_ref[0], _TN, preferred_element_type=F32),
                lax.dot_general(dif_ref[0], wb_ref[1], _TN, preferred_element_type=F32),
                _dot(ret_ref[0], wb_ref[2]))
    y = None
    for j, br in enumerate(branches):
        t = jax.nn.sigmoid(_dot(h, wg_ref[:, j * d:(j + 1) * d])) * br
        y = t if y is None else y + t
    o_ref[0] = x + mod_ref[0, 5:6, :] * _dot(y.astype(BF16), wo_ref[...])


def _merge(x, mod, gain, wg, wb, wo, mla_t, dif_t, ret):
    b, s, d = x.shape
    tm = min(TOKEN_TILE, s)
    tok = lambda w: pl.BlockSpec((1, tm, w), lambda bi, i: (bi, i, 0))
    feat = pl.BlockSpec((1, BRANCH_W, tm), lambda bi, i: (bi, 0, i))
    return pl.pallas_call(
        _merge_kernel,
        grid=(b, s // tm),
        in_specs=[tok(d), pl.BlockSpec((1, N_MOD, d), lambda bi, i: (bi, 0, 0)), _const(gain.shape),
                  _const(wg.shape), _const(wb.shape), _const(wo.shape), feat, feat, tok(BRANCH_W)],
        out_specs=tok(d),
        out_shape=jax.ShapeDtypeStruct(x.shape, F32),
        compiler_params=_params("parallel", "parallel"),
        name="merge",
    )(x, mod, gain, wg, wb, wo, mla_t, dif_t, ret)


def _rot_cols(w, group):
    r, n = w.shape
    w = w.reshape(r, n // group, 2, group // 2)
    return jnp.stack([-w[:, :, 1], w[:, :, 0]], axis=2).reshape(r, n)


def _head_slots(w, n_heads, width):
    r = w.shape[0]
    w = w.reshape(r, n_heads, width)
    return jnp.pad(w, ((0, 0), (0, 0), (0, LANES - width))).reshape(r, n_heads * LANES)


def _proj_weights(w_in, w_qb, w_kvb, q_norm, kv_norm):
    sizes = (MLA_Q_RANK, MLA_KV_RANK, MLA_ROPE, 2 * DIFF_HEADS * DIFF_QK, 2 * DIFF_HEADS * DIFF_QK,
             DIFF_HEADS * DIFF_V, RET_HEADS * RET_K, RET_HEADS * RET_K, RET_HEADS * RET_V, RET_HEADS * RET_V)
    offs = [0]
    for sz in sizes:
        offs.append(offs[-1] + sz)
    p = [w_in[:, offs[i]:offs[i + 1]] for i in range(len(sizes))]
    gates = w_in[:, offs[-1]:]
    d = w_in.shape[0]
    lead = jnp.zeros((d, MLA_NOPE), F32)
    tail = jnp.zeros((d, LANES - MLA_NOPE - MLA_ROPE), F32)
    main = jnp.concatenate([
        p[0], p[1],
        lead, p[2], tail, lead, _rot_cols(p[2], MLA_ROPE // 2), tail,
        p[3], p[4], p[6], p[7], p[8], p[9]], axis=1)
    assert main.shape[1] == _W_MAIN
    hd = MLA_NOPE + MLA_ROPE
    qb = w_qb.reshape(MLA_Q_RANK, MLA_HEADS, hd)
    q_rot = _rot_cols(qb[:, :, MLA_NOPE:].reshape(MLA_Q_RANK, MLA_HEADS * MLA_ROPE), MLA_ROPE // 2)
    q_rot = jnp.concatenate([jnp.zeros((MLA_Q_RANK, MLA_HEADS, MLA_NOPE), F32),
                             q_rot.reshape(MLA_Q_RANK, MLA_HEADS, MLA_ROPE)], axis=2)
    kvb = w_kvb.reshape(MLA_KV_RANK, MLA_HEADS, MLA_NOPE + MLA_V)
    return {
        "main": main.astype(BF16),
        "qa": _head_slots(w_qb, MLA_HEADS, hd).T.astype(BF16),
        "qb": _head_slots(q_rot.reshape(MLA_Q_RANK, MLA_HEADS * hd), MLA_HEADS, hd).T.astype(BF16),
        "ka": _head_slots(kvb[:, :, :MLA_NOPE].reshape(MLA_KV_RANK, -1), MLA_HEADS, MLA_NOPE).astype(BF16),
        "vt": kvb[:, :, MLA_NOPE:].reshape(MLA_KV_RANK, -1).T.astype(BF16),
        "dvt": p[5].T.astype(BF16),
        "qn": q_norm.reshape(1, -1),
        "kvn": kv_norm.reshape(1, -1),
    }, gates.astype(BF16)


def _angles(pos, dim):
    inv = ROPE_BASE ** (-jnp.arange(0, dim, 2, dtype=F32) / dim)
    ang = pos.astype(F32)[:, None] * inv[None, :]
    return jnp.cos(ang), jnp.sin(ang)


def _rope_tables(n):
    t = jnp.arange(n)
    row = jnp.repeat(jnp.arange(n // GRID_W), GRID_W)
    col = t - row * GRID_W

    def axial(dim):
        cr, sr = _angles(row, dim // 2)
        cc, sc = _angles(col, dim // 2)
        return jnp.concatenate([cr, cr, cc, cc], axis=1), jnp.concatenate([sr, sr, sc, sc], axis=1)

    cm, sm = axial(MLA_ROPE)
    ones = jnp.ones((n, MLA_NOPE), F32)
    pad = jnp.zeros((n, LANES - MLA_NOPE - MLA_ROPE), F32)
    cm = jnp.concatenate([ones, cm, pad], axis=1)
    sm = jnp.concatenate([0 * ones, sm, pad], axis=1)
    cd, sd = axial(DIFF_QK)
    cs, ss = _angles(t, RET_K)
    sd = sd * jnp.repeat(jnp.array([-1.0, 1.0, -1.0, 1.0], F32), DIFF_QK // 4)
    return (cm, sm, jnp.tile(cd, (1, 2)), jnp.tile(sd, (1, 2)),
            jnp.tile(cs, (1, 4)), jnp.tile(jnp.concatenate([-ss, ss], axis=1), (1, 2)),
            cm.T, sm.T)


def _identity_tables(n):
    one, zero = jnp.ones((n, LANES), F32), jnp.zeros((n, LANES), F32)
    return one, zero, one, zero, one, zero, one.T, zero.T


def _chunked(w, axis):
    if axis == 1:
        r = w.shape[0]
        return w.reshape(r, D_FF // FF_CHUNK, FF_CHUNK).transpose(1, 0, 2).astype(BF16)
    return w.reshape(D_FF // FF_CHUNK, FF_CHUNK, w.shape[1]).astype(BF16)


def kernel(x, c, ctx, c_ctx, ada_w, ada_b, norm_gain, ffn1_w1, ffn1_w3, ffn1_w2, ffn2_w1, ffn2_w3, ffn2_w2,
           w_in, mla_q_norm, mla_w_qb, mla_kv_norm, mla_w_kvb, diff_lambda, diff_norm, ret_decay, ret_norm,
           w_branch, w_out, final_norm):
    b, n, d = x.shape
    n_ctx = ctx.shape[1]
    pad_rows = -(b + 1) % 8
    cs = jnp.concatenate([c, c_ctx[None, :], jnp.zeros((pad_rows, d), F32)], axis=0)
    mods = _ada(cs, ada_w, ada_b)
    x_tabs, c_tabs = _rope_tables(n), _identity_tables(n_ctx)
    zero_state = jnp.zeros((b, RET_HEADS * RET_K, RET_V), F32)
    n_x, n_all = n // KEY_CHUNK, (n + n_ctx) // KEY_CHUNK
    xc = ctx
    for l in range(DEPTH):
        lambda_init = 0.8 - 0.6 * math.exp(-0.3 * l)
        ctx_out = l < DEPTH - 1
        mx = mods[l, :b].reshape(b, N_MOD, d)
        mc = jnp.broadcast_to(mods[l, b].reshape(1, N_MOD, d), (b, N_MOD, d))
        gain = norm_gain[l]
        f1 = (_chunked(ffn1_w1[l], 1), _chunked(ffn1_w3[l], 1), _chunked(ffn1_w2[l], 0))
        f2 = (_chunked(ffn2_w1[l], 1), _chunked(ffn2_w3[l], 1), _chunked(ffn2_w2[l], 0))
        pw, wg = _proj_weights(w_in[l], mla_w_qb[l], mla_w_kvb[l], mla_q_norm[l], mla_kv_norm[l])
        wb, wo = w_branch[l].astype(BF16), w_out[l].astype(BF16)
        dl = diff_lambda[l].astype(F32)
        lam = jnp.exp(jnp.sum(dl[0] * dl[1])) - jnp.exp(jnp.sum(dl[2] * dl[3])) + lambda_init
        log_gamma = -jnp.exp(ret_decay[l].astype(F32))
        tab_f, tab_b = _decay_tables(log_gamma[0], False), _decay_tables(log_gamma[1], True)

        x = _ffn(x, mx, gain, *f1, row0=0, gain_row=0)
        xc = _ffn(xc, mc, gain, *f1, row0=0, gain_row=0)

        mq_t, mk, mvt, dq, dk, dvt, rq, rk, rv, rg = _proj(x, mx, gain, pw, x_tabs, n_keys=n + n_ctx)
        cmq_t, mk, mvt, cdq, dk, dvt, crq, crk, crv, crg = _proj(
            xc, mc, gain, pw, c_tabs, n_keys=n + n_ctx, key_off=n_x, into=(mk, mvt, dk, dvt))

        mla_t = _mla_attention(mq_t, mk, mvt, (0, n_all))
        dif_t = _diff_attention(dq, dk, dvt, (0, n_all), lam, diff_norm[l], lambda_init)

        oc_f, st_f = _retention(crq, crk, crv, zero_state, tab_f, reverse=False)
        ret_c, st_b = _retention(crq, crk, crv, zero_state, tab_b, reverse=True,
                                 post=(oc_f, crg, ret_norm[l]) if ctx_out else None)
        ox_f, _ = _retention(rq, rk, rv, st_f, tab_f, reverse=False)
        ret_x, _ = _retention(rq, rk, rv, st_b, tab_b, reverse=True, post=(ox_f, rg, ret_norm[l]))

        x = _merge(x, mx, gain, wg, wb, wo, mla_t, dif_t, ret_x)
        x = _ffn(x, mx, gain, *f2, row0=6, gain_row=2, final_gain=None if ctx_out else final_norm)
        if ctx_out:
            mla_c = _mla_attention(cmq_t, mk, mvt, (n_x, n_all))
            dif_c = _diff_attention(cdq, dk, dvt, (n_x, n_all), lam, diff_norm[l], lambda_init)
            xc = _merge(xc, mc, gain, wg, wb, wo, mla_c, dif_c, ret_c)
            xc = _ffn(xc, mc, gain, *f2, row0=6, gain_row=2)
    return x
```

```python
import functools
import math

import jax
import jax.numpy as jnp
from jax import lax
from jax.experimental import pallas as pl
from jax.experimental.pallas import tpu as pltpu

D_MODEL = 1024
DEPTH = 2
GRID_W = 64
N_BRANCH = 3
MLA_HEADS = 8
MLA_Q_RANK = 384
MLA_KV_RANK = 256
MLA_NOPE = 64
MLA_ROPE = 32
MLA_V = 64
DIFF_HEADS = 4
DIFF_QK = 64
DIFF_V = 128
RET_HEADS = 4
RET_K = 64
RET_V = 128
RET_CHUNK = 128
BRANCH_W = 512
D_FF = 2816
N_MOD = 9
ROPE_BASE = 10000.0
EPS = 1e-6

LANES = 128
SUBLANES = 8
LOG2E = 1.4426950408889634
FF_CHUNK = 256
KEY_CHUNK = 256
TOKEN_TILE = 512
MLA_Q_TILE = 512
DIFF_Q_TILE = 256
VMEM_LIMIT = 56 * 1024 * 1024

BF16 = jnp.bfloat16
F32 = jnp.float32
_NT = (((1,), (1,)), ((), ()))
_TN = (((0,), (0,)), ((), ()))


def _params(*sem):
    return pltpu.CompilerParams(dimension_semantics=sem, vmem_limit_bytes=VMEM_LIMIT)


def _const(shape):
    zeros = (0,) * len(shape)
    return pl.BlockSpec(shape, lambda *_: zeros, pipeline_mode=pl.Buffered(1))


def _rms(x, gain):
    return x * lax.rsqrt(jnp.mean(x * x, axis=-1, keepdims=True) + EPS) * gain


def _modulated(x, gain, shift, scale):
    return _rms(x, gain) * (1.0 + scale) + shift


def _dot(a, b):
    return jnp.dot(a, b, preferred_element_type=F32)


def _ada_kernel(c_ref, w_ref, b_ref, o_ref):
    c = c_ref[...]
    o_ref[0] = _dot(c * jax.nn.sigmoid(c), w_ref[0]) + b_ref[0]


def _ada(cs, ada_w, ada_b):
    nl, d, n = ada_w.shape
    r = cs.shape[0]
    tn = n // N_MOD
    return pl.pallas_call(
        _ada_kernel,
        grid=(nl, n // tn),
        in_specs=[pl.BlockSpec((r, d), lambda l, j: (0, 0)),
                  pl.BlockSpec((1, d, tn), lambda l, j: (l, 0, j)),
                  pl.BlockSpec((1, 1, tn), lambda l, j: (l, 0, j))],
        out_specs=pl.BlockSpec((1, r, tn), lambda l, j: (l, 0, j)),
        out_shape=jax.ShapeDtypeStruct((nl, r, n), F32),
        compiler_params=_params("parallel", "parallel"),
        name="ada",
    )(cs, ada_w, ada_b.reshape(nl, 1, n))


def _ffn_kernel(x_ref, mod_ref, gain_ref, w1_ref, w3_ref, w2_ref, *rest, row0, gain_row, final):
    if final:
        fg_ref, o_ref, acc_ref = rest
    else:
        o_ref, acc_ref = rest
    x = x_ref[0]
    h = _modulated(x, gain_ref[gain_row:gain_row + 1, :], mod_ref[0, row0:row0 + 1, :],
                   mod_ref[0, row0 + 1:row0 + 2, :]).astype(BF16)
    n_chunks = w1_ref.shape[0]
    up = (_dot(h, w1_ref[0]), _dot(h, w3_ref[0]))
    for c in range(n_chunks):
        a, b = up
        if c + 1 < n_chunks:
            up = (_dot(h, w1_ref[c + 1]), _dot(h, w3_ref[c + 1]))
        t = _dot((a * jax.nn.sigmoid(a) * b).astype(BF16), w2_ref[c])
        if c == 0:
            acc_ref[...] = t
        else:
            acc_ref[...] += t
    y = x + (mod_ref[0, row0 + 2:row0 + 3, :] * 0.5) * acc_ref[...]
    if final:
        y = _rms(y, fg_ref[...])
    o_ref[0] = y


def _ffn(x, mod, gain, w1, w3, w2, *, row0, gain_row, final_gain=None):
    b, s, d = x.shape
    tm = min(TOKEN_TILE, s)
    final = final_gain is not None
    in_specs = [pl.BlockSpec((1, tm, d), lambda bi, i: (bi, i, 0)),
                pl.BlockSpec((1, N_MOD, d), lambda bi, i: (bi, 0, 0)),
                _const(gain.shape), _const(w1.shape), _const(w3.shape), _const(w2.shape)]
    args = [x, mod, gain, w1, w3, w2]
    if final:
        in_specs.append(_const((1, d)))
        args.append(final_gain.reshape(1, d))
    return pl.pallas_call(
        functools.partial(_ffn_kernel, row0=row0, gain_row=gain_row, final=final),
        grid=(b, s // tm),
        in_specs=in_specs,
        out_specs=pl.BlockSpec((1, tm, d), lambda bi, i: (bi, i, 0)),
        out_shape=jax.ShapeDtypeStruct(x.shape, F32),
        scratch_shapes=[pltpu.VMEM((tm, d), F32)],
        compiler_params=_params("parallel", "parallel"),
        name="ffn",
    )(*args)


_O_QLAT = 0
_O_KVLAT = _O_QLAT + MLA_Q_RANK
_O_KR = _O_KVLAT + MLA_KV_RANK
_O_DQ = _O_KR + 2 * LANES
_O_RQ = _O_DQ + 2 * BRANCH_W
_O_RV = _O_RQ + 2 * RET_HEADS * RET_K
_W_MAIN = _O_RV + 2 * BRANCH_W


def _rotary(x, cos, sin_signed, half):
    lane = lax.broadcasted_iota(jnp.int32, x.shape, 1)
    first = (lane & (2 * half - 1)) < half
    partner = jnp.where(first, pltpu.roll(x, LANES - half, 1), pltpu.roll(x, half, 1))
    return x * cos + partner * sin_signed


_KEY_OUTPUTS = (1, 2, 4, 5)


def _proj_kernel(x_ref, mod_ref, gain_ref, wm_ref, wqa_ref, wqb_ref, wka_ref, wvt_ref, wdvt_ref,
                 qn_ref, kvn_ref, cm_ref, sm_ref, cd_ref, sd_ref, cr_ref, sr_ref, cmt_ref, smt_ref, *rest):
    mqt_ref, mk_ref, mvt_ref, dq_ref, dk_ref, dvt_ref, rq_ref, rk_ref, rv_ref, rg_ref = rest[-10:]
    x = x_ref[0]
    h = _modulated(x, gain_ref[1:2, :], mod_ref[0, 3:4, :], mod_ref[0, 4:5, :]).astype(BF16)

    def seg(lo, width):
        return _dot(h, wm_ref[:, lo:lo + width])

    qlat = _rms(seg(_O_QLAT, MLA_Q_RANK), qn_ref[...]).astype(BF16)
    kvlat = _rms(seg(_O_KVLAT, MLA_KV_RANK), kvn_ref[...]).astype(BF16)
    cm, sm = cm_ref[...], sm_ref[...]
    cmt, smt = cmt_ref[...], smt_ref[...]
    qat = lax.dot_general(wqa_ref[...], qlat, _NT, preferred_element_type=F32)
    qbt = lax.dot_general(wqb_ref[...], qlat, _NT, preferred_element_type=F32)
    ka = _dot(kvlat, wka_ref[...])
    kr = seg(_O_KR, 2 * LANES)
    krr = kr[:, :LANES] * cm + kr[:, LANES:] * sm
    q_scale = (MLA_NOPE + MLA_ROPE) ** -0.5 * LOG2E
    for hd in range(MLA_HEADS):
        sl = slice(hd * LANES, (hd + 1) * LANES)
        mqt_ref[0, sl, :] = ((qat[sl, :] * cmt + qbt[sl, :] * smt) * q_scale).astype(BF16)
        mk_ref[0, :, sl] = (ka[:, sl] + krr).astype(BF16)
    mvt_ref[0, 0] = lax.dot_general(wvt_ref[...], kvlat, _NT, preferred_element_type=F32).astype(BF16)

    cd, sd = cd_ref[...], sd_ref[...]
    dqk = seg(_O_DQ, 2 * BRANCH_W)
    d_scale = DIFF_QK ** -0.5 * LOG2E
    for j in range(BRANCH_W // LANES):
        sl = slice(j * LANES, (j + 1) * LANES)
        sk = slice(BRANCH_W + j * LANES, BRANCH_W + (j + 1) * LANES)
        dq_ref[0, :, sl] = (_rotary(dqk[:, sl], cd, sd, DIFF_QK // 4) * d_scale).astype(BF16)
        dk_ref[0, :, sl] = _rotary(dqk[:, sk], cd, sd, DIFF_QK // 4).astype(BF16)
    dvt_ref[0, 0] = lax.dot_general(wdvt_ref[...], h, _NT, preferred_element_type=F32).astype(BF16)

    cr, sr_t = cr_ref[...], sr_ref[...]
    hk = RET_HEADS * RET_K
    rqk = seg(_O_RQ, 2 * hk)
    for j in range(hk // LANES):
        sl = slice(j * LANES, (j + 1) * LANES)
        sk = slice(hk + j * LANES, hk + (j + 1) * LANES)
        rq_ref[0, :, sl] = _rotary(rqk[:, sl], cr, sr_t, RET_K // 2).astype(BF16)
        rk_ref[0, :, sl] = (_rotary(rqk[:, sk], cr, sr_t, RET_K // 2) * RET_K ** -0.5).astype(BF16)
    rv_ref[0] = seg(_O_RV, BRANCH_W).astype(BF16)
    rg_ref[0] = seg(_O_RV + BRANCH_W, BRANCH_W).astype(BF16)


def _proj(x, mod, gain, pw, tables, *, n_keys, key_off=0, into=None):
    b, s, d = x.shape
    tm = KEY_CHUNK
    nt = s // tm
    tok = lambda w, off=0: pl.BlockSpec((1, tm, w), lambda bi, i: (bi, i + off, 0))
    slab = pl.BlockSpec((1, 1, BRANCH_W, tm), lambda bi, i: (bi, i + key_off, 0, 0))
    tab = pl.BlockSpec((tm, LANES), lambda bi, i: (i, 0))
    tab_t = pl.BlockSpec((LANES, tm), lambda bi, i: (0, i))
    weights = [pw["main"], pw["qa"], pw["qb"], pw["ka"], pw["vt"], pw["dvt"], pw["qn"], pw["kvn"]]
    hk = RET_HEADS * RET_K
    widths = [MLA_HEADS * LANES, MLA_HEADS * LANES, None, BRANCH_W, BRANCH_W, None, hk, hk, BRANCH_W, BRANCH_W]
    out_specs, out_shape = [], []
    for pos, w in enumerate(widths):
        if pos == 0:
            out_specs.append(pl.BlockSpec((1, w, tm), lambda bi, i: (bi, 0, i)))
            out_shape.append(jax.ShapeDtypeStruct((b, w, s), BF16))
        elif w is None:
            out_specs.append(slab)
            out_shape.append(jax.ShapeDtypeStruct((b, n_keys // tm, BRANCH_W, tm), BF16))
        elif pos in _KEY_OUTPUTS:
            out_specs.append(tok(w, key_off))
            out_shape.append(jax.ShapeDtypeStruct((b, n_keys, w), BF16))
        else:
            out_specs.append(tok(w))
            out_shape.append(jax.ShapeDtypeStruct((b, s, w), BF16))
    in_specs = [pl.BlockSpec((1, tm, d), lambda bi, i: (bi, i, 0)),
                pl.BlockSpec((1, N_MOD, d), lambda bi, i: (bi, 0, 0)),
                _const(gain.shape)] + [_const(w.shape) for w in weights] + [tab] * 6 + [tab_t] * 2
    args = [x, mod, gain, *weights, *tables]
    aliases = {}
    if into is not None:
        for buf, pos in zip(into, _KEY_OUTPUTS):
            aliases[len(args)] = pos
            in_specs.append(pl.BlockSpec(memory_space=pl.ANY))
            args.append(buf)
    return pl.pallas_call(
        _proj_kernel,
        grid=(b, nt),
        in_specs=in_specs,
        out_specs=out_specs,
        out_shape=out_shape,
        input_output_aliases=aliases,
        compiler_params=_params("parallel", "parallel"),
        name="proj",
    )(*args)


MLA_GROUP = 4
SUM_ROWS = 16


def _slabs_per_chunk(n_slabs):
    return next(p for p in (3, 2, 1) if n_slabs % p == 0)


def _attend(chains, k_ref, vt_ref, s_ref, mx_ref, dv, *, q_transposed, slab_interleave):
    n = chains[0][0].shape[1 if q_transposed else 0]
    n_slabs = vt_ref.shape[1]
    per = _slabs_per_chunk(n_slabs)
    tk = per * KEY_CHUNK
    n_chunks = n_slabs // per
    ones = jnp.ones((SUM_ROWS, KEY_CHUNK), BF16)

    def score(c, buf, g, slabs):
        q_rhs, lanes, _ = chains[g]
        lo, rows = slabs[0] * KEY_CHUNK, len(slabs) * KEY_CHUNK
        start = c * tk + lo
        if not isinstance(start, int):
            start = pl.multiple_of(start, KEY_CHUNK)
        k = k_ref[0, pl.ds(start, rows), lanes]
        s = _dot(k, q_rhs) if q_transposed else lax.dot_general(k, q_rhs, _NT, preferred_element_type=F32)
        s_ref[buf, g, lo:lo + rows, :] = s
        for i, j in enumerate(slabs):
            part = s[i * KEY_CHUNK:(i + 1) * KEY_CHUNK]
            mx_ref[buf, g, j] = jnp.max(part.reshape(KEY_CHUNK // SUBLANES, SUBLANES, n), axis=0)

    def consume(c, buf, carry, prefetch):
        out = []
        for g, ((_, _, rows), (m, acc)) in enumerate(zip(chains, carry)):
            if prefetch and not slab_interleave:
                score(c + 1, 1 - buf, g, range(per))
            mx = mx_ref[buf, g, 0]
            for j in range(1, per):
                mx = jnp.maximum(mx, mx_ref[buf, g, j])
            m_new = jnp.maximum(m, jnp.max(mx, axis=0, keepdims=True))
            alpha = jnp.exp2(m - m_new)
            pb = jnp.exp2(s_ref[buf, g] - m_new).astype(BF16)
            pv = None
            for j in range(per):
                vt = jnp.concatenate([vt_ref[0, c * per + j, rows, :], ones], axis=0)
                t = _dot(vt, pb[j * KEY_CHUNK:(j + 1) * KEY_CHUNK, :])
                pv = t if pv is None else pv + t
                if prefetch and slab_interleave:
                    score(c + 1, 1 - buf, g, [j])
            out.append((m_new, alpha * acc + pv))
        return tuple(out)

    def step(c, buf, carry):
        return consume(c, buf, carry, True)

    carry = tuple((jnp.full((1, n), -jnp.inf, F32), jnp.zeros((dv + SUM_ROWS, n), F32)) for _ in chains)
    for g in range(len(chains)):
        for j in range(per):
            score(0, 0, g, [j])
    n_pairs = (n_chunks - 1) // 2
    if n_pairs > 0:
        carry = lax.fori_loop(0, n_pairs, lambda j, cr: step(2 * j + 1, 1, step(2 * j, 0, cr)), carry)
    if (n_chunks - 1) % 2:
        carry = step(n_chunks - 2, n_chunks % 2, carry)
    carry = consume(n_chunks - 1, (n_chunks - 1) % 2, carry, False)
    return [(acc[:dv], acc[dv:dv + 1]) for _, acc in carry]


def _kv_specs(k, vt, k_width, v_rows, index, key_range):
    lo, hi = key_range
    n_slabs = hi - lo
    assert lo % n_slabs == 0
    blk = lo // n_slabs
    return [pl.BlockSpec((1, n_slabs * KEY_CHUNK, k_width), lambda *g: (g[0], blk, index(g)),
                         pipeline_mode=pl.Buffered(1)),
            pl.BlockSpec((1, n_slabs, v_rows, KEY_CHUNK), lambda *g: (g[0], blk, index(g), 0),
                         pipeline_mode=pl.Buffered(1))]


def _score_scratch(n_chains, n_slabs, n):
    per = _slabs_per_chunk(n_slabs)
    return [pltpu.VMEM((2, n_chains, per * KEY_CHUNK, n), F32),
            pltpu.VMEM((2, n_chains, per, SUBLANES, n), F32)]


def _mla_kernel(q_ref, k_ref, vt_ref, o_ref, s_ref, mx_ref):
    chains = []
    for g in range(MLA_GROUP):
        lanes = slice(g * LANES, (g + 1) * LANES)
        chains.append((q_ref[0, lanes, :], lanes, slice(g * MLA_V, (g + 1) * MLA_V)))
    for (_, _, rows), (acc, l) in zip(chains, _attend(
            chains, k_ref, vt_ref, s_ref, mx_ref, MLA_V, q_transposed=True, slab_interleave=True)):
        o_ref[0, rows, :] = (acc / l).astype(BF16)


def _mla_attention(q, k, vt, key_range):
    b, _, s = q.shape
    tq = min(MLA_Q_TILE, s)
    n_slabs = key_range[1] - key_range[0]
    return pl.pallas_call(
        _mla_kernel,
        grid=(b, MLA_HEADS // MLA_GROUP, s // tq),
        in_specs=[pl.BlockSpec((1, MLA_GROUP * LANES, tq), lambda bi, hg, i: (bi, hg, i))]
                 + _kv_specs(k, vt, MLA_GROUP * LANES, MLA_GROUP * MLA_V, lambda g: g[1], key_range),
        out_specs=pl.BlockSpec((1, MLA_GROUP * MLA_V, tq), lambda bi, hg, i: (bi, hg, i)),
        out_shape=jax.ShapeDtypeStruct((b, MLA_HEADS * MLA_V, s), BF16),
        scratch_shapes=_score_scratch(MLA_GROUP, n_slabs, tq),
        compiler_params=_params("parallel", "parallel", "parallel"),
        name="mla_attn",
    )(q, k, vt)


def _diff_kernel(lam_ref, q_ref, k_ref, vt_ref, gain_ref, o_ref, s_ref, mx_ref, *, out_scale):
    tq = q_ref.shape[1]
    first = lax.broadcasted_iota(jnp.int32, (tq, LANES), 1) < DIFF_QK
    zero = jnp.zeros((tq, LANES), BF16)
    chains = []
    for hd in range(DIFF_HEADS):
        lanes = slice(hd * LANES, (hd + 1) * LANES)
        q = q_ref[0, :, lanes]
        q_rhs = jnp.concatenate([jnp.where(first, q, zero), jnp.where(first, zero, q)], axis=0)
        chains.append((q_rhs, lanes, slice(hd * DIFF_V, (hd + 1) * DIFF_V)))
    for (_, _, rows), (acc, l) in zip(chains, _attend(
            chains, k_ref, vt_ref, s_ref, mx_ref, DIFF_V, q_transposed=False, slab_interleave=False)):
        o = acc / l
        o = o[:, :tq] - lam_ref[0] * o[:, tq:]
        o = o * lax.rsqrt(jnp.mean(o * o, axis=0, keepdims=True) + EPS) * gain_ref[...]
        o_ref[0, rows, :] = (o * out_scale).astype(BF16)


def _diff_attention(q, k, vt, key_range, lam, gain, lambda_init):
    b, s, w = q.shape
    tq = min(DIFF_Q_TILE, s)
    n_slabs = key_range[1] - key_range[0]
    return pl.pallas_call(
        functools.partial(_diff_kernel, out_scale=1.0 - lambda_init),
        grid=(b, s // tq),
        in_specs=[pl.BlockSpec(memory_space=pltpu.SMEM),
                  pl.BlockSpec((1, tq, w), lambda bi, i: (bi, i, 0))]
                 + _kv_specs(k, vt, w, DIFF_HEADS * DIFF_V, lambda g: 0, key_range)
                 + [pl.BlockSpec((DIFF_V, 1), lambda bi, i: (0, 0))],
        out_specs=pl.BlockSpec((1, DIFF_HEADS * DIFF_V, tq), lambda bi, i: (bi, 0, i)),
        out_shape=jax.ShapeDtypeStruct((b, DIFF_HEADS * DIFF_V, s), BF16),
        scratch_shapes=_score_scratch(DIFF_HEADS, n_slabs, 2 * tq),
        compiler_params=_params("parallel", "parallel"),
        name="diff_attn",
    )(lam.reshape(1), q, k, vt, gain.reshape(DIFF_V, 1))


def _ret_kernel(q_ref, k_ref, v_ref, s0_ref, intra_ref, qdec_ref, kdec_ref, cdec_ref, *rest, reverse, post):
    if post:
        of_ref, g_ref, gain_ref, o_ref, sout_ref, state = rest
    else:
        o_ref, sout_ref, state = rest
    i = pl.program_id(1)

    @pl.when(i == 0)
    def _():
        state[...] = s0_ref[0]

    n_chunks = q_ref.shape[1] // RET_CHUNK
    lane = lax.broadcasted_iota(jnp.int32, (RET_CHUNK, RET_HEADS * RET_K), 1)
    masks = [(lane >= hd * RET_K) & (lane < (hd + 1) * RET_K) for hd in range(RET_HEADS)]
    order = [n_chunks - 1 - cc if reverse else cc for cc in range(n_chunks)]
    heads = range(RET_HEADS)

    def head_v(c, hd):
        return v_ref[0, c * RET_CHUNK:(c + 1) * RET_CHUNK, hd * RET_V:(hd + 1) * RET_V]

    intra, deltas = {}, {}
    for c in order:
        rows = slice(c * RET_CHUNK, (c + 1) * RET_CHUNK)
        qc, kc = q_ref[0, rows, :], k_ref[0, rows, :]
        kdt = (kc.astype(F32) * kdec_ref[...]).T.astype(BF16)
        for hd in heads:
            a = lax.dot_general(jnp.where(masks[hd], qc, jnp.zeros_like(qc)), kc, _NT,
                                preferred_element_type=F32)
            intra[c, hd] = (a * intra_ref[hd]).astype(BF16)
            deltas[c, hd] = _dot(kdt[hd * RET_K:(hd + 1) * RET_K, :], head_v(c, hd))
    st = state[...]
    entering = {}
    for c in order:
        entering[c] = st.astype(BF16)
        st = st * cdec_ref[...] + jnp.concatenate([deltas[c, hd] for hd in heads], axis=0)
    state[...] = st
    for c in order:
        rows = slice(c * RET_CHUNK, (c + 1) * RET_CHUNK)
        qd = (q_ref[0, rows, :].astype(F32) * qdec_ref[...]).astype(BF16)
        outs = [_dot(intra[c, hd], head_v(c, hd))
                + _dot(jnp.where(masks[hd], qd, jnp.zeros_like(qd)), entering[c]) for hd in heads]
        if post:
            of = of_ref[0, rows, :]
            g = g_ref[0, rows, :].astype(F32)
            for hd in range(RET_HEADS):
                sl = slice(hd * RET_V, (hd + 1) * RET_V)
                outs[hd] = _rms(outs[hd] + of[:, sl], gain_ref[...]) * (g[:, sl] * jax.nn.sigmoid(g[:, sl]))
        o_ref[0, rows, :] = jnp.concatenate(outs, axis=1).astype(o_ref.dtype)

    @pl.when(i == pl.num_programs(1) - 1)
    def _():
        sout_ref[0] = state[...]


def _retention(q, k, v, s0, tabs, *, reverse, post=None):
    b, n, _ = q.shape
    t = min(TOKEN_TILE, n)
    nt = n // t
    hk, hv = RET_HEADS * RET_K, RET_HEADS * RET_V
    tile = (lambda bi, i: (bi, nt - 1 - i, 0)) if reverse else (lambda bi, i: (bi, i, 0))
    st_spec = pl.BlockSpec((1, hk, RET_V), lambda bi, i: (bi, 0, 0))
    in_specs = [pl.BlockSpec((1, t, hk), tile), pl.BlockSpec((1, t, hk), tile), pl.BlockSpec((1, t, hv), tile),
                st_spec] + [_const(a.shape) for a in tabs]
    args = [q, k, v, s0, *tabs]
    if post is not None:
        o_f, g, gain = post
        in_specs += [pl.BlockSpec((1, t, hv), tile), pl.BlockSpec((1, t, hv), tile), _const((1, RET_V))]
        args += [o_f, g, gain.reshape(1, RET_V)]
    return pl.pallas_call(
        functools.partial(_ret_kernel, reverse=reverse, post=post is not None),
        grid=(b, nt),
        in_specs=in_specs,
        out_specs=[pl.BlockSpec((1, t, hv), tile), st_spec],
        out_shape=[jax.ShapeDtypeStruct((b, n, hv), F32 if post is None else BF16),
                   jax.ShapeDtypeStruct((b, hk, RET_V), F32)],
        scratch_shapes=[pltpu.VMEM((hk, RET_V), F32)],
        compiler_params=_params("parallel", "arbitrary"),
        name="ret_bwd" if reverse else "ret_fwd",
    )(*args)


def _decay_tables(log_gamma, reverse):
    idx = jnp.arange(RET_CHUNK, dtype=F32)
    lg = log_gamma.astype(F32)[:, None]
    rel = idx[:, None] - idx[None, :]
    if reverse:
        rel = -rel
        q_pow, k_pow = RET_CHUNK - idx, idx
    else:
        q_pow, k_pow = idx + 1.0, RET_CHUNK - 1.0 - idx
    intra = jnp.where(rel >= 0, jnp.exp(lg[:, :, None] * jnp.maximum(rel, 0.0)), 0.0)
    qdec = jnp.repeat(jnp.exp(lg * q_pow).T, RET_K, axis=1)
    kdec = jnp.repeat(jnp.exp(lg * k_pow).T, RET_K, axis=1)
    cdec = jnp.broadcast_to(jnp.repeat(jnp.exp(lg * RET_CHUNK), RET_K, axis=0), (RET_HEADS * RET_K, RET_V))
    return intra, qdec, kdec, cdec


def _merge_kernel(x_ref, mod_ref, gain_ref, wg_ref, wb_ref, wo_ref, mla_ref, dif_ref, ret_ref, o_ref):
    x = x_ref[0]
    d = x.shape[1]
    h = _modulated(x, gain_ref[1:2, :], mod_ref[0, 3:4, :], mod_ref[0, 4:5, :]).astype(BF16)
    branches = (lax.dot_general(mla_ref[0], wb_ref[0], _TN, preferred_element_type=F32),
                lax.dot_general(dif_ref[0], wb_ref[1], _TN, preferred_element_type=F32),
                _dot(ret_ref[0], wb_ref[2]))
    y = None
    for j, br in enumerate(branches):
        t = jax.nn.sigmoid(_dot(h, wg_ref[:, j * d:(j + 1) * d])) * br
        y = t if y is None else y + t
    o_ref[0] = x + mod_ref[0, 5:6, :] * _dot(y.astype(BF16), wo_ref[...])


def _merge(x, mod, gain, wg, wb, wo, mla_t, dif_t, ret):
    b, s, d = x.shape
    tm = min(TOKEN_TILE, s)
    tok = lambda w: pl.BlockSpec((1, tm, w), lambda bi, i: (bi, i, 0))
    feat = pl.BlockSpec((1, BRANCH_W, tm), lambda bi, i: (bi, 0, i))
    return pl.pallas_call(
        _merge_kernel,
        grid=(b, s // tm),
        in_specs=[tok(d), pl.BlockSpec((1, N_MOD, d), lambda bi, i: (bi, 0, 0)), _const(gain.shape),
                  _const(wg.shape), _const(wb.shape), _const(wo.shape), feat, feat, tok(BRANCH_W)],
        out_specs=tok(d),
        out_shape=jax.ShapeDtypeStruct(x.shape, F32),
        compiler_params=_params("parallel", "parallel"),
        name="merge",
    )(x, mod, gain, wg, wb, wo, mla_t, dif_t, ret)


def _rot_cols(w, group):
    r, n = w.shape
    w = w.reshape(r, n // group, 2, group // 2)
    return jnp.stack([-w[:, :, 1], w[:, :, 0]], axis=2).reshape(r, n)


def _head_slots(w, n_heads, width):
    r = w.shape[0]
    w = w.reshape(r, n_heads, width)
    return jnp.pad(w, ((0, 0), (0, 0), (0, LANES - width))).reshape(r, n_heads * LANES)


def _proj_weights(w_in, w_qb, w_kvb, q_norm, kv_norm):
    sizes = (MLA_Q_RANK, MLA_KV_RANK, MLA_ROPE, 2 * DIFF_HEADS * DIFF_QK, 2 * DIFF_HEADS * DIFF_QK,
             DIFF_HEADS * DIFF_V, RET_HEADS * RET_K, RET_HEADS * RET_K, RET_HEADS * RET_V, RET_HEADS * RET_V)
    offs = [0]
    for sz in sizes:
        offs.append(offs[-1] + sz)
    p = [w_in[:, offs[i]:offs[i + 1]] for i in range(len(sizes))]
    gates = w_in[:, offs[-1]:]
    d = w_in.shape[0]
    lead = jnp.zeros((d, MLA_NOPE), F32)
    tail = jnp.zeros((d, LANES - MLA_NOPE - MLA_ROPE), F32)
    main = jnp.concatenate([
        p[0], p[1],
        lead, p[2], tail, lead, _rot_cols(p[2], MLA_ROPE // 2), tail,
        p[3], p[4], p[6], p[7], p[8], p[9]], axis=1)
    assert main.shape[1] == _W_MAIN
    hd = MLA_NOPE + MLA_ROPE
    qb = w_qb.reshape(MLA_Q_RANK, MLA_HEADS, hd)
    q_rot = _rot_cols(qb[:, :, MLA_NOPE:].reshape(MLA_Q_RANK, MLA_HEADS * MLA_ROPE), MLA_ROPE // 2)
    q_rot = jnp.concatenate([jnp.zeros((MLA_Q_RANK, MLA_HEADS, MLA_NOPE), F32),
                             q_rot.reshape(MLA_Q_RANK, MLA_HEADS, MLA_ROPE)], axis=2)
    kvb = w_kvb.reshape(MLA_KV_RANK, MLA_HEADS, MLA_NOPE + MLA_V)
    return {
        "main": main.astype(BF16),
        "qa": _head_slots(w_qb, MLA_HEADS, hd).T.astype(BF16),
        "qb": _head_slots(q_rot.reshape(MLA_Q_RANK, MLA_HEADS * hd), MLA_HEADS, hd).T.astype(BF16),
        "ka": _head_slots(kvb[:, :, :MLA_NOPE].reshape(MLA_KV_RANK, -1), MLA_HEADS, MLA_NOPE).astype(BF16),
        "vt": kvb[:, :, MLA_NOPE:].reshape(MLA_KV_RANK, -1).T.astype(BF16),
        "dvt": p[5].T.astype(BF16),
        "qn": q_norm.reshape(1, -1),
        "kvn": kv_norm.reshape(1, -1),
    }, gates.astype(BF16)


def _angles(pos, dim):
    inv = ROPE_BASE ** (-jnp.arange(0, dim, 2, dtype=F32) / dim)
    ang = pos.astype(F32)[:, None] * inv[None, :]
    return jnp.cos(ang), jnp.sin(ang)


def _rope_tables(n):
    t = jnp.arange(n)
    row = jnp.repeat(jnp.arange(n // GRID_W), GRID_W)
    col = t - row * GRID_W

    def axial(dim):
        cr, sr = _angles(row, dim // 2)
        cc, sc = _angles(col, dim // 2)
        return jnp.concatenate([cr, cr, cc, cc], axis=1), jnp.concatenate([sr, sr, sc, sc], axis=1)

    cm, sm = axial(MLA_ROPE)
    ones = jnp.ones((n, MLA_NOPE), F32)
    pad = jnp.zeros((n, LANES - MLA_NOPE - MLA_ROPE), F32)
    cm = jnp.concatenate([ones, cm, pad], axis=1)
    sm = jnp.concatenate([0 * ones, sm, pad], axis=1)
    cd, sd = axial(DIFF_QK)
    cs, ss = _angles(t, RET_K)
    sd = sd * jnp.repeat(jnp.array([-1.0, 1.0, -1.0, 1.0], F32), DIFF_QK // 4)
    return (cm, sm, jnp.tile(cd, (1, 2)), jnp.tile(sd, (1, 2)),
            jnp.tile(cs, (1, 4)), jnp.tile(jnp.concatenate([-ss, ss], axis=1), (1, 2)),
            cm.T, sm.T)


def _identity_tables(n):
    one, zero = jnp.ones((n, LANES), F32), jnp.zeros((n, LANES), F32)
    return one, zero, one, zero, one, zero, one.T, zero.T


def _chunked(w, axis):
    if axis == 1:
        r = w.shape[0]
        return w.reshape(r, D_FF // FF_CHUNK, FF_CHUNK).transpose(1, 0, 2).astype(BF16)
    return w.reshape(D_FF // FF_CHUNK, FF_CHUNK, w.shape[1]).astype(BF16)


def kernel(x, c, ctx, c_ctx, ada_w, ada_b, norm_gain, ffn1_w1, ffn1_w3, ffn1_w2, ffn2_w1, ffn2_w3, ffn2_w2,
           w_in, mla_q_norm, mla_w_qb, mla_kv_norm, mla_w_kvb, diff_lambda, diff_norm, ret_decay, ret_norm,
           w_branch, w_out, final_norm):
    b, n, d = x.shape
    n_ctx = ctx.shape[1]
    pad_rows = -(b + 1) % 8
    cs = jnp.concatenate([c, c_ctx[None, :], jnp.zeros((pad_rows, d), F32)], axis=0)
    mods = _ada(cs, ada_w, ada_b)
    x_tabs, c_tabs = _rope_tables(n), _identity_tables(n_ctx)
    zero_state = jnp.zeros((b, RET_HEADS * RET_K, RET_V), F32)
    n_x, n_all = n // KEY_CHUNK, (n + n_ctx) // KEY_CHUNK
    xc = ctx
    for l in range(DEPTH):
        lambda_init = 0.8 - 0.6 * math.exp(-0.3 * l)
        ctx_out = l < DEPTH - 1
        mx = mods[l, :b].reshape(b, N_MOD, d)
        mc = jnp.broadcast_to(mods[l, b].reshape(1, N_MOD, d), (b, N_MOD, d))
        gain = norm_gain[l]
        f1 = (_chunked(ffn1_w1[l], 1), _chunked(ffn1_w3[l], 1), _chunked(ffn1_w2[l], 0))
        f2 = (_chunked(ffn2_w1[l], 1), _chunked(ffn2_w3[l], 1), _chunked(ffn2_w2[l], 0))
        pw, wg = _proj_weights(w_in[l], mla_w_qb[l], mla_w_kvb[l], mla_q_norm[l], mla_kv_norm[l])
        wb, wo = w_branch[l].astype(BF16), w_out[l].astype(BF16)
        dl = diff_lambda[l].astype(F32)
        lam = jnp.exp(jnp.sum(dl[0] * dl[1])) - jnp.exp(jnp.sum(dl[2] * dl[3])) + lambda_init
        log_gamma = -jnp.exp(ret_decay[l].astype(F32))
        tab_f, tab_b = _decay_tables(log_gamma[0], False), _decay_tables(log_gamma[1], True)

        x = _ffn(x, mx, gain, *f1, row0=0, gain_row=0)
        xc = _ffn(xc, mc, gain, *f1, row0=0, gain_row=0)

        mq_t, mk, mvt, dq, dk, dvt, rq, rk, rv, rg = _proj(x, mx, gain, pw, x_tabs, n_keys=n + n_ctx)
        cmq_t, mk, mvt, cdq, dk, dvt, crq, crk, crv, crg = _proj(
            xc, mc, gain, pw, c_tabs, n_keys=n + n_ctx, key_off=n_x, into=(mk, mvt, dk, dvt))

        mla_t = _mla_attention(mq_t, mk, mvt, (0, n_all))
        dif_t = _diff_attention(dq, dk, dvt, (0, n_all), lam, diff_norm[l], lambda_init)

        oc_f, st_f = _retention(crq, crk, crv, zero_state, tab_f, reverse=False)
        ret_c, st_b = _retention(crq, crk, crv, zero_state, tab_b, reverse=True,
                                 post=(oc_f, crg, ret_norm[l]) if ctx_out else None)
        ox_f, _ = _retention(rq, rk, rv, st_f, tab_f, reverse=False)
        ret_x, _ = _retention(rq, rk, rv, st_b, tab_b, reverse=True, post=(ox_f, rg, ret_norm[l]))

        x = _merge(x, mx, gain, wg, wb, wo, mla_t, dif_t, ret_x)
        x = _ffn(x, mx, gain, *f2, row0=6, gain_row=2, final_gain=None if ctx_out else final_norm)
        if ctx_out:
            mla_c = _mla_attention(cmq_t, mk, mvt, (n_x, n_all))
            dif_c = _diff_attention(cdq, dk, dvt, (n_x, n_all), lam, diff_norm[l], lambda_init)
            xc = _merge(xc, mc, gain, wg, wb, wo, mla_c, dif_c, ret_c)
            xc = _ffn(xc, mc, gain, *f2, row0=6, gain_row=2)
    return x
```

```python
import functools
import math

import jax
import jax.numpy as jnp
from jax import lax
from jax.experimental import pallas as pl
from jax.experimental.pallas import tpu as pltpu

D_MODEL = 1024
DEPTH = 2
GRID_W = 64
N_BRANCH = 3
MLA_HEADS = 8
MLA_Q_RANK = 384
MLA_KV_RANK = 256
MLA_NOPE = 64
MLA_ROPE = 32
MLA_V = 64
DIFF_HEADS = 4
DIFF_QK = 64
DIFF_V = 128
RET_HEADS = 4
RET_K = 64
RET_V = 128
RET_CHUNK = 128
BRANCH_W = 512
D_FF = 2816
N_MOD = 9
ROPE_BASE = 10000.0
EPS = 1e-6

LANES = 128
SUBLANES = 8
LOG2E = 1.4426950408889634
FF_CHUNK = 256
KEY_CHUNK = 256
TOKEN_TILE = 512
MLA_Q_TILE = 256
DIFF_Q_TILE = 256
VMEM_LIMIT = 56 * 1024 * 1024

BF16 = jnp.bfloat16
F32 = jnp.float32
_NT = (((1,), (1,)), ((), ()))
_TN = (((0,), (0,)), ((), ()))


def _params(*sem):
    return pltpu.CompilerParams(dimension_semantics=sem, vmem_limit_bytes=VMEM_LIMIT)


def _const(shape):
    zeros = (0,) * len(shape)
    return pl.BlockSpec(shape, lambda *_: zeros, pipeline_mode=pl.Buffered(1))


def _rms(x, gain):
    return x * lax.rsqrt(jnp.mean(x * x, axis=-1, keepdims=True) + EPS) * gain


def _modulated(x, gain, shift, scale):
    return _rms(x, gain) * (1.0 + scale) + shift


def _dot(a, b):
    return jnp.dot(a, b, preferred_element_type=F32)


def _ada_kernel(c_ref, w_ref, b_ref, o_ref):
    c = c_ref[...]
    o_ref[0] = _dot(c * jax.nn.sigmoid(c), w_ref[0]) + b_ref[0]


def _ada(cs, ada_w, ada_b):
    nl, d, n = ada_w.shape
    r = cs.shape[0]
    tn = n // N_MOD
    return pl.pallas_call(
        _ada_kernel,
        grid=(nl, n // tn),
        in_specs=[pl.BlockSpec((r, d), lambda l, j: (0, 0)),
                  pl.BlockSpec((1, d, tn), lambda l, j: (l, 0, j)),
                  pl.BlockSpec((1, 1, tn), lambda l, j: (l, 0, j))],
        out_specs=pl.BlockSpec((1, r, tn), lambda l, j: (l, 0, j)),
        out_shape=jax.ShapeDtypeStruct((nl, r, n), F32),
        compiler_params=_params("parallel", "parallel"),
        name="ada",
    )(cs, ada_w, ada_b.reshape(nl, 1, n))


def _ffn_kernel(x_ref, mod_ref, gain_ref, w1_ref, w3_ref, w2_ref, *rest, row0, gain_row, final):
    if final:
        fg_ref, o_ref, acc_ref = rest
    else:
        o_ref, acc_ref = rest
    x = x_ref[0]
    h = _modulated(x, gain_ref[gain_row:gain_row + 1, :], mod_ref[0, row0:row0 + 1, :],
                   mod_ref[0, row0 + 1:row0 + 2, :]).astype(BF16)
    n_chunks = w1_ref.shape[0]
    up = (_dot(h, w1_ref[0]), _dot(h, w3_ref[0]))
    for c in range(n_chunks):
        a, b = up
        if c + 1 < n_chunks:
            up = (_dot(h, w1_ref[c + 1]), _dot(h, w3_ref[c + 1]))
        t = _dot((a * jax.nn.sigmoid(a) * b).astype(BF16), w2_ref[c])
        if c == 0:
            acc_ref[...] = t
        else:
            acc_ref[...] += t
    y = x + (mod_ref[0, row0 + 2:row0 + 3, :] * 0.5) * acc_ref[...]
    if final:
        y = _rms(y, fg_ref[...])
    o_ref[0] = y


def _ffn(x, mod, gain, w1, w3, w2, *, row0, gain_row, final_gain=None):
    b, s, d = x.shape
    tm = min(TOKEN_TILE, s)
    final = final_gain is not None
    in_specs = [pl.BlockSpec((1, tm, d), lambda bi, i: (bi, i, 0)),
                pl.BlockSpec((1, N_MOD, d), lambda bi, i: (bi, 0, 0)),
                _const(gain.shape), _const(w1.shape), _const(w3.shape), _const(w2.shape)]
    args = [x, mod, gain, w1, w3, w2]
    if final:
        in_specs.append(_const((1, d)))
        args.append(final_gain.reshape(1, d))
    return pl.pallas_call(
        functools.partial(_ffn_kernel, row0=row0, gain_row=gain_row, final=final),
        grid=(b, s // tm),
        in_specs=in_specs,
        out_specs=pl.BlockSpec((1, tm, d), lambda bi, i: (bi, i, 0)),
        out_shape=jax.ShapeDtypeStruct(x.shape, F32),
        scratch_shapes=[pltpu.VMEM((tm, d), F32)],
        compiler_params=_params("parallel", "parallel"),
        name="ffn",
    )(*args)


_O_QLAT = 0
_O_KVLAT = _O_QLAT + MLA_Q_RANK
_O_KR = _O_KVLAT + MLA_KV_RANK
_O_DQ = _O_KR + 2 * LANES
_O_RQ = _O_DQ + 2 * BRANCH_W
_O_RV = _O_RQ + 2 * RET_HEADS * RET_K
_W_MAIN = _O_RV + 2 * BRANCH_W


def _rotary(x, cos, sin_signed, half):
    lane = lax.broadcasted_iota(jnp.int32, x.shape, 1)
    first = (lane & (2 * half - 1)) < half
    partner = jnp.where(first, pltpu.roll(x, LANES - half, 1), pltpu.roll(x, half, 1))
    return x * cos + partner * sin_signed


_KEY_OUTPUTS = (1, 2, 4, 5)


def _proj_kernel(x_ref, mod_ref, gain_ref, wm_ref, wqa_ref, wqb_ref, wka_ref, wvt_ref, wdvt_ref,
                 qn_ref, kvn_ref, cm_ref, sm_ref, cd_ref, sd_ref, cr_ref, sr_ref, cmt_ref, smt_ref, *rest):
    mqt_ref, mk_ref, mvt_ref, dq_ref, dk_ref, dvt_ref, rq_ref, rk_ref, rv_ref, rg_ref = rest[-10:]
    x = x_ref[0]
    h = _modulated(x, gain_ref[1:2, :], mod_ref[0, 3:4, :], mod_ref[0, 4:5, :]).astype(BF16)

    def seg(lo, width):
        return _dot(h, wm_ref[:, lo:lo + width])

    qlat = _rms(seg(_O_QLAT, MLA_Q_RANK), qn_ref[...]).astype(BF16)
    kvlat = _rms(seg(_O_KVLAT, MLA_KV_RANK), kvn_ref[...]).astype(BF16)
    cm, sm = cm_ref[...], sm_ref[...]
    cmt, smt = cmt_ref[...], smt_ref[...]
    qat = lax.dot_general(wqa_ref[...], qlat, _NT, preferred_element_type=F32)
    qbt = lax.dot_general(wqb_ref[...], qlat, _NT, preferred_element_type=F32)
    ka = _dot(kvlat, wka_ref[...])
    kr = seg(_O_KR, 2 * LANES)
    krr = kr[:, :LANES] * cm + kr[:, LANES:] * sm
    q_scale = (MLA_NOPE + MLA_ROPE) ** -0.5 * LOG2E
    for hd in range(MLA_HEADS):
        sl = slice(hd * LANES, (hd + 1) * LANES)
        mqt_ref[0, sl, :] = ((qat[sl, :] * cmt + qbt[sl, :] * smt) * q_scale).astype(BF16)
        mk_ref[0, :, sl] = (ka[:, sl] + krr).astype(BF16)
    mvt_ref[0, 0] = lax.dot_general(wvt_ref[...], kvlat, _NT, preferred_element_type=F32).astype(BF16)

    cd, sd = cd_ref[...], sd_ref[...]
    dqk = seg(_O_DQ, 2 * BRANCH_W)
    d_scale = DIFF_QK ** -0.5 * LOG2E
    for j in range(BRANCH_W // LANES):
        sl = slice(j * LANES, (j + 1) * LANES)
        sk = slice(BRANCH_W + j * LANES, BRANCH_W + (j + 1) * LANES)
        dq_ref[0, :, sl] = (_rotary(dqk[:, sl], cd, sd, DIFF_QK // 4) * d_scale).astype(BF16)
        dk_ref[0, :, sl] = _rotary(dqk[:, sk], cd, sd, DIFF_QK // 4).astype(BF16)
    dvt_ref[0, 0] = lax.dot_general(wdvt_ref[...], h, _NT, preferred_element_type=F32).astype(BF16)

    cr, sr_t = cr_ref[...], sr_ref[...]
    hk = RET_HEADS * RET_K
    rqk = seg(_O_RQ, 2 * hk)
    for j in range(hk // LANES):
        sl = slice(j * LANES, (j + 1) * LANES)
        sk = slice(hk + j * LANES, hk + (j + 1) * LANES)
        rq_ref[0, :, sl] = _rotary(rqk[:, sl], cr, sr_t, RET_K // 2).astype(BF16)
        rk_ref[0, :, sl] = (_rotary(rqk[:, sk], cr, sr_t, RET_K // 2) * RET_K ** -0.5).astype(BF16)
    rv_ref[0] = seg(_O_RV, BRANCH_W).astype(BF16)
    rg_ref[0] = seg(_O_RV + BRANCH_W, BRANCH_W).astype(BF16)


def _proj(x, mod, gain, pw, tables, *, n_keys, key_off=0, into=None):
    b, s, d = x.shape
    tm = KEY_CHUNK
    nt = s // tm
    tok = lambda w, off=0: pl.BlockSpec((1, tm, w), lambda bi, i: (bi, i + off, 0))
    slab = pl.BlockSpec((1, 1, BRANCH_W, tm), lambda bi, i: (bi, i + key_off, 0, 0))
    tab = pl.BlockSpec((tm, LANES), lambda bi, i: (i, 0))
    tab_t = pl.BlockSpec((LANES, tm), lambda bi, i: (0, i))
    weights = [pw["main"], pw["qa"], pw["qb"], pw["ka"], pw["vt"], pw["dvt"], pw["qn"], pw["kvn"]]
    hk = RET_HEADS * RET_K
    widths = [MLA_HEADS * LANES, MLA_HEADS * LANES, None, BRANCH_W, BRANCH_W, None, hk, hk, BRANCH_W, BRANCH_W]
    out_specs, out_shape = [], []
    for pos, w in enumerate(widths):
        if pos == 0:
            out_specs.append(pl.BlockSpec((1, w, tm), lambda bi, i: (bi, 0, i)))
            out_shape.append(jax.ShapeDtypeStruct((b, w, s), BF16))
        elif w is None:
            out_specs.append(slab)
            out_shape.append(jax.ShapeDtypeStruct((b, n_keys // tm, BRANCH_W, tm), BF16))
        elif pos in _KEY_OUTPUTS:
            out_specs.append(tok(w, key_off))
            out_shape.append(jax.ShapeDtypeStruct((b, n_keys, w), BF16))
        else:
            out_specs.append(tok(w))
            out_shape.append(jax.ShapeDtypeStruct((b, s, w), BF16))
    in_specs = [pl.BlockSpec((1, tm, d), lambda bi, i: (bi, i, 0)),
                pl.BlockSpec((1, N_MOD, d), lambda bi, i: (bi, 0, 0)),
                _const(gain.shape)] + [_const(w.shape) for w in weights] + [tab] * 6 + [tab_t] * 2
    args = [x, mod, gain, *weights, *tables]
    aliases = {}
    if into is not None:
        for buf, pos in zip(into, _KEY_OUTPUTS):
            aliases[len(args)] = pos
            in_specs.append(pl.BlockSpec(memory_space=pl.ANY))
            args.append(buf)
    return pl.pallas_call(
        _proj_kernel,
        grid=(b, nt),
        in_specs=in_specs,
        out_specs=out_specs,
        out_shape=out_shape,
        input_output_aliases=aliases,
        compiler_params=_params("parallel", "parallel"),
        name="proj",
    )(*args)


SUM_ROWS = 16


def _slabs_per_chunk(n_slabs):
    return next(p for p in (3, 2, 1) if n_slabs % p == 0)


def _attend(tiles, upcoming, first, k_ref, vt_ref, s_ref, mx_ref, dv, *, q_transposed):
    n = tiles[0][0][0].shape[1 if q_transposed else 0]
    n_slabs = vt_ref.shape[1]
    per = _slabs_per_chunk(n_slabs)
    tk = per * KEY_CHUNK
    n_chunks = n_slabs // per
    ones = jnp.ones((SUM_ROWS, KEY_CHUNK), BF16)

    def score(chains, c, buf, g):
        q_rhs, lanes, _ = chains[g]
        start = c * tk if isinstance(c, int) else pl.multiple_of(c * tk, KEY_CHUNK)
        k = k_ref[0, pl.ds(start, tk), lanes]
        s = _dot(k, q_rhs) if q_transposed else lax.dot_general(k, q_rhs, _NT, preferred_element_type=F32)
        s_ref[buf, g] = s
        for j in range(per):
            part = s[j * KEY_CHUNK:(j + 1) * KEY_CHUNK]
            mx_ref[buf, g, j] = jnp.max(part.reshape(KEY_CHUNK // SUBLANES, SUBLANES, n), axis=0)

    def consume(chains, c, buf, carry, following):
        out = []
        for g, ((_, _, rows), (m, acc)) in enumerate(zip(chains, carry)):
            if following is not None:
                score(following[0], following[1], 1 - buf, g)
            mx = mx_ref[buf, g, 0]
            for j in range(1, per):
                mx = jnp.maximum(mx, mx_ref[buf, g, j])
            m_new = jnp.maximum(m, jnp.max(mx, axis=0, keepdims=True))
            alpha = jnp.exp2(m - m_new)
            pb = jnp.exp2(s_ref[buf, g] - m_new).astype(BF16)
            pv = None
            for j in range(per):
                vt = jnp.concatenate([vt_ref[0, c * per + j, rows, :], ones], axis=0)
                t = _dot(vt, pb[j * KEY_CHUNK:(j + 1) * KEY_CHUNK, :])
                pv = t if pv is None else pv + t
            out.append((m_new, alpha * acc + pv))
        return tuple(out)

    def prologue():
        for g in range(len(tiles[0])):
            score(tiles[0], 0, 0, g)

    if first is None:
        prologue()
    else:
        pl.when(first)(prologue)

    results = []
    start = 0
    for t, chains in enumerate(tiles):
        following = tiles[t + 1] if t + 1 < len(tiles) else upcoming

        def step(c, buf, carry, chains=chains):
            return consume(chains, c, buf, carry, (chains, c + 1))

        carry = tuple((jnp.full((1, n), -jnp.inf, F32), jnp.zeros((dv + SUM_ROWS, n), F32)) for _ in chains)
        n_pairs = (n_chunks - 1) // 2
        if n_pairs > 0:
            carry = lax.fori_loop(
                0, n_pairs, lambda j, cr, s0=start, step=step: step(2 * j + 1, 1 - s0, step(2 * j, s0, cr)), carry)
        if (n_chunks - 1) % 2:
            carry = step(n_chunks - 2, (start + n_chunks - 2) % 2, carry)
        last = (start + n_chunks - 1) % 2
        carry = consume(chains, n_chunks - 1, last, carry, None if following is None else (following, 0))
        results.append([(acc[:dv], acc[dv:dv + 1]) for _, acc in carry])
        start = 1 - last
    assert upcoming is None or start == 0
    return results


def _tiles_per_step(n_tiles, n_slabs):
    n_chunks = n_slabs // _slabs_per_chunk(n_slabs)
    if n_tiles % 2 == 0:
        return 2
    assert n_tiles == 1 or n_chunks % 2 == 0
    return 1


def _kv_specs(k_width, v_rows, index, key_range):
    lo, hi = key_range
    n_slabs = hi - lo
    assert lo % n_slabs == 0
    blk = lo // n_slabs
    return [pl.BlockSpec((1, n_slabs * KEY_CHUNK, k_width), lambda *g: (g[0], blk, index(g)),
                         pipeline_mode=pl.Buffered(1)),
            pl.BlockSpec((1, n_slabs, v_rows, KEY_CHUNK), lambda *g: (g[0], blk, index(g), 0),
                         pipeline_mode=pl.Buffered(1))]


def _score_scratch(n_chains, n_slabs, n):
    per = _slabs_per_chunk(n_slabs)
    return [pltpu.VMEM((2, n_chains, per * KEY_CHUNK, n), F32),
            pltpu.VMEM((2, n_chains, per, SUBLANES, n), F32)]


def _mla_kernel(q_ref, qn_ref, k_ref, vt_ref, o_ref, s_ref, mx_ref, *, tq, chained):
    def chains_of(ref, t):
        return [(ref[0, g * LANES:(g + 1) * LANES, t * tq:(t + 1) * tq], slice(g * LANES, (g + 1) * LANES),
                 slice(g * MLA_V, (g + 1) * MLA_V)) for g in range(MLA_HEADS)]

    tiles = [chains_of(q_ref, t) for t in range(q_ref.shape[2] // tq)]
    res = _attend(tiles, chains_of(qn_ref, 0) if chained else None,
                  pl.program_id(1) == 0 if chained else None,
                  k_ref, vt_ref, s_ref, mx_ref, MLA_V, q_transposed=True)
    for t, (chains, out) in enumerate(zip(tiles, res)):
        for (_, _, rows), (acc, l) in zip(chains, out):
            o_ref[0, rows, t * tq:(t + 1) * tq] = (acc / l).astype(BF16)


def _mla_attention(q, k, vt, key_range):
    b, w, s = q.shape
    tq = min(MLA_Q_TILE, s)
    n_tiles = s // tq
    n_slabs = key_range[1] - key_range[0]
    per_step = _tiles_per_step(n_tiles, n_slabs)
    n_steps = n_tiles // per_step
    return pl.pallas_call(
        functools.partial(_mla_kernel, tq=tq, chained=n_steps > 1),
        grid=(b, n_steps),
        in_specs=[pl.BlockSpec((1, w, per_step * tq), lambda bi, i: (bi, 0, i)),
                  pl.BlockSpec((1, w, tq), lambda bi, i: (bi, 0, jnp.minimum((i + 1) * per_step, n_tiles - 1)))]
                 + _kv_specs(w, MLA_HEADS * MLA_V, lambda g: 0, key_range),
        out_specs=pl.BlockSpec((1, MLA_HEADS * MLA_V, per_step * tq), lambda bi, i: (bi, 0, i)),
        out_shape=jax.ShapeDtypeStruct((b, MLA_HEADS * MLA_V, s), BF16),
        scratch_shapes=_score_scratch(MLA_HEADS, n_slabs, tq),
        compiler_params=_params("parallel", "arbitrary"),
        name="mla_attn",
    )(q, q, k, vt)


def _diff_kernel(lam_ref, q_ref, qn_ref, k_ref, vt_ref, gain_ref, o_ref, s_ref, mx_ref, *, tq, chained, out_scale):
    first_half = lax.broadcasted_iota(jnp.int32, (tq, LANES), 1) < DIFF_QK
    zero = jnp.zeros((tq, LANES), BF16)

    def chains_of(ref, t):
        chains = []
        for hd in range(DIFF_HEADS):
            lanes = slice(hd * LANES, (hd + 1) * LANES)
            q = ref[0, t * tq:(t + 1) * tq, lanes]
            q_rhs = jnp.concatenate([jnp.where(first_half, q, zero), jnp.where(first_half, zero, q)], axis=0)
            chains.append((q_rhs, lanes, slice(hd * DIFF_V, (hd + 1) * DIFF_V)))
        return chains

    tiles = [chains_of(q_ref, t) for t in range(q_ref.shape[1] // tq)]
    res = _attend(tiles, chains_of(qn_ref, 0) if chained else None,
                  pl.program_id(1) == 0 if chained else None,
                  k_ref, vt_ref, s_ref, mx_ref, DIFF_V, q_transposed=False)
    for t, (chains, out) in enumerate(zip(tiles, res)):
        for (_, _, rows), (acc, l) in zip(chains, out):
            o = acc / l
            o = o[:, :tq] - lam_ref[0] * o[:, tq:]
            o = o * lax.rsqrt(jnp.mean(o * o, axis=0, keepdims=True) + EPS) * gain_ref[...]
            o_ref[0, rows, t * tq:(t + 1) * tq] = (o * out_scale).astype(BF16)


def _diff_attention(q, k, vt, key_range, lam, gain, lambda_init):
    b, s, w = q.shape
    tq = min(DIFF_Q_TILE, s)
    n_tiles = s // tq
    n_slabs = key_range[1] - key_range[0]
    per_step = _tiles_per_step(n_tiles, n_slabs)
    n_steps = n_tiles // per_step
    return pl.pallas_call(
        functools.partial(_diff_kernel, tq=tq, chained=n_steps > 1, out_scale=1.0 - lambda_init),
        grid=(b, n_steps),
        in_specs=[pl.BlockSpec(memory_space=pltpu.SMEM),
                  pl.BlockSpec((1, per_step * tq, w), lambda bi, i: (bi, i, 0)),
                  pl.BlockSpec((1, tq, w), lambda bi, i: (bi, jnp.minimum((i + 1) * per_step, n_tiles - 1), 0))]
                 + _kv_specs(w, DIFF_HEADS * DIFF_V, lambda g: 0, key_range)
                 + [pl.BlockSpec((DIFF_V, 1), lambda bi, i: (0, 0))],
        out_specs=pl.BlockSpec((1, DIFF_HEADS * DIFF_V, per_step * tq), lambda bi, i: (bi, 0, i)),
        out_shape=jax.ShapeDtypeStruct((b, DIFF_HEADS * DIFF_V, s), BF16),
        scratch_shapes=_score_scratch(DIFF_HEADS, n_slabs, 2 * tq),
        compiler_params=_params("parallel", "arbitrary"),
        name="diff_attn",
    )(lam.reshape(1), q, q, k, vt, gain.reshape(DIFF_V, 1))


def _ret_kernel(q_ref, k_ref, v_ref, s0_ref, intra_ref, qdec_ref, kdec_ref, cdec_ref, *rest, reverse, post):
    if post:
        of_ref, g_ref, gain_ref, o_ref, sout_ref, state = rest
    else:
        o_ref, sout_ref, state = rest
    i = pl.program_id(1)

    @pl.when(i == 0)
    def _():
        state[...] = s0_ref[0]

    n_chunks = q_ref.shape[1] // RET_CHUNK
    lane = lax.broadcasted_iota(jnp.int32, (RET_CHUNK, RET_HEADS * RET_K), 1)
    masks = [(lane >= hd * RET_K) & (lane < (hd + 1) * RET_K) for hd in range(RET_HEADS)]
    order = [n_chunks - 1 - cc if reverse else cc for cc in range(n_chunks)]
    heads = range(RET_HEADS)

    def head_v(c, hd):
        return v_ref[0, c * RET_CHUNK:(c + 1) * RET_CHUNK, hd * RET_V:(hd + 1) * RET_V]

    intra, deltas = {}, {}
    for c in order:
        rows = slice(c * RET_CHUNK, (c + 1) * RET_CHUNK)
        qc, kc = q_ref[0, rows, :], k_ref[0, rows, :]
        kdt = (kc.astype(F32) * kdec_ref[...]).T.astype(BF16)
        for hd in heads:
            a = lax.dot_general(jnp.where(masks[hd], qc, jnp.zeros_like(qc)), kc, _NT,
                                preferred_element_type=F32)
            intra[c, hd] = (a * intra_ref[hd]).astype(BF16)
            deltas[c, hd] = _dot(kdt[hd * RET_K:(hd + 1) * RET_K, :], head_v(c, hd))
    st = state[...]
    entering = {}
    for c in order:
        entering[c] = st.astype(BF16)
        st = st * cdec_ref[...] + jnp.concatenate([deltas[c, hd] for hd in heads], axis=0)
    state[...] = st
    for c in order:
        rows = slice(c * RET_CHUNK, (c + 1) * RET_CHUNK)
        qd = (q_ref[0, rows, :].astype(F32) * qdec_ref[...]).astype(BF16)
        outs = [_dot(intra[c, hd], head_v(c, hd))
                + _dot(jnp.where(masks[hd], qd, jnp.zeros_like(qd)), entering[c]) for hd in heads]
        if post:
            of = of_ref[0, rows, :]
            g = g_ref[0, rows, :].astype(F32)
            for hd in range(RET_HEADS):
                sl = slice(hd * RET_V, (hd + 1) * RET_V)
                outs[hd] = _rms(outs[hd] + of[:, sl], gain_ref[...]) * (g[:, sl] * jax.nn.sigmoid(g[:, sl]))
        o_ref[0, rows, :] = jnp.concatenate(outs, axis=1).astype(o_ref.dtype)

    @pl.when(i == pl.num_programs(1) - 1)
    def _():
        sout_ref[0] = state[...]


def _retention(q, k, v, s0, tabs, *, reverse, post=None):
    b, n, _ = q.shape
    t = min(TOKEN_TILE, n)
    nt = n // t
    hk, hv = RET_HEADS * RET_K, RET_HEADS * RET_V
    tile = (lambda bi, i: (bi, nt - 1 - i, 0)) if reverse else (lambda bi, i: (bi, i, 0))
    st_spec = pl.BlockSpec((1, hk, RET_V), lambda bi, i: (bi, 0, 0))
    in_specs = [pl.BlockSpec((1, t, hk), tile), pl.BlockSpec((1, t, hk), tile), pl.BlockSpec((1, t, hv), tile),
                st_spec] + [_const(a.shape) for a in tabs]
    args = [q, k, v, s0, *tabs]
    if post is not None:
        o_f, g, gain = post
        in_specs += [pl.BlockSpec((1, t, hv), tile), pl.BlockSpec((1, t, hv), tile), _const((1, RET_V))]
        args += [o_f, g, gain.reshape(1, RET_V)]
    return pl.pallas_call(
        functools.partial(_ret_kernel, reverse=reverse, post=post is not None),
        grid=(b, nt),
        in_specs=in_specs,
        out_specs=[pl.BlockSpec((1, t, hv), tile), st_spec],
        out_shape=[jax.ShapeDtypeStruct((b, n, hv), F32 if post is None else BF16),
                   jax.ShapeDtypeStruct((b, hk, RET_V), F32)],
        scratch_shapes=[pltpu.VMEM((hk, RET_V), F32)],
        compiler_params=_params("parallel", "arbitrary"),
        name="ret_bwd" if reverse else "ret_fwd",
    )(*args)


def _decay_tables(log_gamma, reverse):
    idx = jnp.arange(RET_CHUNK, dtype=F32)
    lg = log_gamma.astype(F32)[:, None]
    rel = idx[:, None] - idx[None, :]
    if reverse:
        rel = -rel
        q_pow, k_pow = RET_CHUNK - idx, idx
    else:
        q_pow, k_pow = idx + 1.0, RET_CHUNK - 1.0 - idx
    intra = jnp.where(rel >= 0, jnp.exp(lg[:, :, None] * jnp.maximum(rel, 0.0)), 0.0)
    qdec = jnp.repeat(jnp.exp(lg * q_pow).T, RET_K, axis=1)
    kdec = jnp.repeat(jnp.exp(lg * k_pow).T, RET_K, axis=1)
    cdec = jnp.broadcast_to(jnp.repeat(jnp.exp(lg * RET_CHUNK), RET_K, axis=0), (RET_HEADS * RET_K, RET_V))
    return intra, qdec, kdec, cdec


def _merge_kernel(x_ref, mod_ref, gain_ref, wg_ref, wb_ref, wo_ref, mla_ref, dif_ref, ret_ref, o_ref):
    x = x_ref[0]
    d = x.shape[1]
    h = _modulated(x, gain_ref[1:2, :], mod_ref[0, 3:4, :], mod_ref[0, 4:5, :]).astype(BF16)
    branches = (lax.dot_general(mla_ref[0], wb_ref[0], _TN, preferred_element_type=F32),
                lax.dot_general(dif_ref[0], wb_ref[1], _TN, preferred_element_type=F32),
                _dot(ret_ref[0], wb_ref[2]))
    y = None
    for j, br in enumerate(branches):
        t = jax.nn.sigmoid(_dot(h, wg_ref[:, j * d:(j + 1) * d])) * br
        y = t if y is None else y + t
    o_ref[0] = x + mod_ref[0, 5:6, :] * _dot(y.astype(BF16), wo_ref[...])


def _merge(x, mod, gain, wg, wb, wo, mla_t, dif_t, ret):
    b, s, d = x.shape
    tm = min(TOKEN_TILE, s)
    tok = lambda w: pl.BlockSpec((1, tm, w), lambda bi, i: (bi, i, 0))
    feat = pl.BlockSpec((1, BRANCH_W, tm), lambda bi, i: (bi, 0, i))
    return pl.pallas_call(
        _merge_kernel,
        grid=(b, s // tm),
        in_specs=[tok(d), pl.BlockSpec((1, N_MOD, d), lambda bi, i: (bi, 0, 0)), _const(gain.shape),
                  _const(wg.shape), _const(wb.shape), _const(wo.shape), feat, feat, tok(BRANCH_W)],
        out_specs=tok(d),
        out_shape=jax.ShapeDtypeStruct(x.shape, F32),
        compiler_params=_params("parallel", "parallel"),
        name="merge",
    )(x, mod, gain, wg, wb, wo, mla_t, dif_t, ret)


def _rot_cols(w, group):
    r, n = w.shape
    w = w.reshape(r, n // group, 2, group // 2)
    return jnp.stack([-w[:, :, 1], w[:, :, 0]], axis=2).reshape(r, n)


def _head_slots(w, n_heads, width):
    r = w.shape[0]
    w = w.reshape(r, n_heads, width)
    return jnp.pad(w, ((0, 0), (0, 0), (0, LANES - width))).reshape(r, n_heads * LANES)


def _proj_weights(w_in, w_qb, w_kvb, q_norm, kv_norm):
    sizes = (MLA_Q_RANK, MLA_KV_RANK, MLA_ROPE, 2 * DIFF_HEADS * DIFF_QK, 2 * DIFF_HEADS * DIFF_QK,
             DIFF_HEADS * DIFF_V, RET_HEADS * RET_K, RET_HEADS * RET_K, RET_HEADS * RET_V, RET_HEADS * RET_V)
    offs = [0]
    for sz in sizes:
        offs.append(offs[-1] + sz)
    p = [w_in[:, offs[i]:offs[i + 1]] for i in range(len(sizes))]
    gates = w_in[:, offs[-1]:]
    d = w_in.shape[0]
    lead = jnp.zeros((d, MLA_NOPE), F32)
    tail = jnp.zeros((d, LANES - MLA_NOPE - MLA_ROPE), F32)
    main = jnp.concatenate([
        p[0], p[1],
        lead, p[2], tail, lead, _rot_cols(p[2], MLA_ROPE // 2), tail,
        p[3], p[4], p[6], p[7], p[8], p[9]], axis=1)
    assert main.shape[1] == _W_MAIN
    hd = MLA_NOPE + MLA_ROPE
    qb = w_qb.reshape(MLA_Q_RANK, MLA_HEADS, hd)
    q_rot = _rot_cols(qb[:, :, MLA_NOPE:].reshape(MLA_Q_RANK, MLA_HEADS * MLA_ROPE), MLA_ROPE // 2)
    q_rot = jnp.concatenate([jnp.zeros((MLA_Q_RANK, MLA_HEADS, MLA_NOPE), F32),
                             q_rot.reshape(MLA_Q_RANK, MLA_HEADS, MLA_ROPE)], axis=2)
    kvb = w_kvb.reshape(MLA_KV_RANK, MLA_HEADS, MLA_NOPE + MLA_V)
    return {
        "main": main.astype(BF16),
        "qa": _head_slots(w_qb, MLA_HEADS, hd).T.astype(BF16),
        "qb": _head_slots(q_rot.reshape(MLA_Q_RANK, MLA_HEADS * hd), MLA_HEADS, hd).T.astype(BF16),
        "ka": _head_slots(kvb[:, :, :MLA_NOPE].reshape(MLA_KV_RANK, -1), MLA_HEADS, MLA_NOPE).astype(BF16),
        "vt": kvb[:, :, MLA_NOPE:].reshape(MLA_KV_RANK, -1).T.astype(BF16),
        "dvt": p[5].T.astype(BF16),
        "qn": q_norm.reshape(1, -1),
        "kvn": kv_norm.reshape(1, -1),
    }, gates.astype(BF16)


def _angles(pos, dim):
    inv = ROPE_BASE ** (-jnp.arange(0, dim, 2, dtype=F32) / dim)
    ang = pos.astype(F32)[:, None] * inv[None, :]
    return jnp.cos(ang), jnp.sin(ang)


def _rope_tables(n):
    t = jnp.arange(n)
    row = jnp.repeat(jnp.arange(n // GRID_W), GRID_W)
    col = t - row * GRID_W

    def axial(dim):
        cr, sr = _angles(row, dim // 2)
        cc, sc = _angles(col, dim // 2)
        return jnp.concatenate([cr, cr, cc, cc], axis=1), jnp.concatenate([sr, sr, sc, sc], axis=1)

    cm, sm = axial(MLA_ROPE)
    ones = jnp.ones((n, MLA_NOPE), F32)
    pad = jnp.zeros((n, LANES - MLA_NOPE - MLA_ROPE), F32)
    cm = jnp.concatenate([ones, cm, pad], axis=1)
    sm = jnp.concatenate([0 * ones, sm, pad], axis=1)
    cd, sd = axial(DIFF_QK)
    cs, ss = _angles(t, RET_K)
    sd = sd * jnp.repeat(jnp.array([-1.0, 1.0, -1.0, 1.0], F32), DIFF_QK // 4)
    return (cm, sm, jnp.tile(cd, (1, 2)), jnp.tile(sd, (1, 2)),
            jnp.tile(cs, (1, 4)), jnp.tile(jnp.concatenate([-ss, ss], axis=1), (1, 2)),
            cm.T, sm.T)


def _identity_tables(n):
    one, zero = jnp.ones((n, LANES), F32), jnp.zeros((n, LANES), F32)
    return one, zero, one, zero, one, zero, one.T, zero.T


def _chunked(w, axis):
    if axis == 1:
        r = w.shape[0]
        return w.reshape(r, D_FF // FF_CHUNK, FF_CHUNK).transpose(1, 0, 2).astype(BF16)
    return w.reshape(D_FF // FF_CHUNK, FF_CHUNK, w.shape[1]).astype(BF16)


def kernel(x, c, ctx, c_ctx, ada_w, ada_b, norm_gain, ffn1_w1, ffn1_w3, ffn1_w2, ffn2_w1, ffn2_w3, ffn2_w2,
           w_in, mla_q_norm, mla_w_qb, mla_kv_norm, mla_w_kvb, diff_lambda, diff_norm, ret_decay, ret_norm,
           w_branch, w_out, final_norm):
    b, n, d = x.shape
    n_ctx = ctx.shape[1]
    pad_rows = -(b + 1) % 8
    cs = jnp.concatenate([c, c_ctx[None, :], jnp.zeros((pad_rows, d), F32)], axis=0)
    mods = _ada(cs, ada_w, ada_b)
    x_tabs, c_tabs = _rope_tables(n), _identity_tables(n_ctx)
    zero_state = jnp.zeros((b, RET_HEADS * RET_K, RET_V), F32)
    n_x, n_all = n // KEY_CHUNK, (n + n_ctx) // KEY_CHUNK
    xc = ctx
    for l in range(DEPTH):
        lambda_init = 0.8 - 0.6 * math.exp(-0.3 * l)
        ctx_out = l < DEPTH - 1
        mx = mods[l, :b].reshape(b, N_MOD, d)
        mc = jnp.broadcast_to(mods[l, b].reshape(1, N_MOD, d), (b, N_MOD, d))
        gain = norm_gain[l]
        f1 = (_chunked(ffn1_w1[l], 1), _chunked(ffn1_w3[l], 1), _chunked(ffn1_w2[l], 0))
        f2 = (_chunked(ffn2_w1[l], 1), _chunked(ffn2_w3[l], 1), _chunked(ffn2_w2[l], 0))
        pw, wg = _proj_weights(w_in[l], mla_w_qb[l], mla_w_kvb[l], mla_q_norm[l], mla_kv_norm[l])
        wb, wo = w_branch[l].astype(BF16), w_out[l].astype(BF16)
        dl = diff_lambda[l].astype(F32)
        lam = jnp.exp(jnp.sum(dl[0] * dl[1])) - jnp.exp(jnp.sum(dl[2] * dl[3])) + lambda_init
        log_gamma = -jnp.exp(ret_decay[l].astype(F32))
        tab_f, tab_b = _decay_tables(log_gamma[0], False), _decay_tables(log_gamma[1], True)

        x = _ffn(x, mx, gain, *f1, row0=0, gain_row=0)
        xc = _ffn(xc, mc, gain, *f1, row0=0, gain_row=0)

        mq_t, mk, mvt, dq, dk, dvt, rq, rk, rv, rg = _proj(x, mx, gain, pw, x_tabs, n_keys=n + n_ctx)
        cmq_t, mk, mvt, cdq, dk, dvt, crq, crk, crv, crg = _proj(
            xc, mc, gain, pw, c_tabs, n_keys=n + n_ctx, key_off=n_x, into=(mk, mvt, dk, dvt))

        mla_t = _mla_attention(mq_t, mk, mvt, (0, n_all))
        dif_t = _diff_attention(dq, dk, dvt, (0, n_all), lam, diff_norm[l], lambda_init)

        oc_f, st_f = _retention(crq, crk, crv, zero_state, tab_f, reverse=False)
        ret_c, st_b = _retention(crq, crk, crv, zero_state, tab_b, reverse=True,
                                 post=(oc_f, crg, ret_norm[l]) if ctx_out else None)
        ox_f, _ = _retention(rq, rk, rv, st_f, tab_f, reverse=False)
        ret_x, _ = _retention(rq, rk, rv, st_b, tab_b, reverse=True, post=(ox_f, rg, ret_norm[l]))

        x = _merge(x, mx, gain, wg, wb, wo, mla_t, dif_t, ret_x)
        x = _ffn(x, mx, gain, *f2, row0=6, gain_row=2, final_gain=None if ctx_out else final_norm)
        if ctx_out:
            mla_c = _mla_attention(cmq_t, mk, mvt, (n_x, n_all))
            dif_c = _diff_attention(cdq, dk, dvt, (n_x, n_all), lam, diff_norm[l], lambda_init)
            xc = _merge(xc, mc, gain, wg, wb, wo, mla_c, dif_c, ret_c)
            xc = _ffn(xc, mc, gain, *f2, row0=6, gain_row=2)
    return x
```

```python
import functools
import math

import jax
import jax.numpy as jnp
from jax import lax
from jax.experimental import pallas as pl
from jax.experimental.pallas import tpu as pltpu

D_MODEL = 1024
DEPTH = 2
GRID_W = 64
N_BRANCH = 3
MLA_HEADS = 8
MLA_Q_RANK = 384
MLA_KV_RANK = 256
MLA_NOPE = 64
MLA_ROPE = 32
MLA_V = 64
DIFF_HEADS = 4
DIFF_QK = 64
DIFF_V = 128
RET_HEADS = 4
RET_K = 64
RET_V = 128
RET_CHUNK = 128
BRANCH_W = 512
D_FF = 2816
N_MOD = 9
ROPE_BASE = 10000.0
EPS = 1e-6

LANES = 128
SUBLANES = 8
LOG2E = 1.4426950408889634
FF_CHUNK = 256
KEY_CHUNK = 256
TOKEN_TILE = 512
RET_TILE = 2048
MLA_Q_TILE = 256
DIFF_Q_TILE = 256
VMEM_LIMIT = 56 * 1024 * 1024

BF16 = jnp.bfloat16
F32 = jnp.float32
_NT = (((1,), (1,)), ((), ()))
_TN = (((0,), (0,)), ((), ()))


def _params(*sem):
    return pltpu.CompilerParams(dimension_semantics=sem, vmem_limit_bytes=VMEM_LIMIT)


def _const(shape):
    zeros = (0,) * len(shape)
    return pl.BlockSpec(shape, lambda *_: zeros, pipeline_mode=pl.Buffered(1))


def _rms(x, gain):
    return x * lax.rsqrt(jnp.mean(x * x, axis=-1, keepdims=True) + EPS) * gain


def _modulated(x, gain, shift, scale):
    return _rms(x, gain) * (1.0 + scale) + shift


def _dot(a, b):
    return jnp.dot(a, b, preferred_element_type=F32)


def _ada_kernel(c_ref, w_ref, b_ref, o_ref):
    c = c_ref[...]
    o_ref[0] = _dot(c * jax.nn.sigmoid(c), w_ref[0]) + b_ref[0]


def _ada(cs, ada_w, ada_b):
    nl, d, n = ada_w.shape
    r = cs.shape[0]
    tn = n // N_MOD
    return pl.pallas_call(
        _ada_kernel,
        grid=(nl, n // tn),
        in_specs=[pl.BlockSpec((r, d), lambda l, j: (0, 0)),
                  pl.BlockSpec((1, d, tn), lambda l, j: (l, 0, j)),
                  pl.BlockSpec((1, 1, tn), lambda l, j: (l, 0, j))],
        out_specs=pl.BlockSpec((1, r, tn), lambda l, j: (l, 0, j)),
        out_shape=jax.ShapeDtypeStruct((nl, r, n), F32),
        compiler_params=_params("parallel", "parallel"),
        name="ada",
    )(cs, ada_w, ada_b.reshape(nl, 1, n))


def _ffn_kernel(x_ref, mod_ref, gain_ref, w1_ref, w3_ref, w2_ref, *rest, row0, gain_row, final):
    if final:
        fg_ref, o_ref, acc_ref = rest
    else:
        o_ref, acc_ref = rest
    x = x_ref[0]
    h = _modulated(x, gain_ref[gain_row:gain_row + 1, :], mod_ref[0, row0:row0 + 1, :],
                   mod_ref[0, row0 + 1:row0 + 2, :]).astype(BF16)
    n_chunks = w1_ref.shape[0]
    up = (_dot(h, w1_ref[0]), _dot(h, w3_ref[0]))
    for c in range(n_chunks):
        a, b = up
        if c + 1 < n_chunks:
            up = (_dot(h, w1_ref[c + 1]), _dot(h, w3_ref[c + 1]))
        t = _dot((a * jax.nn.sigmoid(a) * b).astype(BF16), w2_ref[c])
        if c == 0:
            acc_ref[...] = t
        else:
            acc_ref[...] += t
    y = x + (mod_ref[0, row0 + 2:row0 + 3, :] * 0.5) * acc_ref[...]
    if final:
        y = _rms(y, fg_ref[...])
    o_ref[0] = y


def _ffn(x, mod, gain, w1, w3, w2, *, row0, gain_row, final_gain=None):
    b, s, d = x.shape
    tm = min(TOKEN_TILE, s)
    final = final_gain is not None
    in_specs = [pl.BlockSpec((1, tm, d), lambda bi, i: (bi, i, 0)),
                pl.BlockSpec((1, N_MOD, d), lambda bi, i: (bi, 0, 0)),
                _const(gain.shape), _const(w1.shape), _const(w3.shape), _const(w2.shape)]
    args = [x, mod, gain, w1, w3, w2]
    if final:
        in_specs.append(_const((1, d)))
        args.append(final_gain.reshape(1, d))
    return pl.pallas_call(
        functools.partial(_ffn_kernel, row0=row0, gain_row=gain_row, final=final),
        grid=(b, s // tm),
        in_specs=in_specs,
        out_specs=pl.BlockSpec((1, tm, d), lambda bi, i: (bi, i, 0)),
        out_shape=jax.ShapeDtypeStruct(x.shape, F32),
        scratch_shapes=[pltpu.VMEM((tm, d), F32)],
        compiler_params=_params("parallel", "parallel"),
        name="ffn",
    )(*args)


_O_QLAT = 0
_O_KVLAT = _O_QLAT + MLA_Q_RANK
_O_KR = _O_KVLAT + MLA_KV_RANK
_O_DQ = _O_KR + 2 * LANES
_O_RQ = _O_DQ + 2 * BRANCH_W
_O_RV = _O_RQ + 2 * RET_HEADS * RET_K
_W_MAIN = _O_RV + 2 * BRANCH_W


def _rotary(x, cos, sin_signed, half):
    lane = lax.broadcasted_iota(jnp.int32, x.shape, 1)
    first = (lane & (2 * half - 1)) < half
    partner = jnp.where(first, pltpu.roll(x, LANES - half, 1), pltpu.roll(x, half, 1))
    return x * cos + partner * sin_signed


_KEY_OUTPUTS = (1, 2, 4, 5)


def _proj_kernel(x_ref, mod_ref, gain_ref, wm_ref, wqa_ref, wka_ref, wvt_ref, wdvt_ref,
                 qn_ref, kvn_ref, cm_ref, sm_ref, cd_ref, sd_ref, cr_ref, sr_ref, cmt_ref, smt_ref, *rest):
    mqt_ref, mk_ref, mvt_ref, dq_ref, dk_ref, dvt_ref, rq_ref, rk_ref, rv_ref, rg_ref = rest[-10:]
    x = x_ref[0]
    h = _modulated(x, gain_ref[1:2, :], mod_ref[0, 3:4, :], mod_ref[0, 4:5, :]).astype(BF16)

    def seg(lo, width):
        return _dot(h, wm_ref[:, lo:lo + width])

    hk = RET_HEADS * RET_K
    qlat_raw, kvlat_raw = seg(_O_QLAT, MLA_Q_RANK), seg(_O_KVLAT, MLA_KV_RANK)
    kr = seg(_O_KR, 2 * LANES)
    dqk = seg(_O_DQ, 2 * BRANCH_W)
    rqk = seg(_O_RQ, 2 * hk)
    rv_ref[0] = seg(_O_RV, BRANCH_W).astype(BF16)
    rg_ref[0] = seg(_O_RV + BRANCH_W, BRANCH_W).astype(BF16)
    dvt_ref[0, 0] = lax.dot_general(wdvt_ref[...], h, _NT, preferred_element_type=F32).astype(BF16)

    cd, sd = cd_ref[...], sd_ref[...]
    d_scale = DIFF_QK ** -0.5 * LOG2E
    for j in range(BRANCH_W // LANES):
        sl = slice(j * LANES, (j + 1) * LANES)
        sk = slice(BRANCH_W + j * LANES, BRANCH_W + (j + 1) * LANES)
        dq_ref[0, :, sl] = (_rotary(dqk[:, sl], cd, sd, DIFF_QK // 4) * d_scale).astype(BF16)
        dk_ref[0, :, sl] = _rotary(dqk[:, sk], cd, sd, DIFF_QK // 4).astype(BF16)

    cr, sr_t = cr_ref[...], sr_ref[...]
    for j in range(hk // LANES):
        sl = slice(j * LANES, (j + 1) * LANES)
        sk = slice(hk + j * LANES, hk + (j + 1) * LANES)
        rq_ref[0, :, sl] = _rotary(rqk[:, sl], cr, sr_t, RET_K // 2).astype(BF16)
        rk_ref[0, :, sl] = (_rotary(rqk[:, sk], cr, sr_t, RET_K // 2) * RET_K ** -0.5).astype(BF16)

    qlat = _rms(qlat_raw, qn_ref[...]).astype(BF16)
    kvlat = _rms(kvlat_raw, kvn_ref[...]).astype(BF16)
    cm, sm = cm_ref[...], sm_ref[...]
    cmt, smt = cmt_ref[...], smt_ref[...]
    qat = lax.dot_general(wqa_ref[...], qlat, _NT, preferred_element_type=F32)
    ka = _dot(kvlat, wka_ref[...])
    mvt_ref[0, 0] = lax.dot_general(wvt_ref[...], kvlat, _NT, preferred_element_type=F32).astype(BF16)
    krr = kr[:, :LANES] * cm + kr[:, LANES:] * sm
    q_scale = (MLA_NOPE + MLA_ROPE) ** -0.5 * LOG2E
    half = MLA_ROPE // 4
    for hd in range(MLA_HEADS):
        sl = slice(hd * LANES, (hd + 1) * LANES)
        q = qat[sl, :]
        swapped = [q[MLA_NOPE + (i ^ 1) * half:MLA_NOPE + ((i ^ 1) + 1) * half] for i in range(MLA_ROPE // half)]
        partner = jnp.concatenate([q[:MLA_NOPE]] + swapped + [q[MLA_NOPE + MLA_ROPE:]], axis=0)
        mqt_ref[0, sl, :] = ((q * cmt + partner * smt) * q_scale).astype(BF16)
        mk_ref[0, :, sl] = (ka[:, sl] + krr).astype(BF16)


def _proj(x, mod, gain, pw, tables, *, n_keys, key_off=0, into=None):
    b, s, d = x.shape
    tm = KEY_CHUNK
    nt = s // tm
    tok = lambda w, off=0: pl.BlockSpec((1, tm, w), lambda bi, i: (bi, i + off, 0))
    slab = pl.BlockSpec((1, 1, BRANCH_W, tm), lambda bi, i: (bi, i + key_off, 0, 0))
    tab = pl.BlockSpec((tm, LANES), lambda bi, i: (i, 0))
    tab_t = pl.BlockSpec((LANES, tm), lambda bi, i: (0, i))
    weights = [pw["main"], pw["qa"], pw["ka"], pw["vt"], pw["dvt"], pw["qn"], pw["kvn"]]
    hk = RET_HEADS * RET_K
    widths = [MLA_HEADS * LANES, MLA_HEADS * LANES, None, BRANCH_W, BRANCH_W, None, hk, hk, BRANCH_W, BRANCH_W]
    out_specs, out_shape = [], []
    for pos, w in enumerate(widths):
        if pos == 0:
            out_specs.append(pl.BlockSpec((1, w, tm), lambda bi, i: (bi, 0, i)))
            out_shape.append(jax.ShapeDtypeStruct((b, w, s), BF16))
        elif w is None:
            out_specs.append(slab)
            out_shape.append(jax.ShapeDtypeStruct((b, n_keys // tm, BRANCH_W, tm), BF16))
        elif pos in _KEY_OUTPUTS:
            out_specs.append(tok(w, key_off))
            out_shape.append(jax.ShapeDtypeStruct((b, n_keys, w), BF16))
        else:
            out_specs.append(tok(w))
            out_shape.append(jax.ShapeDtypeStruct((b, s, w), BF16))
    in_specs = [pl.BlockSpec((1, tm, d), lambda bi, i: (bi, i, 0)),
                pl.BlockSpec((1, N_MOD, d), lambda bi, i: (bi, 0, 0)),
                _const(gain.shape)] + [_const(w.shape) for w in weights] + [tab] * 6 + [tab_t] * 2
    args = [x, mod, gain, *weights, *tables]
    aliases = {}
    if into is not None:
        for buf, pos in zip(into, _KEY_OUTPUTS):
            aliases[len(args)] = pos
            in_specs.append(pl.BlockSpec(memory_space=pl.ANY))
            args.append(buf)
    return pl.pallas_call(
        _proj_kernel,
        grid=(b, nt),
        in_specs=in_specs,
        out_specs=out_specs,
        out_shape=out_shape,
        input_output_aliases=aliases,
        compiler_params=_params("parallel", "parallel"),
        name="proj",
    )(*args)


SUM_ROWS = 16


def _slabs_per_chunk(n_slabs):
    return next(p for p in (3, 2, 1) if n_slabs % p == 0)


def _attend(tiles, upcoming, first, k_ref, vt_ref, s_ref, mx_ref, dv, *, q_transposed):
    n = tiles[0][0][0].shape[1 if q_transposed else 0]
    n_slabs = vt_ref.shape[1]
    per = _slabs_per_chunk(n_slabs)
    tk = per * KEY_CHUNK
    n_chunks = n_slabs // per
    ones = jnp.ones((SUM_ROWS, KEY_CHUNK), BF16)

    def score(chains, c, buf, g):
        q_rhs, lanes, _ = chains[g]
        start = c * tk if isinstance(c, int) else pl.multiple_of(c * tk, KEY_CHUNK)
        k = k_ref[0, pl.ds(start, tk), lanes]
        s = _dot(k, q_rhs) if q_transposed else lax.dot_general(k, q_rhs, _NT, preferred_element_type=F32)
        s_ref[buf, g] = s
        for j in range(per):
            part = s[j * KEY_CHUNK:(j + 1) * KEY_CHUNK]
            mx_ref[buf, g, j] = jnp.max(part.reshape(KEY_CHUNK // SUBLANES, SUBLANES, n), axis=0)

    def consume(chains, c, buf, carry, following):
        out = []
        for g, ((_, _, rows), (m, acc)) in enumerate(zip(chains, carry)):
            if following is not None:
                score(following[0], following[1], 1 - buf, g)
            mx = mx_ref[buf, g, 0]
            for j in range(1, per):
                mx = jnp.maximum(mx, mx_ref[buf, g, j])
            m_new = jnp.maximum(m, jnp.max(mx, axis=0, keepdims=True))
            alpha = jnp.exp2(m - m_new)
            pb = jnp.exp2(s_ref[buf, g] - m_new).astype(BF16)
            pv = None
            for j in range(per):
                vt = jnp.concatenate([vt_ref[0, c * per + j, rows, :], ones], axis=0)
                t = _dot(vt, pb[j * KEY_CHUNK:(j + 1) * KEY_CHUNK, :])
                pv = t if pv is None else pv + t
            out.append((m_new, alpha * acc + pv))
        return tuple(out)

    def prologue():
        for g in range(len(tiles[0])):
            score(tiles[0], 0, 0, g)

    if first is None:
        prologue()
    else:
        pl.when(first)(prologue)

    results = []
    start = 0
    for t, chains in enumerate(tiles):
        following = tiles[t + 1] if t + 1 < len(tiles) else upcoming

        def step(c, buf, carry, chains=chains):
            return consume(chains, c, buf, carry, (chains, c + 1))

        carry = tuple((jnp.full((1, n), -jnp.inf, F32), jnp.zeros((dv + SUM_ROWS, n), F32)) for _ in chains)
        n_pairs = (n_chunks - 1) // 2
        if n_pairs > 0:
            carry = lax.fori_loop(
                0, n_pairs, lambda j, cr, s0=start, step=step: step(2 * j + 1, 1 - s0, step(2 * j, s0, cr)), carry)
        if (n_chunks - 1) % 2:
            carry = step(n_chunks - 2, (start + n_chunks - 2) % 2, carry)
        last = (start + n_chunks - 1) % 2
        carry = consume(chains, n_chunks - 1, last, carry, None if following is None else (following, 0))
        results.append([(acc[:dv], acc[dv:dv + 1]) for _, acc in carry])
        start = 1 - last
    assert upcoming is None or start == 0
    return results


def _tiles_per_step(n_tiles, n_slabs):
    n_chunks = n_slabs // _slabs_per_chunk(n_slabs)
    if n_tiles % 2 == 0:
        return 2
    assert n_tiles == 1 or n_chunks % 2 == 0
    return 1


def _kv_specs(k_width, v_rows, index, key_range):
    lo, hi = key_range
    n_slabs = hi - lo
    assert lo % n_slabs == 0
    blk = lo // n_slabs
    return [pl.BlockSpec((1, n_slabs * KEY_CHUNK, k_width), lambda *g: (g[0], blk, index(g)),
                         pipeline_mode=pl.Buffered(1)),
            pl.BlockSpec((1, n_slabs, v_rows, KEY_CHUNK), lambda *g: (g[0], blk, index(g), 0),
                         pipeline_mode=pl.Buffered(1))]


def _score_scratch(n_chains, n_slabs, n):
    per = _slabs_per_chunk(n_slabs)
    return [pltpu.VMEM((2, n_chains, per * KEY_CHUNK, n), F32),
            pltpu.VMEM((2, n_chains, per, SUBLANES, n), F32)]


def _mla_kernel(q_ref, qn_ref, k_ref, vt_ref, o_ref, s_ref, mx_ref, *, tq, chained):
    def chains_of(ref, t):
        return [(ref[0, g * LANES:(g + 1) * LANES, t * tq:(t + 1) * tq], slice(g * LANES, (g + 1) * LANES),
                 slice(g * MLA_V, (g + 1) * MLA_V)) for g in range(MLA_HEADS)]

    tiles = [chains_of(q_ref, t) for t in range(q_ref.shape[2] // tq)]
    res = _attend(tiles, chains_of(qn_ref, 0) if chained else None,
                  pl.program_id(1) == 0 if chained else None,
                  k_ref, vt_ref, s_ref, mx_ref, MLA_V, q_transposed=True)
    for t, (chains, out) in enumerate(zip(tiles, res)):
        for (_, _, rows), (acc, l) in zip(chains, out):
            o_ref[0, rows, t * tq:(t + 1) * tq] = (acc / l).astype(BF16)


def _mla_attention(q, k, vt, key_range):
    b, w, s = q.shape
    tq = min(MLA_Q_TILE, s)
    n_tiles = s // tq
    n_slabs = key_range[1] - key_range[0]
    per_step = _tiles_per_step(n_tiles, n_slabs)
    n_steps = n_tiles // per_step
    return pl.pallas_call(
        functools.partial(_mla_kernel, tq=tq, chained=n_steps > 1),
        grid=(b, n_steps),
        in_specs=[pl.BlockSpec((1, w, per_step * tq), lambda bi, i: (bi, 0, i)),
                  pl.BlockSpec((1, w, tq), lambda bi, i: (bi, 0, jnp.minimum((i + 1) * per_step, n_tiles - 1)))]
                 + _kv_specs(w, MLA_HEADS * MLA_V, lambda g: 0, key_range),
        out_specs=pl.BlockSpec((1, MLA_HEADS * MLA_V, per_step * tq), lambda bi, i: (bi, 0, i)),
        out_shape=jax.ShapeDtypeStruct((b, MLA_HEADS * MLA_V, s), BF16),
        scratch_shapes=_score_scratch(MLA_HEADS, n_slabs, tq),
        compiler_params=_params("parallel", "arbitrary"),
        name="mla_attn",
    )(q, q, k, vt)


def _diff_kernel(lam_ref, q_ref, qn_ref, k_ref, vt_ref, gain_ref, o_ref, s_ref, mx_ref, *, tq, chained, out_scale):
    first_half = lax.broadcasted_iota(jnp.int32, (tq, LANES), 1) < DIFF_QK
    zero = jnp.zeros((tq, LANES), BF16)

    def chains_of(ref, t):
        chains = []
        for hd in range(DIFF_HEADS):
            lanes = slice(hd * LANES, (hd + 1) * LANES)
            q = ref[0, t * tq:(t + 1) * tq, lanes]
            q_rhs = jnp.concatenate([jnp.where(first_half, q, zero), jnp.where(first_half, zero, q)], axis=0)
            chains.append((q_rhs, lanes, slice(hd * DIFF_V, (hd + 1) * DIFF_V)))
        return chains

    tiles = [chains_of(q_ref, t) for t in range(q_ref.shape[1] // tq)]
    res = _attend(tiles, chains_of(qn_ref, 0) if chained else None,
                  pl.program_id(1) == 0 if chained else None,
                  k_ref, vt_ref, s_ref, mx_ref, DIFF_V, q_transposed=False)
    for t, (chains, out) in enumerate(zip(tiles, res)):
        for (_, _, rows), (acc, l) in zip(chains, out):
            o = acc / l
            o = o[:, :tq] - lam_ref[0] * o[:, tq:]
            o = o * lax.rsqrt(jnp.mean(o * o, axis=0, keepdims=True) + EPS) * gain_ref[...]
            o_ref[0, rows, t * tq:(t + 1) * tq] = (o * out_scale).astype(BF16)


def _diff_attention(q, k, vt, key_range, lam, gain, lambda_init):
    b, s, w = q.shape
    tq = min(DIFF_Q_TILE, s)
    n_tiles = s // tq
    n_slabs = key_range[1] - key_range[0]
    per_step = _tiles_per_step(n_tiles, n_slabs)
    n_steps = n_tiles // per_step
    return pl.pallas_call(
        functools.partial(_diff_kernel, tq=tq, chained=n_steps > 1, out_scale=1.0 - lambda_init),
        grid=(b, n_steps),
        in_specs=[pl.BlockSpec(memory_space=pltpu.SMEM),
                  pl.BlockSpec((1, per_step * tq, w), lambda bi, i: (bi, i, 0)),
                  pl.BlockSpec((1, tq, w), lambda bi, i: (bi, jnp.minimum((i + 1) * per_step, n_tiles - 1), 0))]
                 + _kv_specs(w, DIFF_HEADS * DIFF_V, lambda g: 0, key_range)
                 + [pl.BlockSpec((DIFF_V, 1), lambda bi, i: (0, 0))],
        out_specs=pl.BlockSpec((1, DIFF_HEADS * DIFF_V, per_step * tq), lambda bi, i: (bi, 0, i)),
        out_shape=jax.ShapeDtypeStruct((b, DIFF_HEADS * DIFF_V, s), BF16),
        scratch_shapes=_score_scratch(DIFF_HEADS, n_slabs, 2 * tq),
        compiler_params=_params("parallel", "arbitrary"),
        name="diff_attn",
    )(lam.reshape(1), q, q, k, vt, gain.reshape(DIFF_V, 1))


def _ret_kernel(q_ref, k_ref, v_ref, s0_ref, intra_ref, qdec_ref, kdec_ref, cdec_ref, *rest, reverse, post):
    if post:
        of_ref, g_ref, gain_ref, o_ref, sout_ref, state = rest
    else:
        o_ref, sout_ref, state = rest
    i = pl.program_id(1)

    @pl.when(i == 0)
    def _():
        state[...] = s0_ref[0]

    n_chunks = q_ref.shape[1] // RET_CHUNK
    lane = lax.broadcasted_iota(jnp.int32, (RET_CHUNK, RET_HEADS * RET_K), 1)
    masks = [(lane >= hd * RET_K) & (lane < (hd + 1) * RET_K) for hd in range(RET_HEADS)]
    order = [n_chunks - 1 - cc if reverse else cc for cc in range(n_chunks)]
    heads = range(RET_HEADS)

    def head_v(c, hd):
        return v_ref[0, c * RET_CHUNK:(c + 1) * RET_CHUNK, hd * RET_V:(hd + 1) * RET_V]

    intra, deltas = {}, {}
    for c in order:
        rows = slice(c * RET_CHUNK, (c + 1) * RET_CHUNK)
        qc, kc = q_ref[0, rows, :], k_ref[0, rows, :]
        kdt = (kc.astype(F32) * kdec_ref[...]).T.astype(BF16)
        for hd in heads:
            a = lax.dot_general(jnp.where(masks[hd], qc, jnp.zeros_like(qc)), kc, _NT,
                                preferred_element_type=F32)
            intra[c, hd] = (a * intra_ref[hd]).astype(BF16)
            deltas[c, hd] = _dot(kdt[hd * RET_K:(hd + 1) * RET_K, :], head_v(c, hd))
    st = state[...]
    entering = {}
    for c in order:
        entering[c] = st.astype(BF16)
        st = st * cdec_ref[...] + jnp.concatenate([deltas[c, hd] for hd in heads], axis=0)
    state[...] = st
    for c in order:
        rows = slice(c * RET_CHUNK, (c + 1) * RET_CHUNK)
        qd = (q_ref[0, rows, :].astype(F32) * qdec_ref[...]).astype(BF16)
        outs = [_dot(intra[c, hd], head_v(c, hd))
                + _dot(jnp.where(masks[hd], qd, jnp.zeros_like(qd)), entering[c]) for hd in heads]
        if post:
            of = of_ref[0, rows, :]
            g = g_ref[0, rows, :].astype(F32)
            for hd in range(RET_HEADS):
                sl = slice(hd * RET_V, (hd + 1) * RET_V)
                outs[hd] = _rms(outs[hd] + of[:, sl], gain_ref[...]) * (g[:, sl] * jax.nn.sigmoid(g[:, sl]))
        o_ref[0, rows, :] = jnp.concatenate(outs, axis=1).astype(o_ref.dtype)

    @pl.when(i == pl.num_programs(1) - 1)
    def _():
        sout_ref[0] = state[...]


def _retention(q, k, v, s0, tabs, *, reverse, post=None):
    b, n, _ = q.shape
    t = min(RET_TILE, n)
    nt = n // t
    hk, hv = RET_HEADS * RET_K, RET_HEADS * RET_V
    tile = (lambda bi, i: (bi, nt - 1 - i, 0)) if reverse else (lambda bi, i: (bi, i, 0))
    st_spec = pl.BlockSpec((1, hk, RET_V), lambda bi, i: (bi, 0, 0))
    in_specs = [pl.BlockSpec((1, t, hk), tile), pl.BlockSpec((1, t, hk), tile), pl.BlockSpec((1, t, hv), tile),
                st_spec] + [_const(a.shape) for a in tabs]
    args = [q, k, v, s0, *tabs]
    if post is not None:
        o_f, g, gain = post
        in_specs += [pl.BlockSpec((1, t, hv), tile), pl.BlockSpec((1, t, hv), tile), _const((1, RET_V))]
        args += [o_f, g, gain.reshape(1, RET_V)]
    return pl.pallas_call(
        functools.partial(_ret_kernel, reverse=reverse, post=post is not None),
        grid=(b, nt),
        in_specs=in_specs,
        out_specs=[pl.BlockSpec((1, t, hv), tile), st_spec],
        out_shape=[jax.ShapeDtypeStruct((b, n, hv), F32 if post is None else BF16),
                   jax.ShapeDtypeStruct((b, hk, RET_V), F32)],
        scratch_shapes=[pltpu.VMEM((hk, RET_V), F32)],
        compiler_params=_params("parallel", "arbitrary"),
        name="ret_bwd" if reverse else "ret_fwd",
    )(*args)


def _decay_tables(log_gamma, reverse):
    idx = jnp.arange(RET_CHUNK, dtype=F32)
    lg = log_gamma.astype(F32)[:, None]
    rel = idx[:, None] - idx[None, :]
    if reverse:
        rel = -rel
        q_pow, k_pow = RET_CHUNK - idx, idx
    else:
        q_pow, k_pow = idx + 1.0, RET_CHUNK - 1.0 - idx
    intra = jnp.where(rel >= 0, jnp.exp(lg[:, :, None] * jnp.maximum(rel, 0.0)), 0.0)
    qdec = jnp.repeat(jnp.exp(lg * q_pow).T, RET_K, axis=1)
    kdec = jnp.repeat(jnp.exp(lg * k_pow).T, RET_K, axis=1)
    cdec = jnp.broadcast_to(jnp.repeat(jnp.exp(lg * RET_CHUNK), RET_K, axis=0), (RET_HEADS * RET_K, RET_V))
    return intra, qdec, kdec, cdec


def _merge_kernel(x_ref, mod_ref, gain_ref, wg_ref, wb_ref, wo_ref, mla_ref, dif_ref, ret_ref, o_ref):
    tm, d = x_ref.shape[1], x_ref.shape[2]
    halves = [slice(0, tm // 2), slice(tm // 2, tm)]

    def gated(rows):
        x = x_ref[0, rows, :]
        h = _modulated(x, gain_ref[1:2, :], mod_ref[0, 3:4, :], mod_ref[0, 4:5, :]).astype(BF16)
        branches = (lax.dot_general(mla_ref[0, :, rows], wb_ref[0], _TN, preferred_element_type=F32),
                    lax.dot_general(dif_ref[0, :, rows], wb_ref[1], _TN, preferred_element_type=F32),
                    _dot(ret_ref[0, rows, :], wb_ref[2]))
        y = None
        for j, br in enumerate(branches):
            t = jax.nn.sigmoid(_dot(h, wg_ref[:, j * d:(j + 1) * d])) * br
            y = t if y is None else y + t
        return x, y.astype(BF16)

    mixed = [gated(rows) for rows in halves]
    for rows, (x, y) in zip(halves, mixed):
        o_ref[0, rows, :] = x + mod_ref[0, 5:6, :] * _dot(y, wo_ref[...])


def _merge(x, mod, gain, wg, wb, wo, mla_t, dif_t, ret):
    b, s, d = x.shape
    tm = min(TOKEN_TILE, s)
    tok = lambda w: pl.BlockSpec((1, tm, w), lambda bi, i: (bi, i, 0))
    feat = pl.BlockSpec((1, BRANCH_W, tm), lambda bi, i: (bi, 0, i))
    return pl.pallas_call(
        _merge_kernel,
        grid=(b, s // tm),
        in_specs=[tok(d), pl.BlockSpec((1, N_MOD, d), lambda bi, i: (bi, 0, 0)), _const(gain.shape),
                  _const(wg.shape), _const(wb.shape), _const(wo.shape), feat, feat, tok(BRANCH_W)],
        out_specs=tok(d),
        out_shape=jax.ShapeDtypeStruct(x.shape, F32),
        compiler_params=_params("parallel", "parallel"),
        name="merge",
    )(x, mod, gain, wg, wb, wo, mla_t, dif_t, ret)


def _rot_cols(w, group):
    r, n = w.shape
    w = w.reshape(r, n // group, 2, group // 2)
    return jnp.stack([-w[:, :, 1], w[:, :, 0]], axis=2).reshape(r, n)


def _head_slots(w, n_heads, width):
    r = w.shape[0]
    w = w.reshape(r, n_heads, width)
    return jnp.pad(w, ((0, 0), (0, 0), (0, LANES - width))).reshape(r, n_heads * LANES)


def _proj_weights(w_in, w_qb, w_kvb, q_norm, kv_norm):
    sizes = (MLA_Q_RANK, MLA_KV_RANK, MLA_ROPE, 2 * DIFF_HEADS * DIFF_QK, 2 * DIFF_HEADS * DIFF_QK,
             DIFF_HEADS * DIFF_V, RET_HEADS * RET_K, RET_HEADS * RET_K, RET_HEADS * RET_V, RET_HEADS * RET_V)
    offs = [0]
    for sz in sizes:
        offs.append(offs[-1] + sz)
    p = [w_in[:, offs[i]:offs[i + 1]] for i in range(len(sizes))]
    gates = w_in[:, offs[-1]:]
    d = w_in.shape[0]
    lead = jnp.zeros((d, MLA_NOPE), F32)
    tail = jnp.zeros((d, LANES - MLA_NOPE - MLA_ROPE), F32)
    main = jnp.concatenate([
        p[0], p[1],
        lead, p[2], tail, lead, _rot_cols(p[2], MLA_ROPE // 2), tail,
        p[3], p[4], p[6], p[7], p[8], p[9]], axis=1)
    assert main.shape[1] == _W_MAIN
    hd = MLA_NOPE + MLA_ROPE
    kvb = w_kvb.reshape(MLA_KV_RANK, MLA_HEADS, MLA_NOPE + MLA_V)
    return {
        "main": main.astype(BF16),
        "qa": _head_slots(w_qb, MLA_HEADS, hd).T.astype(BF16),
        "ka": _head_slots(kvb[:, :, :MLA_NOPE].reshape(MLA_KV_RANK, -1), MLA_HEADS, MLA_NOPE).astype(BF16),
        "vt": kvb[:, :, MLA_NOPE:].reshape(MLA_KV_RANK, -1).T.astype(BF16),
        "dvt": p[5].T.astype(BF16),
        "qn": q_norm.reshape(1, -1),
        "kvn": kv_norm.reshape(1, -1),
    }, gates.astype(BF16)


def _angles(pos, dim):
    inv = ROPE_BASE ** (-jnp.arange(0, dim, 2, dtype=F32) / dim)
    ang = pos.astype(F32)[:, None] * inv[None, :]
    return jnp.cos(ang), jnp.sin(ang)


def _rope_tables(n):
    t = jnp.arange(n)
    row = jnp.repeat(jnp.arange(n // GRID_W), GRID_W)
    col = t - row * GRID_W

    def axial(dim):
        cr, sr = _angles(row, dim // 2)
        cc, sc = _angles(col, dim // 2)
        return jnp.concatenate([cr, cr, cc, cc], axis=1), jnp.concatenate([sr, sr, sc, sc], axis=1)

    cm, sm = axial(MLA_ROPE)
    ones = jnp.ones((n, MLA_NOPE), F32)
    pad = jnp.zeros((n, LANES - MLA_NOPE - MLA_ROPE), F32)
    cm = jnp.concatenate([ones, cm, pad], axis=1)
    sm = jnp.concatenate([0 * ones, sm, pad], axis=1)
    cd, sd = axial(DIFF_QK)
    cs, ss = _angles(t, RET_K)
    sd = sd * jnp.repeat(jnp.array([-1.0, 1.0, -1.0, 1.0], F32), DIFF_QK // 4)
    mla_sign = jnp.concatenate([jnp.ones((MLA_NOPE,), F32),
                                jnp.repeat(jnp.array([-1.0, 1.0, -1.0, 1.0], F32), MLA_ROPE // 4),
                                jnp.ones((LANES - MLA_NOPE - MLA_ROPE,), F32)])
    return (cm, sm, jnp.tile(cd, (1, 2)), jnp.tile(sd, (1, 2)),
            jnp.tile(cs, (1, 4)), jnp.tile(jnp.concatenate([-ss, ss], axis=1), (1, 2)),
            cm.T, (sm * mla_sign).T)


def _identity_tables(n):
    one, zero = jnp.ones((n, LANES), F32), jnp.zeros((n, LANES), F32)
    return one, zero, one, zero, one, zero, one.T, zero.T


def _chunked(w, axis):
    if axis == 1:
        r = w.shape[0]
        return w.reshape(r, D_FF // FF_CHUNK, FF_CHUNK).transpose(1, 0, 2).astype(BF16)
    return w.reshape(D_FF // FF_CHUNK, FF_CHUNK, w.shape[1]).astype(BF16)


def kernel(x, c, ctx, c_ctx, ada_w, ada_b, norm_gain, ffn1_w1, ffn1_w3, ffn1_w2, ffn2_w1, ffn2_w3, ffn2_w2,
           w_in, mla_q_norm, mla_w_qb, mla_kv_norm, mla_w_kvb, diff_lambda, diff_norm, ret_decay, ret_norm,
           w_branch, w_out, final_norm):
    b, n, d = x.shape
    n_ctx = ctx.shape[1]
    pad_rows = -(b + 1) % 8
    cs = jnp.concatenate([c, c_ctx[None, :], jnp.zeros((pad_rows, d), F32)], axis=0)
    mods = _ada(cs, ada_w, ada_b)
    x_tabs, c_tabs = _rope_tables(n), _identity_tables(n_ctx)
    zero_state = jnp.zeros((b, RET_HEADS * RET_K, RET_V), F32)
    n_x, n_all = n // KEY_CHUNK, (n + n_ctx) // KEY_CHUNK
    xc = ctx
    for l in range(DEPTH):
        lambda_init = 0.8 - 0.6 * math.exp(-0.3 * l)
        ctx_out = l < DEPTH - 1
        mx = mods[l, :b].reshape(b, N_MOD, d)
        mc = jnp.broadcast_to(mods[l, b].reshape(1, N_MOD, d), (b, N_MOD, d))
        gain = norm_gain[l]
        f1 = (_chunked(ffn1_w1[l], 1), _chunked(ffn1_w3[l], 1), _chunked(ffn1_w2[l], 0))
        f2 = (_chunked(ffn2_w1[l], 1), _chunked(ffn2_w3[l], 1), _chunked(ffn2_w2[l], 0))
        pw, wg = _proj_weights(w_in[l], mla_w_qb[l], mla_w_kvb[l], mla_q_norm[l], mla_kv_norm[l])
        wb, wo = w_branch[l].astype(BF16), w_out[l].astype(BF16)
        dl = diff_lambda[l].astype(F32)
        lam = jnp.exp(jnp.sum(dl[0] * dl[1])) - jnp.exp(jnp.sum(dl[2] * dl[3])) + lambda_init
        log_gamma = -jnp.exp(ret_decay[l].astype(F32))
        tab_f, tab_b = _decay_tables(log_gamma[0], False), _decay_tables(log_gamma[1], True)

        x = _ffn(x, mx, gain, *f1, row0=0, gain_row=0)
        xc = _ffn(xc, mc, gain, *f1, row0=0, gain_row=0)

        mq_t, mk, mvt, dq, dk, dvt, rq, rk, rv, rg = _proj(x, mx, gain, pw, x_tabs, n_keys=n + n_ctx)
        cmq_t, mk, mvt, cdq, dk, dvt, crq, crk, crv, crg = _proj(
            xc, mc, gain, pw, c_tabs, n_keys=n + n_ctx, key_off=n_x, into=(mk, mvt, dk, dvt))

        mla_t = _mla_attention(mq_t, mk, mvt, (0, n_all))
        dif_t = _diff_attention(dq, dk, dvt, (0, n_all), lam, diff_norm[l], lambda_init)

        oc_f, st_f = _retention(crq, crk, crv, zero_state, tab_f, reverse=False)
        ret_c, st_b = _retention(crq, crk, crv, zero_state, tab_b, reverse=True,
                                 post=(oc_f, crg, ret_norm[l]) if ctx_out else None)
        ox_f, _ = _retention(rq, rk, rv, st_f, tab_f, reverse=False)
        ret_x, _ = _retention(rq, rk, rv, st_b, tab_b, reverse=True, post=(ox_f, rg, ret_norm[l]))

        x = _merge(x, mx, gain, wg, wb, wo, mla_t, dif_t, ret_x)
        x = _ffn(x, mx, gain, *f2, row0=6, gain_row=2, final_gain=None if ctx_out else final_norm)
        if ctx_out:
            mla_c = _mla_attention(cmq_t, mk, mvt, (n_x, n_all))
            dif_c = _diff_attention(cdq, dk, dvt, (n_x, n_all), lam, diff_norm[l], lambda_init)
            xc = _merge(xc, mc, gain, wg, wb, wo, mla_c, dif_c, ret_c)
            xc = _ffn(xc, mc, gain, *f2, row0=6, gain_row=2)
    return x
```

```python
import functools
import math

import jax
import jax.numpy as jnp
from jax import lax
from jax.experimental import pallas as pl
from jax.experimental.pallas import tpu as pltpu

D_MODEL = 1024
DEPTH = 2
GRID_W = 64
N_BRANCH = 3
MLA_HEADS = 8
MLA_Q_RANK = 384
MLA_KV_RANK = 256
MLA_NOPE = 64
MLA_ROPE = 32
MLA_V = 64
DIFF_HEADS = 4
DIFF_QK = 64
DIFF_V = 128
RET_HEADS = 4
RET_K = 64
RET_V = 128
RET_CHUNK = 128
BRANCH_W = 512
D_FF = 2816
N_MOD = 9
ROPE_BASE = 10000.0
EPS = 1e-6

LANES = 128
SUBLANES = 8
LOG2E = 1.4426950408889634
FF_CHUNK = 256
KEY_CHUNK = 256
TOKEN_TILE = 512
MERGE_TILE = 1024
RET_TILE = 2048
MLA_Q_TILE = 256
DIFF_Q_TILE = 256
VMEM_LIMIT = 56 * 1024 * 1024

BF16 = jnp.bfloat16
F32 = jnp.float32
_NT = (((1,), (1,)), ((), ()))
_TN = (((0,), (0,)), ((), ()))


def _params(*sem):
    return pltpu.CompilerParams(dimension_semantics=sem, vmem_limit_bytes=VMEM_LIMIT)


def _const(shape):
    zeros = (0,) * len(shape)
    return pl.BlockSpec(shape, lambda *_: zeros, pipeline_mode=pl.Buffered(1))


def _rms(x, gain):
    return x * lax.rsqrt(jnp.mean(x * x, axis=-1, keepdims=True) + EPS) * gain


def _modulated(x, gain, shift, scale):
    return _rms(x, gain) * (1.0 + scale) + shift


def _dot(a, b):
    return jnp.dot(a, b, preferred_element_type=F32)


def _ada_kernel(c_ref, w_ref, b_ref, o_ref):
    c = c_ref[...]
    o_ref[0] = _dot(c * jax.nn.sigmoid(c), w_ref[0]) + b_ref[0]


def _ada(cs, ada_w, ada_b):
    nl, d, n = ada_w.shape
    r = cs.shape[0]
    tn = n // N_MOD
    return pl.pallas_call(
        _ada_kernel,
        grid=(nl, n // tn),
        in_specs=[pl.BlockSpec((r, d), lambda l, j: (0, 0)),
                  pl.BlockSpec((1, d, tn), lambda l, j: (l, 0, j)),
                  pl.BlockSpec((1, 1, tn), lambda l, j: (l, 0, j))],
        out_specs=pl.BlockSpec((1, r, tn), lambda l, j: (l, 0, j)),
        out_shape=jax.ShapeDtypeStruct((nl, r, n), F32),
        compiler_params=_params("parallel", "parallel"),
        name="ada",
    )(cs, ada_w, ada_b.reshape(nl, 1, n))


def _ffn_kernel(x_ref, mod_ref, gain_ref, w1_ref, w3_ref, w2_ref, *rest, row0, gain_row, final):
    if final:
        fg_ref, o_ref, acc_ref = rest
    else:
        o_ref, acc_ref = rest
    x = x_ref[0]
    h = _modulated(x, gain_ref[gain_row:gain_row + 1, :], mod_ref[0, row0:row0 + 1, :],
                   mod_ref[0, row0 + 1:row0 + 2, :]).astype(BF16)
    n_chunks = w1_ref.shape[0]
    up = (_dot(h, w1_ref[0]), _dot(h, w3_ref[0]))
    for c in range(n_chunks):
        a, b = up
        if c + 1 < n_chunks:
            up = (_dot(h, w1_ref[c + 1]), _dot(h, w3_ref[c + 1]))
        t = _dot((a * jax.nn.sigmoid(a) * b).astype(BF16), w2_ref[c])
        if c == 0:
            acc_ref[...] = t
        else:
            acc_ref[...] += t
    y = x + (mod_ref[0, row0 + 2:row0 + 3, :] * 0.5) * acc_ref[...]
    if final:
        y = _rms(y, fg_ref[...])
    o_ref[0] = y


def _ffn(x, mod, gain, w1, w3, w2, *, row0, gain_row, final_gain=None):
    b, s, d = x.shape
    tm = min(TOKEN_TILE, s)
    final = final_gain is not None
    in_specs = [pl.BlockSpec((1, tm, d), lambda bi, i: (bi, i, 0)),
                pl.BlockSpec((1, N_MOD, d), lambda bi, i: (bi, 0, 0)),
                _const(gain.shape), _const(w1.shape), _const(w3.shape), _const(w2.shape)]
    args = [x, mod, gain, w1, w3, w2]
    if final:
        in_specs.append(_const((1, d)))
        args.append(final_gain.reshape(1, d))
    return pl.pallas_call(
        functools.partial(_ffn_kernel, row0=row0, gain_row=gain_row, final=final),
        grid=(b, s // tm),
        in_specs=in_specs,
        out_specs=pl.BlockSpec((1, tm, d), lambda bi, i: (bi, i, 0)),
        out_shape=jax.ShapeDtypeStruct(x.shape, F32),
        scratch_shapes=[pltpu.VMEM((tm, d), F32)],
        compiler_params=_params("parallel", "parallel"),
        name="ffn",
    )(*args)


_O_QLAT = 0
_O_KVLAT = _O_QLAT + MLA_Q_RANK
_O_KR = _O_KVLAT + MLA_KV_RANK
_O_DQ = _O_KR + 2 * LANES
_O_RQ = _O_DQ + 2 * BRANCH_W
_O_RV = _O_RQ + 2 * RET_HEADS * RET_K
_W_MAIN = _O_RV + 2 * BRANCH_W


def _rotary(x, cos, sin_signed, half):
    lane = lax.broadcasted_iota(jnp.int32, x.shape, 1)
    first = (lane & (2 * half - 1)) < half
    partner = jnp.where(first, pltpu.roll(x, LANES - half, 1), pltpu.roll(x, half, 1))
    return x * cos + partner * sin_signed


_KEY_OUTPUTS = (1, 2, 4, 5)


def _proj_kernel(x_ref, mod_ref, gain_ref, wm_ref, wqa_ref, wka_ref, wvt_ref, wdvt_ref,
                 qn_ref, kvn_ref, cm_ref, sm_ref, cd_ref, sd_ref, cr_ref, sr_ref, cmt_ref, smt_ref, *rest):
    mqt_ref, mk_ref, mvt_ref, dq_ref, dk_ref, dvt_ref, rq_ref, rk_ref, rv_ref, rg_ref = rest[-10:]
    x = x_ref[0]
    h = _modulated(x, gain_ref[1:2, :], mod_ref[0, 3:4, :], mod_ref[0, 4:5, :]).astype(BF16)

    def seg(lo, width):
        return _dot(h, wm_ref[:, lo:lo + width])

    hk = RET_HEADS * RET_K
    qlat_raw, kvlat_raw = seg(_O_QLAT, MLA_Q_RANK), seg(_O_KVLAT, MLA_KV_RANK)
    kr = seg(_O_KR, 2 * LANES)
    dqk = seg(_O_DQ, 2 * BRANCH_W)
    rqk = seg(_O_RQ, 2 * hk)
    rv_ref[0] = seg(_O_RV, BRANCH_W).astype(BF16)
    rg_ref[0] = seg(_O_RV + BRANCH_W, BRANCH_W).astype(BF16)
    dvt_ref[0, 0] = lax.dot_general(wdvt_ref[...], h, _NT, preferred_element_type=F32).astype(BF16)

    cd, sd = cd_ref[...], sd_ref[...]
    d_scale = DIFF_QK ** -0.5 * LOG2E
    for j in range(BRANCH_W // LANES):
        sl = slice(j * LANES, (j + 1) * LANES)
        sk = slice(BRANCH_W + j * LANES, BRANCH_W + (j + 1) * LANES)
        dq_ref[0, :, sl] = (_rotary(dqk[:, sl], cd, sd, DIFF_QK // 4) * d_scale).astype(BF16)
        dk_ref[0, :, sl] = _rotary(dqk[:, sk], cd, sd, DIFF_QK // 4).astype(BF16)

    cr, sr_t = cr_ref[...], sr_ref[...]
    for j in range(hk // LANES):
        sl = slice(j * LANES, (j + 1) * LANES)
        sk = slice(hk + j * LANES, hk + (j + 1) * LANES)
        rq_ref[0, :, sl] = _rotary(rqk[:, sl], cr, sr_t, RET_K // 2).astype(BF16)
        rk_ref[0, :, sl] = (_rotary(rqk[:, sk], cr, sr_t, RET_K // 2) * RET_K ** -0.5).astype(BF16)

    qlat = _rms(qlat_raw, qn_ref[...]).astype(BF16)
    kvlat = _rms(kvlat_raw, kvn_ref[...]).astype(BF16)
    cm, sm = cm_ref[...], sm_ref[...]
    cmt, smt = cmt_ref[...], smt_ref[...]
    qat = lax.dot_general(wqa_ref[...], qlat, _NT, preferred_element_type=F32)
    ka = _dot(kvlat, wka_ref[...])
    mvt_ref[0, 0] = lax.dot_general(wvt_ref[...], kvlat, _NT, preferred_element_type=F32).astype(BF16)
    krr = kr[:, :LANES] * cm + kr[:, LANES:] * sm
    q_scale = (MLA_NOPE + MLA_ROPE) ** -0.5 * LOG2E
    half = MLA_ROPE // 4
    for hd in range(MLA_HEADS):
        sl = slice(hd * LANES, (hd + 1) * LANES)
        q = qat[sl, :]
        swapped = [q[MLA_NOPE + (i ^ 1) * half:MLA_NOPE + ((i ^ 1) + 1) * half] for i in range(MLA_ROPE // half)]
        partner = jnp.concatenate([q[:MLA_NOPE]] + swapped + [q[MLA_NOPE + MLA_ROPE:]], axis=0)
        mqt_ref[0, sl, :] = ((q * cmt + partner * smt) * q_scale).astype(BF16)
        mk_ref[0, :, sl] = (ka[:, sl] + krr).astype(BF16)


def _proj(x, mod, gain, pw, tables, *, n_keys, key_off=0, into=None):
    b, s, d = x.shape
    tm = KEY_CHUNK
    nt = s // tm
    tok = lambda w, off=0: pl.BlockSpec((1, tm, w), lambda bi, i: (bi, i + off, 0))
    slab = pl.BlockSpec((1, 1, BRANCH_W, tm), lambda bi, i: (bi, i + key_off, 0, 0))
    tab = pl.BlockSpec((tm, LANES), lambda bi, i: (i, 0))
    tab_t = pl.BlockSpec((LANES, tm), lambda bi, i: (0, i))
    weights = [pw["main"], pw["qa"], pw["ka"], pw["vt"], pw["dvt"], pw["qn"], pw["kvn"]]
    hk = RET_HEADS * RET_K
    widths = [MLA_HEADS * LANES, MLA_HEADS * LANES, None, BRANCH_W, BRANCH_W, None, hk, hk, BRANCH_W, BRANCH_W]
    out_specs, out_shape = [], []
    for pos, w in enumerate(widths):
        if pos == 0:
            out_specs.append(pl.BlockSpec((1, w, tm), lambda bi, i: (bi, 0, i)))
            out_shape.append(jax.ShapeDtypeStruct((b, w, s), BF16))
        elif w is None:
            out_specs.append(slab)
            out_shape.append(jax.ShapeDtypeStruct((b, n_keys // tm, BRANCH_W, tm), BF16))
        elif pos in _KEY_OUTPUTS:
            out_specs.append(tok(w, key_off))
            out_shape.append(jax.ShapeDtypeStruct((b, n_keys, w), BF16))
        else:
            out_specs.append(tok(w))
            out_shape.append(jax.ShapeDtypeStruct((b, s, w), BF16))
    in_specs = [pl.BlockSpec((1, tm, d), lambda bi, i: (bi, i, 0)),
                pl.BlockSpec((1, N_MOD, d), lambda bi, i: (bi, 0, 0)),
                _const(gain.shape)] + [_const(w.shape) for w in weights] + [tab] * 6 + [tab_t] * 2
    args = [x, mod, gain, *weights, *tables]
    aliases = {}
    if into is not None:
        for buf, pos in zip(into, _KEY_OUTPUTS):
            aliases[len(args)] = pos
            in_specs.append(pl.BlockSpec(memory_space=pl.ANY))
            args.append(buf)
    return pl.pallas_call(
        _proj_kernel,
        grid=(b, nt),
        in_specs=in_specs,
        out_specs=out_specs,
        out_shape=out_shape,
        input_output_aliases=aliases,
        compiler_params=_params("parallel", "parallel"),
        name="proj",
    )(*args)


SUM_ROWS = 16


def _slabs_per_chunk(n_slabs):
    return next(p for p in (3, 2, 1) if n_slabs % p == 0)


def _attend(tiles, upcoming, first, k_ref, vt_ref, s_ref, mx_ref, dv, *, q_transposed):
    n = tiles[0][0][0].shape[1 if q_transposed else 0]
    n_slabs = vt_ref.shape[1]
    per = _slabs_per_chunk(n_slabs)
    tk = per * KEY_CHUNK
    n_chunks = n_slabs // per
    ones = jnp.ones((SUM_ROWS, KEY_CHUNK), BF16)

    def score(chains, c, buf, g):
        q_rhs, lanes, _ = chains[g]
        start = c * tk if isinstance(c, int) else pl.multiple_of(c * tk, KEY_CHUNK)
        k = k_ref[0, pl.ds(start, tk), lanes]
        s = _dot(k, q_rhs) if q_transposed else lax.dot_general(k, q_rhs, _NT, preferred_element_type=F32)
        s_ref[buf, g] = s
        for j in range(per):
            part = s[j * KEY_CHUNK:(j + 1) * KEY_CHUNK]
            mx_ref[buf, g, j] = jnp.max(part.reshape(KEY_CHUNK // SUBLANES, SUBLANES, n), axis=0)

    def consume(chains, c, buf, carry, following):
        out = []
        for g, ((_, _, rows), (m, acc)) in enumerate(zip(chains, carry)):
            if following is not None:
                score(following[0], following[1], 1 - buf, g)
            mx = mx_ref[buf, g, 0]
            for j in range(1, per):
                mx = jnp.maximum(mx, mx_ref[buf, g, j])
            m_new = jnp.maximum(m, jnp.max(mx, axis=0, keepdims=True))
            alpha = jnp.exp2(m - m_new)
            pb = jnp.exp2(s_ref[buf, g] - m_new).astype(BF16)
            pv = None
            for j in range(per):
                vt = jnp.concatenate([vt_ref[0, c * per + j, rows, :], ones], axis=0)
                t = _dot(vt, pb[j * KEY_CHUNK:(j + 1) * KEY_CHUNK, :])
                pv = t if pv is None else pv + t
            out.append((m_new, alpha * acc + pv))
        return tuple(out)

    def prologue():
        for g in range(len(tiles[0])):
            score(tiles[0], 0, 0, g)

    if first is None:
        prologue()
    else:
        pl.when(first)(prologue)

    results = []
    start = 0
    for t, chains in enumerate(tiles):
        following = tiles[t + 1] if t + 1 < len(tiles) else upcoming

        def step(c, buf, carry, chains=chains):
            return consume(chains, c, buf, carry, (chains, c + 1))

        carry = tuple((jnp.full((1, n), -jnp.inf, F32), jnp.zeros((dv + SUM_ROWS, n), F32)) for _ in chains)
        n_pairs = (n_chunks - 1) // 2
        if n_pairs > 0:
            carry = lax.fori_loop(
                0, n_pairs, lambda j, cr, s0=start, step=step: step(2 * j + 1, 1 - s0, step(2 * j, s0, cr)), carry)
        if (n_chunks - 1) % 2:
            carry = step(n_chunks - 2, (start + n_chunks - 2) % 2, carry)
        last = (start + n_chunks - 1) % 2
        carry = consume(chains, n_chunks - 1, last, carry, None if following is None else (following, 0))
        results.append([(acc[:dv], acc[dv:dv + 1]) for _, acc in carry])
        start = 1 - last
    assert upcoming is None or start == 0
    return results


def _tiles_per_step(n_tiles, n_slabs):
    n_chunks = n_slabs // _slabs_per_chunk(n_slabs)
    if n_tiles % 2 == 0:
        return 2
    assert n_tiles == 1 or n_chunks % 2 == 0
    return 1


def _kv_specs(k_width, v_rows, key_range, buffers):
    lo, hi = key_range
    n_slabs = hi - lo
    assert lo % n_slabs == 0
    blk = lo // n_slabs
    return [pl.BlockSpec((1, n_slabs * KEY_CHUNK, k_width), lambda bi, i: (bi, blk, 0),
                         pipeline_mode=pl.Buffered(buffers)),
            pl.BlockSpec((1, n_slabs, v_rows, KEY_CHUNK), lambda bi, i: (bi, blk, 0, 0),
                         pipeline_mode=pl.Buffered(buffers))]


def _score_scratch(n_chains, n_slabs, n):
    per = _slabs_per_chunk(n_slabs)
    return [pltpu.VMEM((2, n_chains, per * KEY_CHUNK, n), F32),
            pltpu.VMEM((2, n_chains, per, SUBLANES, n), F32)]


def _mla_kernel(q_ref, qn_ref, k_ref, vt_ref, o_ref, s_ref, mx_ref, *, tq, chained):
    def chains_of(ref, t):
        return [(ref[0, g * LANES:(g + 1) * LANES, t * tq:(t + 1) * tq], slice(g * LANES, (g + 1) * LANES),
                 slice(g * MLA_V, (g + 1) * MLA_V)) for g in range(MLA_HEADS)]

    tiles = [chains_of(q_ref, t) for t in range(q_ref.shape[2] // tq)]
    res = _attend(tiles, chains_of(qn_ref, 0) if chained else None,
                  pl.program_id(1) == 0 if chained else None,
                  k_ref, vt_ref, s_ref, mx_ref, MLA_V, q_transposed=True)
    for t, (chains, out) in enumerate(zip(tiles, res)):
        for (_, _, rows), (acc, l) in zip(chains, out):
            o_ref[0, rows, t * tq:(t + 1) * tq] = (acc / l).astype(BF16)


def _mla_attention(q, k, vt, key_range):
    b, w, s = q.shape
    tq = min(MLA_Q_TILE, s)
    n_tiles = s // tq
    n_slabs = key_range[1] - key_range[0]
    per_step = _tiles_per_step(n_tiles, n_slabs)
    n_steps = n_tiles // per_step
    return pl.pallas_call(
        functools.partial(_mla_kernel, tq=tq, chained=n_steps > 1),
        grid=(b, n_steps),
        in_specs=[pl.BlockSpec((1, w, per_step * tq), lambda bi, i: (bi, 0, i)),
                  pl.BlockSpec((1, w, tq), lambda bi, i: (bi, 0, jnp.minimum((i + 1) * per_step, n_tiles - 1)))]
                 + _kv_specs(w, MLA_HEADS * MLA_V, key_range, 1),
        out_specs=pl.BlockSpec((1, MLA_HEADS * MLA_V, per_step * tq), lambda bi, i: (bi, 0, i)),
        out_shape=jax.ShapeDtypeStruct((b, MLA_HEADS * MLA_V, s), BF16),
        scratch_shapes=_score_scratch(MLA_HEADS, n_slabs, tq),
        compiler_params=_params("parallel", "arbitrary"),
        name="mla_attn",
    )(q, q, k, vt)


def _diff_kernel(lam_ref, q_ref, qn_ref, k_ref, vt_ref, gain_ref, o_ref, s_ref, mx_ref, *, tq, chained, out_scale):
    first_half = lax.broadcasted_iota(jnp.int32, (tq, LANES), 1) < DIFF_QK
    zero = jnp.zeros((tq, LANES), BF16)

    def chains_of(ref, t):
        chains = []
        for hd in range(DIFF_HEADS):
            lanes = slice(hd * LANES, (hd + 1) * LANES)
            q = ref[0, t * tq:(t + 1) * tq, lanes]
            q_rhs = jnp.concatenate([jnp.where(first_half, q, zero), jnp.where(first_half, zero, q)], axis=0)
            chains.append((q_rhs, lanes, slice(hd * DIFF_V, (hd + 1) * DIFF_V)))
        return chains

    tiles = [chains_of(q_ref, t) for t in range(q_ref.shape[1] // tq)]
    res = _attend(tiles, chains_of(qn_ref, 0) if chained else None,
                  pl.program_id(1) == 0 if chained else None,
                  k_ref, vt_ref, s_ref, mx_ref, DIFF_V, q_transposed=False)
    for t, (chains, out) in enumerate(zip(tiles, res)):
        for (_, _, rows), (acc, l) in zip(chains, out):
            o = acc / l
            o = o[:, :tq] - lam_ref[0] * o[:, tq:]
            o = o * lax.rsqrt(jnp.mean(o * o, axis=0, keepdims=True) + EPS) * gain_ref[...]
            o_ref[0, rows, t * tq:(t + 1) * tq] = (o * out_scale).astype(BF16)


def _diff_attention(q, k, vt, key_range, lam, gain, lambda_init):
    b, s, w = q.shape
    tq = min(DIFF_Q_TILE, s)
    n_tiles = s // tq
    n_slabs = key_range[1] - key_range[0]
    per_step = _tiles_per_step(n_tiles, n_slabs)
    n_steps = n_tiles // per_step
    return pl.pallas_call(
        functools.partial(_diff_kernel, tq=tq, chained=n_steps > 1, out_scale=1.0 - lambda_init),
        grid=(b, n_steps),
        in_specs=[pl.BlockSpec(memory_space=pltpu.SMEM),
                  pl.BlockSpec((1, per_step * tq, w), lambda bi, i: (bi, i, 0)),
                  pl.BlockSpec((1, tq, w), lambda bi, i: (bi, jnp.minimum((i + 1) * per_step, n_tiles - 1), 0))]
                 + _kv_specs(w, DIFF_HEADS * DIFF_V, key_range, 2)
                 + [pl.BlockSpec((DIFF_V, 1), lambda bi, i: (0, 0))],
        out_specs=pl.BlockSpec((1, DIFF_HEADS * DIFF_V, per_step * tq), lambda bi, i: (bi, 0, i)),
        out_shape=jax.ShapeDtypeStruct((b, DIFF_HEADS * DIFF_V, s), BF16),
        scratch_shapes=_score_scratch(DIFF_HEADS, n_slabs, 2 * tq),
        compiler_params=_params("parallel", "arbitrary"),
        name="diff_attn",
    )(lam.reshape(1), q, q, k, vt, gain.reshape(DIFF_V, 1))


def _ret_kernel(q_ref, k_ref, v_ref, s0_ref, intra_ref, qdec_ref, kdec_ref, cdec_ref, *rest, reverse, post):
    if post:
        of_ref, g_ref, gain_ref, o_ref, sout_ref, state = rest
    else:
        o_ref, sout_ref, state = rest
    i = pl.program_id(1)

    @pl.when(i == 0)
    def _():
        state[...] = s0_ref[0]

    n_chunks = q_ref.shape[1] // RET_CHUNK
    lane = lax.broadcasted_iota(jnp.int32, (RET_CHUNK, RET_HEADS * RET_K), 1)
    masks = [(lane >= hd * RET_K) & (lane < (hd + 1) * RET_K) for hd in range(RET_HEADS)]
    order = [n_chunks - 1 - cc if reverse else cc for cc in range(n_chunks)]
    heads = range(RET_HEADS)

    def head_v(c, hd):
        return v_ref[0, c * RET_CHUNK:(c + 1) * RET_CHUNK, hd * RET_V:(hd + 1) * RET_V]

    intra, deltas = {}, {}
    for c in order:
        rows = slice(c * RET_CHUNK, (c + 1) * RET_CHUNK)
        qc, kc = q_ref[0, rows, :], k_ref[0, rows, :]
        kdt = (kc.astype(F32) * kdec_ref[...]).T.astype(BF16)
        for hd in heads:
            a = lax.dot_general(jnp.where(masks[hd], qc, jnp.zeros_like(qc)), kc, _NT,
                                preferred_element_type=F32)
            intra[c, hd] = (a * intra_ref[hd]).astype(BF16)
            deltas[c, hd] = _dot(kdt[hd * RET_K:(hd + 1) * RET_K, :], head_v(c, hd))
    st = state[...]
    entering = {}
    for c in order:
        entering[c] = st.astype(BF16)
        st = st * cdec_ref[...] + jnp.concatenate([deltas[c, hd] for hd in heads], axis=0)
    state[...] = st
    for c in order:
        rows = slice(c * RET_CHUNK, (c + 1) * RET_CHUNK)
        qd = (q_ref[0, rows, :].astype(F32) * qdec_ref[...]).astype(BF16)
        outs = [_dot(intra[c, hd], head_v(c, hd))
                + _dot(jnp.where(masks[hd], qd, jnp.zeros_like(qd)), entering[c]) for hd in heads]
        if post:
            of = of_ref[0, rows, :]
            g = g_ref[0, rows, :].astype(F32)
            for hd in range(RET_HEADS):
                sl = slice(hd * RET_V, (hd + 1) * RET_V)
                outs[hd] = _rms(outs[hd] + of[:, sl], gain_ref[...]) * (g[:, sl] * jax.nn.sigmoid(g[:, sl]))
        o_ref[0, rows, :] = jnp.concatenate(outs, axis=1).astype(o_ref.dtype)

    @pl.when(i == pl.num_programs(1) - 1)
    def _():
        sout_ref[0] = state[...]


def _retention(q, k, v, s0, tabs, *, reverse, post=None):
    b, n, _ = q.shape
    t = min(RET_TILE, n)
    nt = n // t
    hk, hv = RET_HEADS * RET_K, RET_HEADS * RET_V
    tile = (lambda bi, i: (bi, nt - 1 - i, 0)) if reverse else (lambda bi, i: (bi, i, 0))
    st_spec = pl.BlockSpec((1, hk, RET_V), lambda bi, i: (bi, 0, 0))
    in_specs = [pl.BlockSpec((1, t, hk), tile), pl.BlockSpec((1, t, hk), tile), pl.BlockSpec((1, t, hv), tile),
                st_spec] + [_const(a.shape) for a in tabs]
    args = [q, k, v, s0, *tabs]
    if post is not None:
        o_f, g, gain = post
        in_specs += [pl.BlockSpec((1, t, hv), tile), pl.BlockSpec((1, t, hv), tile), _const((1, RET_V))]
        args += [o_f, g, gain.reshape(1, RET_V)]
    return pl.pallas_call(
        functools.partial(_ret_kernel, reverse=reverse, post=post is not None),
        grid=(b, nt),
        in_specs=in_specs,
        out_specs=[pl.BlockSpec((1, t, hv), tile), st_spec],
        out_shape=[jax.ShapeDtypeStruct((b, n, hv), F32 if post is None else BF16),
                   jax.ShapeDtypeStruct((b, hk, RET_V), F32)],
        scratch_shapes=[pltpu.VMEM((hk, RET_V), F32)],
        compiler_params=_params("parallel", "arbitrary"),
        name="ret_bwd" if reverse else "ret_fwd",
    )(*args)


def _decay_tables(log_gamma, reverse):
    idx = jnp.arange(RET_CHUNK, dtype=F32)
    lg = log_gamma.astype(F32)[:, None]
    rel = idx[:, None] - idx[None, :]
    if reverse:
        rel = -rel
        q_pow, k_pow = RET_CHUNK - idx, idx
    else:
        q_pow, k_pow = idx + 1.0, RET_CHUNK - 1.0 - idx
    intra = jnp.where(rel >= 0, jnp.exp(lg[:, :, None] * jnp.maximum(rel, 0.0)), 0.0)
    qdec = jnp.repeat(jnp.exp(lg * q_pow).T, RET_K, axis=1)
    kdec = jnp.repeat(jnp.exp(lg * k_pow).T, RET_K, axis=1)
    cdec = jnp.broadcast_to(jnp.repeat(jnp.exp(lg * RET_CHUNK), RET_K, axis=0), (RET_HEADS * RET_K, RET_V))
    return intra, qdec, kdec, cdec


def _merge_kernel(x_ref, mod_ref, gain_ref, wg_ref, wb_ref, wo_ref, mla_ref, dif_ref, ret_ref, o_ref):
    tm, d = x_ref.shape[1], x_ref.shape[2]
    halves = [slice(0, tm // 2), slice(tm // 2, tm)]

    def gated(rows):
        x = x_ref[0, rows, :]
        h = _modulated(x, gain_ref[1:2, :], mod_ref[0, 3:4, :], mod_ref[0, 4:5, :]).astype(BF16)
        branches = (lax.dot_general(mla_ref[0, :, rows], wb_ref[0], _TN, preferred_element_type=F32),
                    lax.dot_general(dif_ref[0, :, rows], wb_ref[1], _TN, preferred_element_type=F32),
                    _dot(ret_ref[0, rows, :], wb_ref[2]))
        y = None
        for j, br in enumerate(branches):
            t = jax.nn.sigmoid(_dot(h, wg_ref[:, j * d:(j + 1) * d])) * br
            y = t if y is None else y + t
        return x, y.astype(BF16)

    mixed = [gated(rows) for rows in halves]
    for rows, (x, y) in zip(halves, mixed):
        o_ref[0, rows, :] = x + mod_ref[0, 5:6, :] * _dot(y, wo_ref[...])


def _merge(x, mod, gain, wg, wb, wo, mla_t, dif_t, ret):
    b, s, d = x.shape
    tm = min(MERGE_TILE, s)
    tok = lambda w: pl.BlockSpec((1, tm, w), lambda bi, i: (bi, i, 0))
    feat = pl.BlockSpec((1, BRANCH_W, tm), lambda bi, i: (bi, 0, i))
    return pl.pallas_call(
        _merge_kernel,
        grid=(b, s // tm),
        in_specs=[tok(d), pl.BlockSpec((1, N_MOD, d), lambda bi, i: (bi, 0, 0)), _const(gain.shape),
                  _const(wg.shape), _const(wb.shape), _const(wo.shape), feat, feat, tok(BRANCH_W)],
        out_specs=tok(d),
        out_shape=jax.ShapeDtypeStruct(x.shape, F32),
        compiler_params=_params("parallel", "parallel"),
        name="merge",
    )(x, mod, gain, wg, wb, wo, mla_t, dif_t, ret)


def _rot_cols(w, group):
    r, n = w.shape
    w = w.reshape(r, n // group, 2, group // 2)
    return jnp.stack([-w[:, :, 1], w[:, :, 0]], axis=2).reshape(r, n)


def _head_slots(w, n_heads, width):
    r = w.shape[0]
    w = w.reshape(r, n_heads, width)
    return jnp.pad(w, ((0, 0), (0, 0), (0, LANES - width))).reshape(r, n_heads * LANES)


def _proj_weights(w_in, w_qb, w_kvb, q_norm, kv_norm):
    sizes = (MLA_Q_RANK, MLA_KV_RANK, MLA_ROPE, 2 * DIFF_HEADS * DIFF_QK, 2 * DIFF_HEADS * DIFF_QK,
             DIFF_HEADS * DIFF_V, RET_HEADS * RET_K, RET_HEADS * RET_K, RET_HEADS * RET_V, RET_HEADS * RET_V)
    offs = [0]
    for sz in sizes:
        offs.append(offs[-1] + sz)
    p = [w_in[:, offs[i]:offs[i + 1]] for i in range(len(sizes))]
    gates = w_in[:, offs[-1]:]
    d = w_in.shape[0]
    lead = jnp.zeros((d, MLA_NOPE), F32)
    tail = jnp.zeros((d, LANES - MLA_NOPE - MLA_ROPE), F32)
    main = jnp.concatenate([
        p[0], p[1],
        lead, p[2], tail, lead, _rot_cols(p[2], MLA_ROPE // 2), tail,
        p[3], p[4], p[6], p[7], p[8], p[9]], axis=1)
    assert main.shape[1] == _W_MAIN
    hd = MLA_NOPE + MLA_ROPE
    kvb = w_kvb.reshape(MLA_KV_RANK, MLA_HEADS, MLA_NOPE + MLA_V)
    return {
        "main": main.astype(BF16),
        "qa": _head_slots(w_qb, MLA_HEADS, hd).T.astype(BF16),
        "ka": _head_slots(kvb[:, :, :MLA_NOPE].reshape(MLA_KV_RANK, -1), MLA_HEADS, MLA_NOPE).astype(BF16),
        "vt": kvb[:, :, MLA_NOPE:].reshape(MLA_KV_RANK, -1).T.astype(BF16),
        "dvt": p[5].T.astype(BF16),
        "qn": q_norm.reshape(1, -1),
        "kvn": kv_norm.reshape(1, -1),
    }, gates.astype(BF16)


def _angles(pos, dim):
    inv = ROPE_BASE ** (-jnp.arange(0, dim, 2, dtype=F32) / dim)
    ang = pos.astype(F32)[:, None] * inv[None, :]
    return jnp.cos(ang), jnp.sin(ang)


def _rope_tables(n):
    t = jnp.arange(n)
    row = jnp.repeat(jnp.arange(n // GRID_W), GRID_W)
    col = t - row * GRID_W

    def axial(dim):
        cr, sr = _angles(row, dim // 2)
        cc, sc = _angles(col, dim // 2)
        return jnp.concatenate([cr, cr, cc, cc], axis=1), jnp.concatenate([sr, sr, sc, sc], axis=1)

    cm, sm = axial(MLA_ROPE)
    ones = jnp.ones((n, MLA_NOPE), F32)
    pad = jnp.zeros((n, LANES - MLA_NOPE - MLA_ROPE), F32)
    cm = jnp.concatenate([ones, cm, pad], axis=1)
    sm = jnp.concatenate([0 * ones, sm, pad], axis=1)
    cd, sd = axial(DIFF_QK)
    cs, ss = _angles(t, RET_K)
    sd = sd * jnp.repeat(jnp.array([-1.0, 1.0, -1.0, 1.0], F32), DIFF_QK // 4)
    mla_sign = jnp.concatenate([jnp.ones((MLA_NOPE,), F32),
                                jnp.repeat(jnp.array([-1.0, 1.0, -1.0, 1.0], F32), MLA_ROPE // 4),
                                jnp.ones((LANES - MLA_NOPE - MLA_ROPE,), F32)])
    return (cm, sm, jnp.tile(cd, (1, 2)), jnp.tile(sd, (1, 2)),
            jnp.tile(cs, (1, 4)), jnp.tile(jnp.concatenate([-ss, ss], axis=1), (1, 2)),
            cm.T, (sm * mla_sign).T)


def _identity_tables(n):
    one, zero = jnp.ones((n, LANES), F32), jnp.zeros((n, LANES), F32)
    return one, zero, one, zero, one, zero, one.T, zero.T


def _chunked(w, axis):
    if axis == 1:
        r = w.shape[0]
        return w.reshape(r, D_FF // FF_CHUNK, FF_CHUNK).transpose(1, 0, 2).astype(BF16)
    return w.reshape(D_FF // FF_CHUNK, FF_CHUNK, w.shape[1]).astype(BF16)


def kernel(x, c, ctx, c_ctx, ada_w, ada_b, norm_gain, ffn1_w1, ffn1_w3, ffn1_w2, ffn2_w1, ffn2_w3, ffn2_w2,
           w_in, mla_q_norm, mla_w_qb, mla_kv_norm, mla_w_kvb, diff_lambda, diff_norm, ret_decay, ret_norm,
           w_branch, w_out, final_norm):
    b, n, d = x.shape
    n_ctx = ctx.shape[1]
    pad_rows = -(b + 1) % 8
    cs = jnp.concatenate([c, c_ctx[None, :], jnp.zeros((pad_rows, d), F32)], axis=0)
    mods = _ada(cs, ada_w, ada_b)
    x_tabs, c_tabs = _rope_tables(n), _identity_tables(n_ctx)
    zero_state = jnp.zeros((b, RET_HEADS * RET_K, RET_V), F32)
    n_x, n_all = n // KEY_CHUNK, (n + n_ctx) // KEY_CHUNK
    xc = ctx
    for l in range(DEPTH):
        lambda_init = 0.8 - 0.6 * math.exp(-0.3 * l)
        ctx_out = l < DEPTH - 1
        mx = mods[l, :b].reshape(b, N_MOD, d)
        mc = jnp.broadcast_to(mods[l, b].reshape(1, N_MOD, d), (b, N_MOD, d))
        gain = norm_gain[l]
        f1 = (_chunked(ffn1_w1[l], 1), _chunked(ffn1_w3[l], 1), _chunked(ffn1_w2[l], 0))
        f2 = (_chunked(ffn2_w1[l], 1), _chunked(ffn2_w3[l], 1), _chunked(ffn2_w2[l], 0))
        pw, wg = _proj_weights(w_in[l], mla_w_qb[l], mla_w_kvb[l], mla_q_norm[l], mla_kv_norm[l])
        wb, wo = w_branch[l].astype(BF16), w_out[l].astype(BF16)
        dl = diff_lambda[l].astype(F32)
        lam = jnp.exp(jnp.sum(dl[0] * dl[1])) - jnp.exp(jnp.sum(dl[2] * dl[3])) + lambda_init
        log_gamma = -jnp.exp(ret_decay[l].astype(F32))
        tab_f, tab_b = _decay_tables(log_gamma[0], False), _decay_tables(log_gamma[1], True)

        x = _ffn(x, mx, gain, *f1, row0=0, gain_row=0)
        xc = _ffn(xc, mc, gain, *f1, row0=0, gain_row=0)

        mq_t, mk, mvt, dq, dk, dvt, rq, rk, rv, rg = _proj(x, mx, gain, pw, x_tabs, n_keys=n + n_ctx)
        cmq_t, mk, mvt, cdq, dk, dvt, crq, crk, crv, crg = _proj(
            xc, mc, gain, pw, c_tabs, n_keys=n + n_ctx, key_off=n_x, into=(mk, mvt, dk, dvt))

        mla_t = _mla_attention(mq_t, mk, mvt, (0, n_all))
        dif_t = _diff_attention(dq, dk, dvt, (0, n_all), lam, diff_norm[l], lambda_init)

        oc_f, st_f = _retention(crq, crk, crv, zero_state, tab_f, reverse=False)
        ret_c, st_b = _retention(crq, crk, crv, zero_state, tab_b, reverse=True,
                                 post=(oc_f, crg, ret_norm[l]) if ctx_out else None)
        ox_f, _ = _retention(rq, rk, rv, st_f, tab_f, reverse=False)
        ret_x, _ = _retention(rq, rk, rv, st_b, tab_b, reverse=True, post=(ox_f, rg, ret_norm[l]))

        x = _merge(x, mx, gain, wg, wb, wo, mla_t, dif_t, ret_x)
        x = _ffn(x, mx, gain, *f2, row0=6, gain_row=2, final_gain=None if ctx_out else final_norm)
        if ctx_out:
            mla_c = _mla_attention(cmq_t, mk, mvt, (n_x, n_all))
            dif_c = _diff_attention(cdq, dk, dvt, (n_x, n_all), lam, diff_norm[l], lambda_init)
            xc = _merge(xc, mc, gain, wg, wb, wo, mla_c, dif_c, ret_c)
            xc = _ffn(xc, mc, gain, *f2, row0=6, gain_row=2)
    return x
```

```python
import functools
import math

import jax
import jax.numpy as jnp
from jax import lax
from jax.experimental import pallas as pl
from jax.experimental.pallas import tpu as pltpu

DEPTH = 2
GRID_W = 64
MLA_HEADS = 8
MLA_Q_RANK = 384
MLA_KV_RANK = 256
MLA_NOPE = 64
MLA_ROPE = 32
MLA_V = 64
DIFF_HEADS = 4
DIFF_QK = 64
DIFF_V = 128
RET_HEADS = 4
RET_K = 64
RET_V = 128
RET_CHUNK = 128
BRANCH_W = 512
D_FF = 2816
N_MOD = 9
ROPE_BASE = 10000.0
EPS = 1e-6

LANES = 128
SUBLANES = 8
LOG2E = 1.4426950408889634
FF_CHUNK = 256
KEY_CHUNK = 256
TOKEN_TILE = 512
MERGE_TILE = 1024
RET_TILE = 2048
MLA_Q_TILE = 256
DIFF_Q_TILE = 256
VMEM_LIMIT = 56 * 1024 * 1024

BF16 = jnp.bfloat16
F32 = jnp.float32
_NT = (((1,), (1,)), ((), ()))
_TN = (((0,), (0,)), ((), ()))


def _params(*sem):
    return pltpu.CompilerParams(dimension_semantics=sem, vmem_limit_bytes=VMEM_LIMIT)


def _const(shape):
    zeros = (0,) * len(shape)
    return pl.BlockSpec(shape, lambda *_: zeros, pipeline_mode=pl.Buffered(1))


def _rms(x, gain):
    return x * lax.rsqrt(jnp.mean(x * x, axis=-1, keepdims=True) + EPS) * gain


def _modulated(x, gain, shift, scale):
    return _rms(x, gain) * (1.0 + scale) + shift


def _dot(a, b):
    return jnp.dot(a, b, preferred_element_type=F32)


def _ada_kernel(c_ref, w_ref, b_ref, o_ref):
    c = c_ref[...]
    o_ref[0] = _dot(c * jax.nn.sigmoid(c), w_ref[0]) + b_ref[0]


def _ada(cs, ada_w, ada_b):
    nl, d, n = ada_w.shape
    r = cs.shape[0]
    tn = n // N_MOD
    return pl.pallas_call(
        _ada_kernel,
        grid=(nl, n // tn),
        in_specs=[pl.BlockSpec((r, d), lambda l, j: (0, 0)),
                  pl.BlockSpec((1, d, tn), lambda l, j: (l, 0, j)),
                  pl.BlockSpec((1, 1, tn), lambda l, j: (l, 0, j))],
        out_specs=pl.BlockSpec((1, r, tn), lambda l, j: (l, 0, j)),
        out_shape=jax.ShapeDtypeStruct((nl, r, n), F32),
        compiler_params=_params("parallel", "parallel"),
        name="ada",
    )(cs, ada_w, ada_b.reshape(nl, 1, n))


def _ffn_kernel(x_ref, mod_ref, gain_ref, w1_ref, w3_ref, w2_ref, *rest, row0, gain_row, final):
    if final:
        fg_ref, o_ref, acc_ref = rest
    else:
        o_ref, acc_ref = rest
    x = x_ref[0]
    h = _modulated(x, gain_ref[gain_row:gain_row + 1, :], mod_ref[0, row0:row0 + 1, :],
                   mod_ref[0, row0 + 1:row0 + 2, :]).astype(BF16)
    n_chunks = w1_ref.shape[0]
    up = (_dot(h, w1_ref[0]), _dot(h, w3_ref[0]))
    for c in range(n_chunks):
        a, b = up
        if c + 1 < n_chunks:
            up = (_dot(h, w1_ref[c + 1]), _dot(h, w3_ref[c + 1]))
        t = _dot((a * jax.nn.sigmoid(a) * b).astype(BF16), w2_ref[c])
        if c == 0:
            acc_ref[...] = t
        else:
            acc_ref[...] += t
    y = x + (mod_ref[0, row0 + 2:row0 + 3, :] * 0.5) * acc_ref[...]
    if final:
        y = _rms(y, fg_ref[...])
    o_ref[0] = y


def _ffn(x, mod, gain, w1, w3, w2, *, row0, gain_row, final_gain=None):
    b, s, d = x.shape
    tm = min(TOKEN_TILE, s)
    final = final_gain is not None
    in_specs = [pl.BlockSpec((1, tm, d), lambda bi, i: (bi, i, 0)),
                pl.BlockSpec((1, N_MOD, d), lambda bi, i: (bi, 0, 0)),
                _const(gain.shape), _const(w1.shape), _const(w3.shape), _const(w2.shape)]
    args = [x, mod, gain, w1, w3, w2]
    if final:
        in_specs.append(_const((1, d)))
        args.append(final_gain.reshape(1, d))
    return pl.pallas_call(
        functools.partial(_ffn_kernel, row0=row0, gain_row=gain_row, final=final),
        grid=(b, s // tm),
        in_specs=in_specs,
        out_specs=pl.BlockSpec((1, tm, d), lambda bi, i: (bi, i, 0)),
        out_shape=jax.ShapeDtypeStruct(x.shape, F32),
        scratch_shapes=[pltpu.VMEM((tm, d), F32)],
        compiler_params=_params("parallel", "parallel"),
        name="ffn",
    )(*args)


_O_QLAT = 0
_O_KVLAT = _O_QLAT + MLA_Q_RANK
_O_KR = _O_KVLAT + MLA_KV_RANK
_O_DQ = _O_KR + LANES
_O_RQ = _O_DQ + 2 * BRANCH_W
_O_RV = _O_RQ + 2 * RET_HEADS * RET_K
_W_MAIN = _O_RV + 2 * BRANCH_W


def _rotary(x, cos, sin_signed, half):
    lane = lax.broadcasted_iota(jnp.int32, x.shape, 1)
    first = (lane & (2 * half - 1)) < half
    partner = jnp.where(first, pltpu.roll(x, LANES - half, 1), pltpu.roll(x, half, 1))
    return x * cos + partner * sin_signed


_KEY_OUTPUTS = (1, 2, 4, 5)


def _proj_kernel(x_ref, mod_ref, gain_ref, wm_ref, wqa_ref, wka_ref, wvt_ref, wdvt_ref,
                 qn_ref, kvn_ref, cm_ref, sm_ref, cd_ref, sd_ref, cr_ref, sr_ref, cmt_ref, smt_ref, *rest):
    mqt_ref, mk_ref, mvt_ref, dq_ref, dk_ref, dvt_ref, rq_ref, rk_ref, rv_ref, rg_ref = rest[-10:]
    x = x_ref[0]
    h = _modulated(x, gain_ref[1:2, :], mod_ref[0, 3:4, :], mod_ref[0, 4:5, :]).astype(BF16)

    def seg(lo, width):
        return _dot(h, wm_ref[:, lo:lo + width])

    hk = RET_HEADS * RET_K
    qlat_raw, kvlat_raw = seg(_O_QLAT, MLA_Q_RANK), seg(_O_KVLAT, MLA_KV_RANK)
    kr = seg(_O_KR, LANES)
    dqk = seg(_O_DQ, 2 * BRANCH_W)
    rqk = seg(_O_RQ, 2 * hk)
    rv_ref[0] = seg(_O_RV, BRANCH_W).astype(BF16)
    rg_ref[0] = seg(_O_RV + BRANCH_W, BRANCH_W).astype(BF16)
    dvt_ref[0, 0] = lax.dot_general(wdvt_ref[...], h, _NT, preferred_element_type=F32).astype(BF16)

    cd, sd = cd_ref[...], sd_ref[...]
    d_scale = DIFF_QK ** -0.5 * LOG2E
    for j in range(BRANCH_W // LANES):
        sl = slice(j * LANES, (j + 1) * LANES)
        sk = slice(BRANCH_W + j * LANES, BRANCH_W + (j + 1) * LANES)
        dq_ref[0, :, sl] = (_rotary(dqk[:, sl], cd, sd, DIFF_QK // 4) * d_scale).astype(BF16)
        dk_ref[0, :, sl] = _rotary(dqk[:, sk], cd, sd, DIFF_QK // 4).astype(BF16)

    cr, sr_t = cr_ref[...], sr_ref[...]
    for j in range(hk // LANES):
        sl = slice(j * LANES, (j + 1) * LANES)
        sk = slice(hk + j * LANES, hk + (j + 1) * LANES)
        rq_ref[0, :, sl] = _rotary(rqk[:, sl], cr, sr_t, RET_K // 2).astype(BF16)
        rk_ref[0, :, sl] = (_rotary(rqk[:, sk], cr, sr_t, RET_K // 2) * RET_K ** -0.5).astype(BF16)

    qlat = _rms(qlat_raw, qn_ref[...]).astype(BF16)
    kvlat = _rms(kvlat_raw, kvn_ref[...]).astype(BF16)
    cm, sm = cm_ref[...], sm_ref[...]
    cmt, smt = cmt_ref[...], smt_ref[...]
    qat = lax.dot_general(wqa_ref[...], qlat, _NT, preferred_element_type=F32)
    ka = _dot(kvlat, wka_ref[...])
    mvt_ref[0, 0] = lax.dot_general(wvt_ref[...], kvlat, _NT, preferred_element_type=F32).astype(BF16)
    krr = _rotary(kr, cm, sm, MLA_ROPE // 4)
    q_scale = (MLA_NOPE + MLA_ROPE) ** -0.5 * LOG2E
    half = MLA_ROPE // 4
    for hd in range(MLA_HEADS):
        sl = slice(hd * LANES, (hd + 1) * LANES)
        q = qat[sl, :]
        swapped = [q[MLA_NOPE + (i ^ 1) * half:MLA_NOPE + ((i ^ 1) + 1) * half] for i in range(MLA_ROPE // half)]
        partner = jnp.concatenate([q[:MLA_NOPE]] + swapped + [q[MLA_NOPE + MLA_ROPE:]], axis=0)
        mqt_ref[0, sl, :] = ((q * cmt + partner * smt) * q_scale).astype(BF16)
        mk_ref[0, :, sl] = (ka[:, sl] + krr).astype(BF16)


def _proj(x, mod, gain, pw, tables, *, n_keys, key_off=0, into=None):
    b, s, d = x.shape
    tm = KEY_CHUNK
    nt = s // tm
    tok = lambda w, off=0: pl.BlockSpec((1, tm, w), lambda bi, i: (bi, i + off, 0))
    slab = pl.BlockSpec((1, 1, BRANCH_W, tm), lambda bi, i: (bi, i + key_off, 0, 0))
    tab = pl.BlockSpec((tm, LANES), lambda bi, i: (i, 0))
    tab_t = pl.BlockSpec((LANES, tm), lambda bi, i: (0, i))
    weights = [pw["main"], pw["qa"], pw["ka"], pw["vt"], pw["dvt"], pw["qn"], pw["kvn"]]
    hk = RET_HEADS * RET_K
    widths = [MLA_HEADS * LANES, MLA_HEADS * LANES, None, BRANCH_W, BRANCH_W, None, hk, hk, BRANCH_W, BRANCH_W]
    out_specs, out_shape = [], []
    for pos, w in enumerate(widths):
        if pos == 0:
            out_specs.append(pl.BlockSpec((1, w, tm), lambda bi, i: (bi, 0, i)))
            out_shape.append(jax.ShapeDtypeStruct((b, w, s), BF16))
        elif w is None:
            out_specs.append(slab)
            out_shape.append(jax.ShapeDtypeStruct((b, n_keys // tm, BRANCH_W, tm), BF16))
        elif pos in _KEY_OUTPUTS:
            out_specs.append(tok(w, key_off))
            out_shape.append(jax.ShapeDtypeStruct((b, n_keys, w), BF16))
        else:
            out_specs.append(tok(w))
            out_shape.append(jax.ShapeDtypeStruct((b, s, w), BF16))
    in_specs = [pl.BlockSpec((1, tm, d), lambda bi, i: (bi, i, 0)),
                pl.BlockSpec((1, N_MOD, d), lambda bi, i: (bi, 0, 0)),
                _const(gain.shape)] + [_const(w.shape) for w in weights] + [tab] * 6 + [tab_t] * 2
    args = [x, mod, gain, *weights, *tables]
    aliases = {}
    if into is not None:
        for buf, pos in zip(into, _KEY_OUTPUTS):
            aliases[len(args)] = pos
            in_specs.append(pl.BlockSpec(memory_space=pl.ANY))
            args.append(buf)
    return pl.pallas_call(
        _proj_kernel,
        grid=(b, nt),
        in_specs=in_specs,
        out_specs=out_specs,
        out_shape=out_shape,
        input_output_aliases=aliases,
        compiler_params=_params("parallel", "parallel"),
        name="proj",
    )(*args)


BF16_ROWS = 2 * SUBLANES
SUM_ROWS = BF16_ROWS


def _slabs_per_chunk(n_slabs):
    return next(p for p in (3, 2, 1) if n_slabs % p == 0)


def _attend(tiles, upcoming, first, k_ref, vt_ref, s_ref, mx_ref, dv, *, q_transposed):
    n = tiles[0][0][0].shape[1 if q_transposed else 0]
    n_slabs = vt_ref.shape[1]
    per = _slabs_per_chunk(n_slabs)
    tk = per * KEY_CHUNK
    n_chunks = n_slabs // per
    ones = jnp.ones((SUM_ROWS, KEY_CHUNK), BF16)

    def score(chains, c, buf, g):
        q_rhs, lanes, _ = chains[g]
        start = c * tk if isinstance(c, int) else pl.multiple_of(c * tk, KEY_CHUNK)
        k = k_ref[0, pl.ds(start, tk), lanes]
        s = _dot(k, q_rhs) if q_transposed else lax.dot_general(k, q_rhs, _NT, preferred_element_type=F32)
        s_ref[buf, g] = s
        for j in range(per):
            part = s[j * KEY_CHUNK:(j + 1) * KEY_CHUNK]
            mx_ref[buf, g, j] = jnp.max(part.reshape(KEY_CHUNK // SUBLANES, SUBLANES, n), axis=0)

    def consume(chains, c, buf, carry, following):
        out = []
        for g, ((_, _, rows), (m, acc)) in enumerate(zip(chains, carry)):
            if following is not None:
                score(following[0], following[1], 1 - buf, g)
            mx = mx_ref[buf, g, 0]
            for j in range(1, per):
                mx = jnp.maximum(mx, mx_ref[buf, g, j])
            m_new = jnp.maximum(m, jnp.max(mx, axis=0, keepdims=True))
            alpha = jnp.exp2(m - m_new)
            pb = jnp.exp2(s_ref[buf, g] - m_new).astype(BF16)
            pv = None
            for j in range(per):
                vt = jnp.concatenate([vt_ref[0, c * per + j, rows, :], ones], axis=0)
                t = _dot(vt, pb[j * KEY_CHUNK:(j + 1) * KEY_CHUNK, :])
                pv = t if pv is None else pv + t
            out.append((m_new, alpha * acc + pv))
        return tuple(out)

    def prologue():
        for g in range(len(tiles[0])):
            score(tiles[0], 0, 0, g)

    if first is None:
        prologue()
    else:
        pl.when(first)(prologue)

    results = []
    start = 0
    for t, chains in enumerate(tiles):
        following = tiles[t + 1] if t + 1 < len(tiles) else upcoming

        def step(c, buf, carry, chains=chains):
            return consume(chains, c, buf, carry, (chains, c + 1))

        carry = tuple((jnp.full((1, n), -jnp.inf, F32), jnp.zeros((dv + SUM_ROWS, n), F32)) for _ in chains)
        n_pairs = (n_chunks - 1) // 2
        if n_pairs > 0:
            carry = lax.fori_loop(
                0, n_pairs, lambda j, cr, s0=start, step=step: step(2 * j + 1, 1 - s0, step(2 * j, s0, cr)), carry)
        if (n_chunks - 1) % 2:
            carry = step(n_chunks - 2, (start + n_chunks - 2) % 2, carry)
        last = (start + n_chunks - 1) % 2
        carry = consume(chains, n_chunks - 1, last, carry, None if following is None else (following, 0))
        results.append([(acc[:dv], acc[dv:dv + 1]) for _, acc in carry])
        start = 1 - last
    assert upcoming is None or start == 0
    return results


def _tiles_per_step(n_tiles, n_slabs):
    n_chunks = n_slabs // _slabs_per_chunk(n_slabs)
    if n_tiles % 2 == 0:
        return 2
    assert n_tiles == 1 or n_chunks % 2 == 0
    return 1


def _kv_specs(k_width, v_rows, key_range, k_buffers, v_buffers):
    lo, hi = key_range
    n_slabs = hi - lo
    assert lo % n_slabs == 0
    blk = lo // n_slabs
    return [pl.BlockSpec((1, n_slabs * KEY_CHUNK, k_width), lambda bi, i: (bi, blk, 0),
                         pipeline_mode=pl.Buffered(k_buffers)),
            pl.BlockSpec((1, n_slabs, v_rows, KEY_CHUNK), lambda bi, i: (bi, blk, 0, 0),
                         pipeline_mode=pl.Buffered(v_buffers))]


def _score_scratch(n_chains, n_slabs, n):
    per = _slabs_per_chunk(n_slabs)
    return [pltpu.VMEM((2, n_chains, per * KEY_CHUNK, n), F32),
            pltpu.VMEM((2, n_chains, per, SUBLANES, n), F32)]


def _mla_kernel(q_ref, qn_ref, k_ref, vt_ref, o_ref, s_ref, mx_ref, *, tq, chained):
    def chains_of(ref, t):
        return [(ref[0, g * LANES:(g + 1) * LANES, t * tq:(t + 1) * tq], slice(g * LANES, (g + 1) * LANES),
                 slice(g * MLA_V, (g + 1) * MLA_V)) for g in range(MLA_HEADS)]

    tiles = [chains_of(q_ref, t) for t in range(q_ref.shape[2] // tq)]
    res = _attend(tiles, chains_of(qn_ref, 0) if chained else None,
                  pl.program_id(1) == 0 if chained else None,
                  k_ref, vt_ref, s_ref, mx_ref, MLA_V, q_transposed=True)
    for t, (chains, out) in enumerate(zip(tiles, res)):
        for (_, _, rows), (acc, l) in zip(chains, out):
            o_ref[0, rows, t * tq:(t + 1) * tq] = (acc / l).astype(BF16)


def _mla_attention(q, k, vt, key_range):
    b, w, s = q.shape
    tq = min(MLA_Q_TILE, s)
    n_tiles = s // tq
    n_slabs = key_range[1] - key_range[0]
    per_step = _tiles_per_step(n_tiles, n_slabs)
    n_steps = n_tiles // per_step
    return pl.pallas_call(
        functools.partial(_mla_kernel, tq=tq, chained=n_steps > 1),
        grid=(b, n_steps),
        in_specs=[pl.BlockSpec((1, w, per_step * tq), lambda bi, i: (bi, 0, i)),
                  pl.BlockSpec((1, w, tq), lambda bi, i: (bi, 0, jnp.minimum((i + 1) * per_step, n_tiles - 1)))]
                 + _kv_specs(w, MLA_HEADS * MLA_V, key_range, 1, 2),
        out_specs=pl.BlockSpec((1, MLA_HEADS * MLA_V, per_step * tq), lambda bi, i: (bi, 0, i)),
        out_shape=jax.ShapeDtypeStruct((b, MLA_HEADS * MLA_V, s), BF16),
        scratch_shapes=_score_scratch(MLA_HEADS, n_slabs, tq),
        compiler_params=_params("parallel", "arbitrary"),
        name="mla_attn",
    )(q, q, k, vt)


def _diff_kernel(lam_ref, q_ref, qn_ref, k_ref, vt_ref, gain_ref, o_ref, s_ref, mx_ref, *, tq, chained, out_scale):
    first_half = lax.broadcasted_iota(jnp.int32, (tq, LANES), 1) < DIFF_QK
    zero = jnp.zeros((tq, LANES), BF16)

    def chains_of(ref, t):
        chains = []
        for hd in range(DIFF_HEADS):
            lanes = slice(hd * LANES, (hd + 1) * LANES)
            q = ref[0, t * tq:(t + 1) * tq, lanes]
            q_rhs = jnp.concatenate([jnp.where(first_half, q, zero), jnp.where(first_half, zero, q)], axis=0)
            chains.append((q_rhs, lanes, slice(hd * DIFF_V, (hd + 1) * DIFF_V)))
        return chains

    tiles = [chains_of(q_ref, t) for t in range(q_ref.shape[1] // tq)]
    res = _attend(tiles, chains_of(qn_ref, 0) if chained else None,
                  pl.program_id(1) == 0 if chained else None,
                  k_ref, vt_ref, s_ref, mx_ref, DIFF_V, q_transposed=False)
    for t, (chains, out) in enumerate(zip(tiles, res)):
        for (_, _, rows), (acc, l) in zip(chains, out):
            o = acc / l
            o = o[:, :tq] - lam_ref[0] * o[:, tq:]
            o = o * lax.rsqrt(jnp.mean(o * o, axis=0, keepdims=True) + EPS) * gain_ref[...]
            o_ref[0, rows, t * tq:(t + 1) * tq] = (o * out_scale).astype(BF16)


def _diff_attention(q, k, vt, key_range, lam, gain, lambda_init):
    b, s, w = q.shape
    tq = min(DIFF_Q_TILE, s)
    n_tiles = s // tq
    n_slabs = key_range[1] - key_range[0]
    per_step = _tiles_per_step(n_tiles, n_slabs)
    n_steps = n_tiles // per_step
    return pl.pallas_call(
        functools.partial(_diff_kernel, tq=tq, chained=n_steps > 1, out_scale=1.0 - lambda_init),
        grid=(b, n_steps),
        in_specs=[pl.BlockSpec(memory_space=pltpu.SMEM),
                  pl.BlockSpec((1, per_step * tq, w), lambda bi, i: (bi, i, 0)),
                  pl.BlockSpec((1, tq, w), lambda bi, i: (bi, jnp.minimum((i + 1) * per_step, n_tiles - 1), 0))]
                 + _kv_specs(w, DIFF_HEADS * DIFF_V, key_range, 2, 2)
                 + [pl.BlockSpec((DIFF_V, 1), lambda bi, i: (0, 0))],
        out_specs=pl.BlockSpec((1, DIFF_HEADS * DIFF_V, per_step * tq), lambda bi, i: (bi, 0, i)),
        out_shape=jax.ShapeDtypeStruct((b, DIFF_HEADS * DIFF_V, s), BF16),
        scratch_shapes=_score_scratch(DIFF_HEADS, n_slabs, 2 * tq),
        compiler_params=_params("parallel", "arbitrary"),
        name="diff_attn",
    )(lam.reshape(1), q, q, k, vt, gain.reshape(DIFF_V, 1))


def _ret_kernel(q_ref, k_ref, v_ref, s0_ref, intra_ref, qdec_ref, kdec_ref, cdec_ref, *rest, reverse, post):
    if post:
        of_ref, g_ref, gain_ref, o_ref, sout_ref, state = rest
    else:
        o_ref, sout_ref, state = rest
    i = pl.program_id(1)

    @pl.when(i == 0)
    def _():
        state[...] = s0_ref[0]

    n_chunks = q_ref.shape[1] // RET_CHUNK
    lane = lax.broadcasted_iota(jnp.int32, (RET_CHUNK, RET_HEADS * RET_K), 1)
    masks = [(lane >= hd * RET_K) & (lane < (hd + 1) * RET_K) for hd in range(RET_HEADS)]
    order = [n_chunks - 1 - cc if reverse else cc for cc in range(n_chunks)]
    heads = range(RET_HEADS)

    def head_v(c, hd):
        return v_ref[0, c * RET_CHUNK:(c + 1) * RET_CHUNK, hd * RET_V:(hd + 1) * RET_V]

    intra, deltas = {}, {}
    for c in order:
        rows = slice(c * RET_CHUNK, (c + 1) * RET_CHUNK)
        qc, kc = q_ref[0, rows, :], k_ref[0, rows, :]
        kdt = (kc.astype(F32) * kdec_ref[...]).T.astype(BF16)
        for hd in heads:
            a = lax.dot_general(jnp.where(masks[hd], qc, jnp.zeros_like(qc)), kc, _NT,
                                preferred_element_type=F32)
            intra[c, hd] = (a * intra_ref[hd]).astype(BF16)
            deltas[c, hd] = _dot(kdt[hd * RET_K:(hd + 1) * RET_K, :], head_v(c, hd))
    st = state[...]
    entering = {}
    for c in order:
        entering[c] = st.astype(BF16)
        st = st * cdec_ref[...] + jnp.concatenate([deltas[c, hd] for hd in heads], axis=0)
    state[...] = st
    for c in order:
        rows = slice(c * RET_CHUNK, (c + 1) * RET_CHUNK)
        qd = (q_ref[0, rows, :].astype(F32) * qdec_ref[...]).astype(BF16)
        outs = [_dot(intra[c, hd], head_v(c, hd))
                + _dot(jnp.where(masks[hd], qd, jnp.zeros_like(qd)), entering[c]) for hd in heads]
        if post:
            of = of_ref[0, rows, :]
            g = g_ref[0, rows, :].astype(F32)
            for hd in range(RET_HEADS):
                sl = slice(hd * RET_V, (hd + 1) * RET_V)
                outs[hd] = _rms(outs[hd] + of[:, sl], gain_ref[...]) * (g[:, sl] * jax.nn.sigmoid(g[:, sl]))
        o_ref[0, rows, :] = jnp.concatenate(outs, axis=1).astype(o_ref.dtype)

    @pl.when(i == pl.num_programs(1) - 1)
    def _():
        sout_ref[0] = state[...]


def _retention(q, k, v, s0, tabs, *, reverse, post=None):
    b, n, _ = q.shape
    t = min(RET_TILE, n)
    nt = n // t
    hk, hv = RET_HEADS * RET_K, RET_HEADS * RET_V
    tile = (lambda bi, i: (bi, nt - 1 - i, 0)) if reverse else (lambda bi, i: (bi, i, 0))
    st_spec = pl.BlockSpec((1, hk, RET_V), lambda bi, i: (bi, 0, 0))
    in_specs = [pl.BlockSpec((1, t, hk), tile), pl.BlockSpec((1, t, hk), tile), pl.BlockSpec((1, t, hv), tile),
                st_spec] + [_const(a.shape) for a in tabs]
    args = [q, k, v, s0, *tabs]
    if post is not None:
        o_f, g, gain = post
        in_specs += [pl.BlockSpec((1, t, hv), tile), pl.BlockSpec((1, t, hv), tile), _const((1, RET_V))]
        args += [o_f, g, gain.reshape(1, RET_V)]
    return pl.pallas_call(
        functools.partial(_ret_kernel, reverse=reverse, post=post is not None),
        grid=(b, nt),
        in_specs=in_specs,
        out_specs=[pl.BlockSpec((1, t, hv), tile), st_spec],
        out_shape=[jax.ShapeDtypeStruct((b, n, hv), F32 if post is None else BF16),
                   jax.ShapeDtypeStruct((b, hk, RET_V), F32)],
        scratch_shapes=[pltpu.VMEM((hk, RET_V), F32)],
        compiler_params=_params("parallel", "arbitrary"),
        name="ret_bwd" if reverse else "ret_fwd",
    )(*args)


def _decay_tables(log_gamma, reverse):
    idx = jnp.arange(RET_CHUNK, dtype=F32)
    lg = log_gamma.astype(F32)[:, None]
    rel = idx[:, None] - idx[None, :]
    if reverse:
        rel = -rel
        q_pow, k_pow = RET_CHUNK - idx, idx
    else:
        q_pow, k_pow = idx + 1.0, RET_CHUNK - 1.0 - idx
    intra = jnp.where(rel >= 0, jnp.exp(lg[:, :, None] * jnp.maximum(rel, 0.0)), 0.0)
    qdec = jnp.repeat(jnp.exp(lg * q_pow).T, RET_K, axis=1)
    kdec = jnp.repeat(jnp.exp(lg * k_pow).T, RET_K, axis=1)
    cdec = jnp.broadcast_to(jnp.repeat(jnp.exp(lg * RET_CHUNK), RET_K, axis=0), (RET_HEADS * RET_K, RET_V))
    return intra, qdec, kdec, cdec


def _merge_kernel(x_ref, mod_ref, gain_ref, wg_ref, wb_ref, wo_ref, mla_ref, dif_ref, ret_ref, o_ref):
    tm, d = x_ref.shape[1], x_ref.shape[2]
    halves = [slice(0, tm // 2), slice(tm // 2, tm)]

    def gated(rows):
        x = x_ref[0, rows, :]
        h = _modulated(x, gain_ref[1:2, :], mod_ref[0, 3:4, :], mod_ref[0, 4:5, :]).astype(BF16)
        branches = (lax.dot_general(mla_ref[0, :, rows], wb_ref[0], _TN, preferred_element_type=F32),
                    lax.dot_general(dif_ref[0, :, rows], wb_ref[1], _TN, preferred_element_type=F32),
                    _dot(ret_ref[0, rows, :], wb_ref[2]))
        y = None
        for j, br in enumerate(branches):
            t = jax.nn.sigmoid(_dot(h, wg_ref[:, j * d:(j + 1) * d])) * br
            y = t if y is None else y + t
        return x, y.astype(BF16)

    mixed = [gated(rows) for rows in halves]
    for rows, (x, y) in zip(halves, mixed):
        o_ref[0, rows, :] = x + mod_ref[0, 5:6, :] * _dot(y, wo_ref[...])


def _merge(x, mod, gain, wg, wb, wo, mla_t, dif_t, ret):
    b, s, d = x.shape
    tm = min(MERGE_TILE, s)
    tok = lambda w: pl.BlockSpec((1, tm, w), lambda bi, i: (bi, i, 0))
    feat = pl.BlockSpec((1, BRANCH_W, tm), lambda bi, i: (bi, 0, i))
    return pl.pallas_call(
        _merge_kernel,
        grid=(b, s // tm),
        in_specs=[tok(d), pl.BlockSpec((1, N_MOD, d), lambda bi, i: (bi, 0, 0)), _const(gain.shape),
                  _const(wg.shape), _const(wb.shape), _const(wo.shape), feat, feat, tok(BRANCH_W)],
        out_specs=tok(d),
        out_shape=jax.ShapeDtypeStruct(x.shape, F32),
        compiler_params=_params("parallel", "parallel"),
        name="merge",
    )(x, mod, gain, wg, wb, wo, mla_t, dif_t, ret)


def _head_slots(w, n_heads, width):
    r = w.shape[0]
    w = w.reshape(r, n_heads, width)
    return jnp.pad(w, ((0, 0), (0, 0), (0, LANES - width))).reshape(r, n_heads * LANES)


def _proj_weights(w_in, w_qb, w_kvb, q_norm, kv_norm):
    sizes = (MLA_Q_RANK, MLA_KV_RANK, MLA_ROPE, 2 * DIFF_HEADS * DIFF_QK, 2 * DIFF_HEADS * DIFF_QK,
             DIFF_HEADS * DIFF_V, RET_HEADS * RET_K, RET_HEADS * RET_K, RET_HEADS * RET_V, RET_HEADS * RET_V)
    offs = [0]
    for sz in sizes:
        offs.append(offs[-1] + sz)
    p = [w_in[:, offs[i]:offs[i + 1]] for i in range(len(sizes))]
    gates = w_in[:, offs[-1]:]
    d = w_in.shape[0]
    lead = jnp.zeros((d, MLA_NOPE), F32)
    tail = jnp.zeros((d, LANES - MLA_NOPE - MLA_ROPE), F32)
    main = jnp.concatenate([
        p[0], p[1],
        lead, p[2], tail,
        p[3], p[4], p[6], p[7], p[8], p[9]], axis=1)
    assert main.shape[1] == _W_MAIN
    hd = MLA_NOPE + MLA_ROPE
    kvb = w_kvb.reshape(MLA_KV_RANK, MLA_HEADS, MLA_NOPE + MLA_V)
    return {
        "main": main.astype(BF16),
        "qa": _head_slots(w_qb, MLA_HEADS, hd).T.astype(BF16),
        "ka": _head_slots(kvb[:, :, :MLA_NOPE].reshape(MLA_KV_RANK, -1), MLA_HEADS, MLA_NOPE).astype(BF16),
        "vt": kvb[:, :, MLA_NOPE:].reshape(MLA_KV_RANK, -1).T.astype(BF16),
        "dvt": p[5].T.astype(BF16),
        "qn": q_norm.reshape(1, -1),
        "kvn": kv_norm.reshape(1, -1),
    }, gates.astype(BF16)


def _angles(pos, dim):
    inv = ROPE_BASE ** (-jnp.arange(0, dim, 2, dtype=F32) / dim)
    ang = pos.astype(F32)[:, None] * inv[None, :]
    return jnp.cos(ang), jnp.sin(ang)


def _rope_tables(n):
    t = jnp.arange(n)
    row = jnp.repeat(jnp.arange(n // GRID_W), GRID_W)
    col = t - row * GRID_W

    def axial(dim):
        cr, sr = _angles(row, dim // 2)
        cc, sc = _angles(col, dim // 2)
        return jnp.concatenate([cr, cr, cc, cc], axis=1), jnp.concatenate([sr, sr, sc, sc], axis=1)

    cm, sm = axial(MLA_ROPE)
    ones = jnp.ones((n, MLA_NOPE), F32)
    pad = jnp.zeros((n, LANES - MLA_NOPE - MLA_ROPE), F32)
    cm = jnp.concatenate([ones, cm, pad], axis=1)
    cd, sd = axial(DIFF_QK)
    cs, ss = _angles(t, RET_K)
    sign = jnp.array([-1.0, 1.0, -1.0, 1.0], F32)
    sm = jnp.concatenate([0 * ones, sm * jnp.repeat(sign, MLA_ROPE // 4), pad], axis=1)
    sd = sd * jnp.repeat(sign, DIFF_QK // 4)
    return (cm, sm, jnp.tile(cd, (1, 2)), jnp.tile(sd, (1, 2)),
            jnp.tile(cs, (1, 4)), jnp.tile(jnp.concatenate([-ss, ss], axis=1), (1, 2)),
            cm.T, sm.T)


def _identity_tables(n):
    one, zero = jnp.ones((n, LANES), F32), jnp.zeros((n, LANES), F32)
    return one, zero, one, zero, one, zero, one.T, zero.T


def _chunked(w, axis):
    if axis == 1:
        r = w.shape[0]
        return w.reshape(r, D_FF // FF_CHUNK, FF_CHUNK).transpose(1, 0, 2).astype(BF16)
    return w.reshape(D_FF // FF_CHUNK, FF_CHUNK, w.shape[1]).astype(BF16)


def kernel(x, c, ctx, c_ctx, ada_w, ada_b, norm_gain, ffn1_w1, ffn1_w3, ffn1_w2, ffn2_w1, ffn2_w3, ffn2_w2,
           w_in, mla_q_norm, mla_w_qb, mla_kv_norm, mla_w_kvb, diff_lambda, diff_norm, ret_decay, ret_norm,
           w_branch, w_out, final_norm):
    b, n, d = x.shape
    n_ctx = ctx.shape[1]
    pad_rows = -(b + 1) % 8
    cs = jnp.concatenate([c, c_ctx[None, :], jnp.zeros((pad_rows, d), F32)], axis=0)
    mods = _ada(cs, ada_w, ada_b)
    x_tabs, c_tabs = _rope_tables(n), _identity_tables(n_ctx)
    zero_state = jnp.zeros((b, RET_HEADS * RET_K, RET_V), F32)
    n_x, n_all = n // KEY_CHUNK, (n + n_ctx) // KEY_CHUNK
    xc = ctx
    for l in range(DEPTH):
        lambda_init = 0.8 - 0.6 * math.exp(-0.3 * l)
        ctx_out = l < DEPTH - 1
        mx = mods[l, :b].reshape(b, N_MOD, d)
        mc = jnp.broadcast_to(mods[l, b].reshape(1, N_MOD, d), (b, N_MOD, d))
        gain = norm_gain[l]
        f1 = (_chunked(ffn1_w1[l], 1), _chunked(ffn1_w3[l], 1), _chunked(ffn1_w2[l], 0))
        f2 = (_chunked(ffn2_w1[l], 1), _chunked(ffn2_w3[l], 1), _chunked(ffn2_w2[l], 0))
        pw, wg = _proj_weights(w_in[l], mla_w_qb[l], mla_w_kvb[l], mla_q_norm[l], mla_kv_norm[l])
        wb, wo = w_branch[l].astype(BF16), w_out[l].astype(BF16)
        dl = diff_lambda[l].astype(F32)
        lam = jnp.exp(jnp.sum(dl[0] * dl[1])) - jnp.exp(jnp.sum(dl[2] * dl[3])) + lambda_init
        log_gamma = -jnp.exp(ret_decay[l].astype(F32))
        tab_f, tab_b = _decay_tables(log_gamma[0], False), _decay_tables(log_gamma[1], True)

        x = _ffn(x, mx, gain, *f1, row0=0, gain_row=0)
        xc = _ffn(xc, mc, gain, *f1, row0=0, gain_row=0)

        mq_t, mk, mvt, dq, dk, dvt, rq, rk, rv, rg = _proj(x, mx, gain, pw, x_tabs, n_keys=n + n_ctx)
        cmq_t, mk, mvt, cdq, dk, dvt, crq, crk, crv, crg = _proj(
            xc, mc, gain, pw, c_tabs, n_keys=n + n_ctx, key_off=n_x, into=(mk, mvt, dk, dvt))

        mla_t = _mla_attention(mq_t, mk, mvt, (0, n_all))
        dif_t = _diff_attention(dq, dk, dvt, (0, n_all), lam, diff_norm[l], lambda_init)

        oc_f, st_f = _retention(crq, crk, crv, zero_state, tab_f, reverse=False)
        ret_c, st_b = _retention(crq, crk, crv, zero_state, tab_b, reverse=True,
                                 post=(oc_f, crg, ret_norm[l]) if ctx_out else None)
        ox_f, _ = _retention(rq, rk, rv, st_f, tab_f, reverse=False)
        ret_x, _ = _retention(rq, rk, rv, st_b, tab_b, reverse=True, post=(ox_f, rg, ret_norm[l]))

        x = _merge(x, mx, gain, wg, wb, wo, mla_t, dif_t, ret_x)
        x = _ffn(x, mx, gain, *f2, row0=6, gain_row=2, final_gain=None if ctx_out else final_norm)
        if ctx_out:
            mla_c = _mla_attention(cmq_t, mk, mvt, (n_x, n_all))
            dif_c = _diff_attention(cdq, dk, dvt, (n_x, n_all), lam, diff_norm[l], lambda_init)
            xc = _merge(xc, mc, gain, wg, wb, wo, mla_c, dif_c, ret_c)
            xc = _ffn(xc, mc, gain, *f2, row0=6, gain_row=2)
    return x
```

```python
import functools
import math

import jax
import jax.numpy as jnp
from jax import lax
from jax.experimental import pallas as pl
from jax.experimental.pallas import tpu as pltpu

DEPTH = 2
GRID_W = 64
MLA_HEADS = 8
MLA_Q_RANK = 384
MLA_KV_RANK = 256
MLA_NOPE = 64
MLA_ROPE = 32
MLA_V = 64
DIFF_HEADS = 4
DIFF_QK = 64
DIFF_V = 128
RET_HEADS = 4
RET_K = 64
RET_V = 128
RET_CHUNK = 128
BRANCH_W = 512
D_FF = 2816
N_MOD = 9
ROPE_BASE = 10000.0
EPS = 1e-6

LANES = 128
SUBLANES = 8
LOG2E = 1.4426950408889634
FF_CHUNK = 256
KEY_CHUNK = 256
TOKEN_TILE = 512
RET_TILE = 2048
MLA_Q_TILE = 256
DIFF_Q_TILE = 256
VMEM_LIMIT = 56 * 1024 * 1024

BF16 = jnp.bfloat16
F32 = jnp.float32
_NT = (((1,), (1,)), ((), ()))
_TN = (((0,), (0,)), ((), ()))


def _params(*sem):
    return pltpu.CompilerParams(dimension_semantics=sem, vmem_limit_bytes=VMEM_LIMIT)


def _const(shape):
    zeros = (0,) * len(shape)
    return pl.BlockSpec(shape, lambda *_: zeros, pipeline_mode=pl.Buffered(1))


def _rms(x, gain):
    return x * lax.rsqrt(jnp.mean(x * x, axis=-1, keepdims=True) + EPS) * gain


def _modulated(x, gain, shift, scale):
    return _rms(x, gain) * (1.0 + scale) + shift


def _dot(a, b):
    return jnp.dot(a, b, preferred_element_type=F32)


def _ada_kernel(c_ref, w_ref, b_ref, o_ref):
    c = c_ref[...]
    o_ref[0] = _dot(c * jax.nn.sigmoid(c), w_ref[0]) + b_ref[0]


def _ada(cs, ada_w, ada_b):
    nl, d, n = ada_w.shape
    r = cs.shape[0]
    tn = n // N_MOD
    return pl.pallas_call(
        _ada_kernel,
        grid=(nl, n // tn),
        in_specs=[pl.BlockSpec((r, d), lambda l, j: (0, 0)),
                  pl.BlockSpec((1, d, tn), lambda l, j: (l, 0, j)),
                  pl.BlockSpec((1, 1, tn), lambda l, j: (l, 0, j))],
        out_specs=pl.BlockSpec((1, r, tn), lambda l, j: (l, 0, j)),
        out_shape=jax.ShapeDtypeStruct((nl, r, n), F32),
        compiler_params=_params("parallel", "parallel"),
        name="ada",
    )(cs, ada_w, ada_b.reshape(nl, 1, n))


def _half_step(x, mod_ref, gain_ref, w1_ref, w3_ref, w2_ref, acc_ref, row0, gain_row):
    h = _modulated(x, gain_ref[gain_row:gain_row + 1, :], mod_ref[0, row0:row0 + 1, :],
                   mod_ref[0, row0 + 1:row0 + 2, :]).astype(BF16)
    n_chunks = w1_ref.shape[0]
    up = (_dot(h, w1_ref[0]), _dot(h, w3_ref[0]))
    for c in range(n_chunks):
        a, b = up
        if c + 1 < n_chunks:
            up = (_dot(h, w1_ref[c + 1]), _dot(h, w3_ref[c + 1]))
        t = _dot((a * jax.nn.sigmoid(a) * b).astype(BF16), w2_ref[c])
        if c == 0:
            acc_ref[...] = t
        else:
            acc_ref[...] += t
    return x + (mod_ref[0, row0 + 2:row0 + 3, :] * 0.5) * acc_ref[...]


def _ffn_kernel(x_ref, mod_ref, gain_ref, w1_ref, w3_ref, w2_ref, *rest, row0, gain_row, final):
    if final:
        fg_ref, o_ref, acc_ref = rest
    else:
        o_ref, acc_ref = rest
    y = _half_step(x_ref[0], mod_ref, gain_ref, w1_ref, w3_ref, w2_ref, acc_ref, row0, gain_row)
    if final:
        y = _rms(y, fg_ref[...])
    o_ref[0] = y


def _ffn(x, mod, gain, w1, w3, w2, *, row0, gain_row, final_gain=None):
    b, s, d = x.shape
    tm = min(TOKEN_TILE, s)
    final = final_gain is not None
    in_specs = [pl.BlockSpec((1, tm, d), lambda bi, i: (bi, i, 0)),
                pl.BlockSpec((1, N_MOD, d), lambda bi, i: (bi, 0, 0)),
                _const(gain.shape), _const(w1.shape), _const(w3.shape), _const(w2.shape)]
    args = [x, mod, gain, w1, w3, w2]
    if final:
        in_specs.append(_const((1, d)))
        args.append(final_gain.reshape(1, d))
    return pl.pallas_call(
        functools.partial(_ffn_kernel, row0=row0, gain_row=gain_row, final=final),
        grid=(b, s // tm),
        in_specs=in_specs,
        out_specs=pl.BlockSpec((1, tm, d), lambda bi, i: (bi, i, 0)),
        out_shape=jax.ShapeDtypeStruct(x.shape, F32),
        scratch_shapes=[pltpu.VMEM((tm, d), F32)],
        compiler_params=_params("parallel", "parallel"),
        name="ffn",
    )(*args)


_O_QLAT = 0
_O_KVLAT = _O_QLAT + MLA_Q_RANK
_O_KR = _O_KVLAT + MLA_KV_RANK
_O_DQ = _O_KR + LANES
_O_RQ = _O_DQ + 2 * BRANCH_W
_O_RV = _O_RQ + 2 * RET_HEADS * RET_K
_W_MAIN = _O_RV + 2 * BRANCH_W


def _rotary(x, cos, sin_signed, half):
    lane = lax.broadcasted_iota(jnp.int32, x.shape, 1)
    first = (lane & (2 * half - 1)) < half
    partner = jnp.where(first, pltpu.roll(x, LANES - half, 1), pltpu.roll(x, half, 1))
    return x * cos + partner * sin_signed


_KEY_OUTPUTS = (1, 2, 4, 5)


def _proj_kernel(x_ref, mod_ref, gain_ref, wm_ref, wqa_ref, wka_ref, wvt_ref, wdvt_ref,
                 qn_ref, kvn_ref, cm_ref, sm_ref, cd_ref, sd_ref, cr_ref, sr_ref, cmt_ref, smt_ref, *rest):
    mqt_ref, mk_ref, mvt_ref, dq_ref, dk_ref, dvt_ref, rq_ref, rk_ref, rv_ref, rg_ref = rest[-10:]
    x = x_ref[0]
    h = _modulated(x, gain_ref[1:2, :], mod_ref[0, 3:4, :], mod_ref[0, 4:5, :]).astype(BF16)

    def seg(lo, width):
        return _dot(h, wm_ref[:, lo:lo + width])

    hk = RET_HEADS * RET_K
    qlat_raw, kvlat_raw = seg(_O_QLAT, MLA_Q_RANK), seg(_O_KVLAT, MLA_KV_RANK)
    kr = seg(_O_KR, LANES)
    dqk = seg(_O_DQ, 2 * BRANCH_W)
    rqk = seg(_O_RQ, 2 * hk)
    rv_ref[0] = seg(_O_RV, BRANCH_W).astype(BF16)
    rg_ref[0] = seg(_O_RV + BRANCH_W, BRANCH_W).astype(BF16)
    dvt_ref[0, 0] = lax.dot_general(wdvt_ref[...], h, _NT, preferred_element_type=F32).astype(BF16)

    cd, sd = cd_ref[...], sd_ref[...]
    d_scale = DIFF_QK ** -0.5 * LOG2E
    for j in range(BRANCH_W // LANES):
        sl = slice(j * LANES, (j + 1) * LANES)
        sk = slice(BRANCH_W + j * LANES, BRANCH_W + (j + 1) * LANES)
        dq_ref[0, :, sl] = (_rotary(dqk[:, sl], cd, sd, DIFF_QK // 4) * d_scale).astype(BF16)
        dk_ref[0, :, sl] = _rotary(dqk[:, sk], cd, sd, DIFF_QK // 4).astype(BF16)

    cr, sr_t = cr_ref[...], sr_ref[...]
    for j in range(hk // LANES):
        sl = slice(j * LANES, (j + 1) * LANES)
        sk = slice(hk + j * LANES, hk + (j + 1) * LANES)
        rq_ref[0, :, sl] = _rotary(rqk[:, sl], cr, sr_t, RET_K // 2).astype(BF16)
        rk_ref[0, :, sl] = (_rotary(rqk[:, sk], cr, sr_t, RET_K // 2) * RET_K ** -0.5).astype(BF16)

    qlat = _rms(qlat_raw, qn_ref[...]).astype(BF16)
    kvlat = _rms(kvlat_raw, kvn_ref[...]).astype(BF16)
    cm, sm = cm_ref[...], sm_ref[...]
    cmt, smt = cmt_ref[...], smt_ref[...]
    qat = lax.dot_general(wqa_ref[...], qlat, _NT, preferred_element_type=F32)
    ka = _dot(kvlat, wka_ref[...])
    mvt_ref[0, 0] = lax.dot_general(wvt_ref[...], kvlat, _NT, preferred_element_type=F32).astype(BF16)
    krr = _rotary(kr, cm, sm, MLA_ROPE // 4)
    q_scale = (MLA_NOPE + MLA_ROPE) ** -0.5 * LOG2E
    half = MLA_ROPE // 4
    for hd in range(MLA_HEADS):
        sl = slice(hd * LANES, (hd + 1) * LANES)
        q = qat[sl, :]
        swapped = [q[MLA_NOPE + (i ^ 1) * half:MLA_NOPE + ((i ^ 1) + 1) * half] for i in range(MLA_ROPE // half)]
        partner = jnp.concatenate([q[:MLA_NOPE]] + swapped + [q[MLA_NOPE + MLA_ROPE:]], axis=0)
        mqt_ref[0, sl, :] = ((q * cmt + partner * smt) * q_scale).astype(BF16)
        mk_ref[0, :, sl] = (ka[:, sl] + krr).astype(BF16)


def _proj(x, mod, gain, pw, tables, *, n_keys, key_off=0, into=None):
    b, s, d = x.shape
    tm = KEY_CHUNK
    nt = s // tm
    tok = lambda w, off=0: pl.BlockSpec((1, tm, w), lambda bi, i: (bi, i + off, 0))
    slab = pl.BlockSpec((1, 1, BRANCH_W, tm), lambda bi, i: (bi, i + key_off, 0, 0))
    tab = pl.BlockSpec((tm, LANES), lambda bi, i: (i, 0))
    tab_t = pl.BlockSpec((LANES, tm), lambda bi, i: (0, i))
    weights = [pw["main"], pw["qa"], pw["ka"], pw["vt"], pw["dvt"], pw["qn"], pw["kvn"]]
    hk = RET_HEADS * RET_K
    widths = [MLA_HEADS * LANES, MLA_HEADS * LANES, None, BRANCH_W, BRANCH_W, None, hk, hk, BRANCH_W, BRANCH_W]
    out_specs, out_shape = [], []
    for pos, w in enumerate(widths):
        if pos == 0:
            out_specs.append(pl.BlockSpec((1, w, tm), lambda bi, i: (bi, 0, i)))
            out_shape.append(jax.ShapeDtypeStruct((b, w, s), BF16))
        elif w is None:
            out_specs.append(slab)
            out_shape.append(jax.ShapeDtypeStruct((b, n_keys // tm, BRANCH_W, tm), BF16))
        elif pos in _KEY_OUTPUTS:
            out_specs.append(tok(w, key_off))
            out_shape.append(jax.ShapeDtypeStruct((b, n_keys, w), BF16))
        else:
            out_specs.append(tok(w))
            out_shape.append(jax.ShapeDtypeStruct((b, s, w), BF16))
    in_specs = [pl.BlockSpec((1, tm, d), lambda bi, i: (bi, i, 0)),
                pl.BlockSpec((1, N_MOD, d), lambda bi, i: (bi, 0, 0)),
                _const(gain.shape)] + [_const(w.shape) for w in weights] + [tab] * 6 + [tab_t] * 2
    args = [x, mod, gain, *weights, *tables]
    aliases = {}
    if into is not None:
        for buf, pos in zip(into, _KEY_OUTPUTS):
            aliases[len(args)] = pos
            in_specs.append(pl.BlockSpec(memory_space=pl.ANY))
            args.append(buf)
    return pl.pallas_call(
        _proj_kernel,
        grid=(b, nt),
        in_specs=in_specs,
        out_specs=out_specs,
        out_shape=out_shape,
        input_output_aliases=aliases,
        compiler_params=_params("parallel", "parallel"),
        name="proj",
    )(*args)


BF16_ROWS = 2 * SUBLANES
SUM_ROWS = BF16_ROWS


def _slabs_per_chunk(n_slabs):
    return next(p for p in (3, 2, 1) if n_slabs % p == 0)


def _attend(tiles, upcoming, first, k_ref, vt_ref, s_ref, mx_ref, dv, *, q_transposed):
    n = tiles[0][0][0].shape[1 if q_transposed else 0]
    n_slabs = vt_ref.shape[1]
    per = _slabs_per_chunk(n_slabs)
    tk = per * KEY_CHUNK
    n_chunks = n_slabs // per
    ones = jnp.ones((SUM_ROWS, KEY_CHUNK), BF16)

    def score(chains, c, buf, g):
        q_rhs, lanes, _ = chains[g]
        start = c * tk if isinstance(c, int) else pl.multiple_of(c * tk, KEY_CHUNK)
        k = k_ref[0, pl.ds(start, tk), lanes]
        s = _dot(k, q_rhs) if q_transposed else lax.dot_general(k, q_rhs, _NT, preferred_element_type=F32)
        s_ref[buf, g] = s
        for j in range(per):
            part = s[j * KEY_CHUNK:(j + 1) * KEY_CHUNK]
            mx_ref[buf, g, j] = jnp.max(part.reshape(KEY_CHUNK // SUBLANES, SUBLANES, n), axis=0)

    def consume(chains, c, buf, carry, following):
        out = []
        for g, ((_, _, rows), (m, acc)) in enumerate(zip(chains, carry)):
            if following is not None:
                score(following[0], following[1], 1 - buf, g)
            mx = mx_ref[buf, g, 0]
            for j in range(1, per):
                mx = jnp.maximum(mx, mx_ref[buf, g, j])
            m_new = jnp.maximum(m, jnp.max(mx, axis=0, keepdims=True))
            alpha = jnp.exp2(m - m_new)
            pb = jnp.exp2(s_ref[buf, g] - m_new).astype(BF16)
            pv = None
            for j in range(per):
                vt = jnp.concatenate([vt_ref[0, c * per + j, rows, :], ones], axis=0)
                t = _dot(vt, pb[j * KEY_CHUNK:(j + 1) * KEY_CHUNK, :])
                pv = t if pv is None else pv + t
            out.append((m_new, alpha * acc + pv))
        return tuple(out)

    def prologue():
        for g in range(len(tiles[0])):
            score(tiles[0], 0, 0, g)

    if first is None:
        prologue()
    else:
        pl.when(first)(prologue)

    results = []
    start = 0
    for t, chains in enumerate(tiles):
        following = tiles[t + 1] if t + 1 < len(tiles) else upcoming

        def step(c, buf, carry, chains=chains):
            return consume(chains, c, buf, carry, (chains, c + 1))

        carry = tuple((jnp.full((1, n), -jnp.inf, F32), jnp.zeros((dv + SUM_ROWS, n), F32)) for _ in chains)
        n_pairs = (n_chunks - 1) // 2
        if n_pairs > 0:
            carry = lax.fori_loop(
                0, n_pairs, lambda j, cr, s0=start, step=step: step(2 * j + 1, 1 - s0, step(2 * j, s0, cr)), carry)
        if (n_chunks - 1) % 2:
            carry = step(n_chunks - 2, (start + n_chunks - 2) % 2, carry)
        last = (start + n_chunks - 1) % 2
        carry = consume(chains, n_chunks - 1, last, carry, None if following is None else (following, 0))
        results.append([(acc[:dv], acc[dv:dv + 1]) for _, acc in carry])
        start = 1 - last
    assert upcoming is None or start == 0
    return results


def _tiles_per_step(n_tiles, n_slabs):
    n_chunks = n_slabs // _slabs_per_chunk(n_slabs)
    if n_tiles % 2 == 0:
        return 2
    assert n_tiles == 1 or n_chunks % 2 == 0
    return 1


def _kv_specs(k_width, v_rows, key_range, k_buffers, v_buffers):
    lo, hi = key_range
    n_slabs = hi - lo
    assert lo % n_slabs == 0
    blk = lo // n_slabs
    return [pl.BlockSpec((1, n_slabs * KEY_CHUNK, k_width), lambda bi, i: (bi, blk, 0),
                         pipeline_mode=pl.Buffered(k_buffers)),
            pl.BlockSpec((1, n_slabs, v_rows, KEY_CHUNK), lambda bi, i: (bi, blk, 0, 0),
                         pipeline_mode=pl.Buffered(v_buffers))]


def _score_scratch(n_chains, n_slabs, n):
    per = _slabs_per_chunk(n_slabs)
    return [pltpu.VMEM((2, n_chains, per * KEY_CHUNK, n), F32),
            pltpu.VMEM((2, n_chains, per, SUBLANES, n), F32)]


def _mla_kernel(q_ref, qn_ref, k_ref, vt_ref, o_ref, s_ref, mx_ref, *, tq, chained):
    def chains_of(ref, t):
        return [(ref[0, g * LANES:(g + 1) * LANES, t * tq:(t + 1) * tq], slice(g * LANES, (g + 1) * LANES),
                 slice(g * MLA_V, (g + 1) * MLA_V)) for g in range(MLA_HEADS)]

    tiles = [chains_of(q_ref, t) for t in range(q_ref.shape[2] // tq)]
    res = _attend(tiles, chains_of(qn_ref, 0) if chained else None,
                  pl.program_id(1) == 0 if chained else None,
                  k_ref, vt_ref, s_ref, mx_ref, MLA_V, q_transposed=True)
    for t, (chains, out) in enumerate(zip(tiles, res)):
        for (_, _, rows), (acc, l) in zip(chains, out):
            o_ref[0, rows, t * tq:(t + 1) * tq] = (acc / l).astype(BF16)


def _mla_attention(q, k, vt, key_range):
    b, w, s = q.shape
    tq = min(MLA_Q_TILE, s)
    n_tiles = s // tq
    n_slabs = key_range[1] - key_range[0]
    per_step = _tiles_per_step(n_tiles, n_slabs)
    n_steps = n_tiles // per_step
    return pl.pallas_call(
        functools.partial(_mla_kernel, tq=tq, chained=n_steps > 1),
        grid=(b, n_steps),
        in_specs=[pl.BlockSpec((1, w, per_step * tq), lambda bi, i: (bi, 0, i)),
                  pl.BlockSpec((1, w, tq), lambda bi, i: (bi, 0, jnp.minimum((i + 1) * per_step, n_tiles - 1)))]
                 + _kv_specs(w, MLA_HEADS * MLA_V, key_range, 1, 2),
        out_specs=pl.BlockSpec((1, MLA_HEADS * MLA_V, per_step * tq), lambda bi, i: (bi, 0, i)),
        out_shape=jax.ShapeDtypeStruct((b, MLA_HEADS * MLA_V, s), BF16),
        scratch_shapes=_score_scratch(MLA_HEADS, n_slabs, tq),
        compiler_params=_params("parallel", "arbitrary"),
        name="mla_attn",
    )(q, q, k, vt)


def _diff_kernel(lam_ref, q_ref, qn_ref, k_ref, vt_ref, gain_ref, o_ref, s_ref, mx_ref, *, tq, chained, out_scale):
    first_half = lax.broadcasted_iota(jnp.int32, (tq, LANES), 1) < DIFF_QK
    zero = jnp.zeros((tq, LANES), BF16)

    def chains_of(ref, t):
        chains = []
        for hd in range(DIFF_HEADS):
            lanes = slice(hd * LANES, (hd + 1) * LANES)
            q = ref[0, t * tq:(t + 1) * tq, lanes]
            q_rhs = jnp.concatenate([jnp.where(first_half, q, zero), jnp.where(first_half, zero, q)], axis=0)
            chains.append((q_rhs, lanes, slice(hd * DIFF_V, (hd + 1) * DIFF_V)))
        return chains

    tiles = [chains_of(q_ref, t) for t in range(q_ref.shape[1] // tq)]
    res = _attend(tiles, chains_of(qn_ref, 0) if chained else None,
                  pl.program_id(1) == 0 if chained else None,
                  k_ref, vt_ref, s_ref, mx_ref, DIFF_V, q_transposed=False)
    for t, (chains, out) in enumerate(zip(tiles, res)):
        for (_, _, rows), (acc, l) in zip(chains, out):
            o = acc / l
            o = o[:, :tq] - lam_ref[0] * o[:, tq:]
            o = o * lax.rsqrt(jnp.mean(o * o, axis=0, keepdims=True) + EPS) * gain_ref[...]
            o_ref[0, rows, t * tq:(t + 1) * tq] = (o * out_scale).astype(BF16)


def _diff_attention(q, k, vt, key_range, lam, gain, lambda_init):
    b, s, w = q.shape
    tq = min(DIFF_Q_TILE, s)
    n_tiles = s // tq
    n_slabs = key_range[1] - key_range[0]
    per_step = _tiles_per_step(n_tiles, n_slabs)
    n_steps = n_tiles // per_step
    return pl.pallas_call(
        functools.partial(_diff_kernel, tq=tq, chained=n_steps > 1, out_scale=1.0 - lambda_init),
        grid=(b, n_steps),
        in_specs=[pl.BlockSpec(memory_space=pltpu.SMEM),
                  pl.BlockSpec((1, per_step * tq, w), lambda bi, i: (bi, i, 0)),
                  pl.BlockSpec((1, tq, w), lambda bi, i: (bi, jnp.minimum((i + 1) * per_step, n_tiles - 1), 0))]
                 + _kv_specs(w, DIFF_HEADS * DIFF_V, key_range, 2, 2)
                 + [pl.BlockSpec((DIFF_V, 1), lambda bi, i: (0, 0))],
        out_specs=pl.BlockSpec((1, DIFF_HEADS * DIFF_V, per_step * tq), lambda bi, i: (bi, 0, i)),
        out_shape=jax.ShapeDtypeStruct((b, DIFF_HEADS * DIFF_V, s), BF16),
        scratch_shapes=_score_scratch(DIFF_HEADS, n_slabs, 2 * tq),
        compiler_params=_params("parallel", "arbitrary"),
        name="diff_attn",
    )(lam.reshape(1), q, q, k, vt, gain.reshape(DIFF_V, 1))


def _ret_kernel(q_ref, k_ref, v_ref, s0_ref, intra_ref, qdec_ref, kdec_ref, cdec_ref, *rest, reverse, post):
    if post:
        of_ref, g_ref, gain_ref, o_ref, sout_ref, state = rest
    else:
        o_ref, sout_ref, state = rest
    i = pl.program_id(1)

    @pl.when(i == 0)
    def _():
        state[...] = s0_ref[0]

    n_chunks = q_ref.shape[1] // RET_CHUNK
    lane = lax.broadcasted_iota(jnp.int32, (RET_CHUNK, RET_HEADS * RET_K), 1)
    masks = [(lane >= hd * RET_K) & (lane < (hd + 1) * RET_K) for hd in range(RET_HEADS)]
    order = [n_chunks - 1 - cc if reverse else cc for cc in range(n_chunks)]
    heads = range(RET_HEADS)

    def head_v(c, hd):
        return v_ref[0, c * RET_CHUNK:(c + 1) * RET_CHUNK, hd * RET_V:(hd + 1) * RET_V]

    intra, deltas = {}, {}
    for c in order:
        rows = slice(c * RET_CHUNK, (c + 1) * RET_CHUNK)
        qc, kc = q_ref[0, rows, :], k_ref[0, rows, :]
        kdt = (kc.astype(F32) * kdec_ref[...]).T.astype(BF16)
        for hd in heads:
            a = lax.dot_general(jnp.where(masks[hd], qc, jnp.zeros_like(qc)), kc, _NT,
                                preferred_element_type=F32)
            intra[c, hd] = (a * intra_ref[hd]).astype(BF16)
            deltas[c, hd] = _dot(kdt[hd * RET_K:(hd + 1) * RET_K, :], head_v(c, hd))
    st = state[...]
    entering = {}
    for c in order:
        entering[c] = st.astype(BF16)
        st = st * cdec_ref[...] + jnp.concatenate([deltas[c, hd] for hd in heads], axis=0)
    state[...] = st
    for c in order:
        rows = slice(c * RET_CHUNK, (c + 1) * RET_CHUNK)
        qd = (q_ref[0, rows, :].astype(F32) * qdec_ref[...]).astype(BF16)
        outs = [_dot(intra[c, hd], head_v(c, hd))
                + _dot(jnp.where(masks[hd], qd, jnp.zeros_like(qd)), entering[c]) for hd in heads]
        if post:
            of = of_ref[0, rows, :]
            g = g_ref[0, rows, :].astype(F32)
            for hd in range(RET_HEADS):
                sl = slice(hd * RET_V, (hd + 1) * RET_V)
                outs[hd] = _rms(outs[hd] + of[:, sl], gain_ref[...]) * (g[:, sl] * jax.nn.sigmoid(g[:, sl]))
        o_ref[0, rows, :] = jnp.concatenate(outs, axis=1).astype(o_ref.dtype)

    @pl.when(i == pl.num_programs(1) - 1)
    def _():
        sout_ref[0] = state[...]


def _retention(q, k, v, s0, tabs, *, reverse, post=None):
    b, n, _ = q.shape
    t = min(RET_TILE, n)
    nt = n // t
    hk, hv = RET_HEADS * RET_K, RET_HEADS * RET_V
    tile = (lambda bi, i: (bi, nt - 1 - i, 0)) if reverse else (lambda bi, i: (bi, i, 0))
    st_spec = pl.BlockSpec((1, hk, RET_V), lambda bi, i: (bi, 0, 0))
    in_specs = [pl.BlockSpec((1, t, hk), tile), pl.BlockSpec((1, t, hk), tile), pl.BlockSpec((1, t, hv), tile),
                st_spec] + [_const(a.shape) for a in tabs]
    args = [q, k, v, s0, *tabs]
    if post is not None:
        o_f, g, gain = post
        in_specs += [pl.BlockSpec((1, t, hv), tile), pl.BlockSpec((1, t, hv), tile), _const((1, RET_V))]
        args += [o_f, g, gain.reshape(1, RET_V)]
    return pl.pallas_call(
        functools.partial(_ret_kernel, reverse=reverse, post=post is not None),
        grid=(b, nt),
        in_specs=in_specs,
        out_specs=[pl.BlockSpec((1, t, hv), tile), st_spec],
        out_shape=[jax.ShapeDtypeStruct((b, n, hv), F32 if post is None else BF16),
                   jax.ShapeDtypeStruct((b, hk, RET_V), F32)],
        scratch_shapes=[pltpu.VMEM((hk, RET_V), F32)],
        compiler_params=_params("parallel", "arbitrary"),
        name="ret_bwd" if reverse else "ret_fwd",
    )(*args)


def _decay_tables(log_gamma, reverse):
    idx = jnp.arange(RET_CHUNK, dtype=F32)
    lg = log_gamma.astype(F32)[:, None]
    rel = idx[:, None] - idx[None, :]
    if reverse:
        rel = -rel
        q_pow, k_pow = RET_CHUNK - idx, idx
    else:
        q_pow, k_pow = idx + 1.0, RET_CHUNK - 1.0 - idx
    intra = jnp.where(rel >= 0, jnp.exp(lg[:, :, None] * jnp.maximum(rel, 0.0)), 0.0)
    qdec = jnp.repeat(jnp.exp(lg * q_pow).T, RET_K, axis=1)
    kdec = jnp.repeat(jnp.exp(lg * k_pow).T, RET_K, axis=1)
    cdec = jnp.broadcast_to(jnp.repeat(jnp.exp(lg * RET_CHUNK), RET_K, axis=0), (RET_HEADS * RET_K, RET_V))
    return intra, qdec, kdec, cdec


def _merge_ffn_kernel(x_ref, mod_ref, gain_ref, wg_ref, wb_ref, wo_ref, mla_ref, dif_ref, ret_ref,
                      w1_ref, w3_ref, w2_ref, *rest, final):
    if final:
        fg_ref, o_ref, acc_ref = rest
    else:
        o_ref, acc_ref = rest
    tm, d = x_ref.shape[1], x_ref.shape[2]
    halves = [slice(0, tm // 2), slice(tm // 2, tm)]

    def gated(rows):
        x = x_ref[0, rows, :]
        h = _modulated(x, gain_ref[1:2, :], mod_ref[0, 3:4, :], mod_ref[0, 4:5, :]).astype(BF16)
        branches = (lax.dot_general(mla_ref[0, :, rows], wb_ref[0], _TN, preferred_element_type=F32),
                    lax.dot_general(dif_ref[0, :, rows], wb_ref[1], _TN, preferred_element_type=F32),
                    _dot(ret_ref[0, rows, :], wb_ref[2]))
        y = None
        for j, br in enumerate(branches):
            t = jax.nn.sigmoid(_dot(h, wg_ref[:, j * d:(j + 1) * d])) * br
            y = t if y is None else y + t
        return x, y.astype(BF16)

    mixed = [gated(rows) for rows in halves]
    x1 = jnp.concatenate([x + mod_ref[0, 5:6, :] * _dot(y, wo_ref[...]) for x, y in mixed], axis=0)
    y = _half_step(x1, mod_ref, gain_ref, w1_ref, w3_ref, w2_ref, acc_ref, 6, 2)
    if final:
        y = _rms(y, fg_ref[...])
    o_ref[0] = y


def _merge_ffn(x, mod, gain, wg, wb, wo, mla_t, dif_t, ret, w1, w3, w2, final_gain=None):
    b, s, d = x.shape
    tm = min(TOKEN_TILE, s)
    final = final_gain is not None
    tok = lambda w: pl.BlockSpec((1, tm, w), lambda bi, i: (bi, i, 0))
    feat = pl.BlockSpec((1, BRANCH_W, tm), lambda bi, i: (bi, 0, i))
    in_specs = [tok(d), pl.BlockSpec((1, N_MOD, d), lambda bi, i: (bi, 0, 0)), _const(gain.shape),
                _const(wg.shape), _const(wb.shape), _const(wo.shape), feat, feat, tok(BRANCH_W),
                _const(w1.shape), _const(w3.shape), _const(w2.shape)]
    args = [x, mod, gain, wg, wb, wo, mla_t, dif_t, ret, w1, w3, w2]
    if final:
        in_specs.append(_const((1, d)))
        args.append(final_gain.reshape(1, d))
    return pl.pallas_call(
        functools.partial(_merge_ffn_kernel, final=final),
        grid=(b, s // tm),
        in_specs=in_specs,
        out_specs=tok(d),
        out_shape=jax.ShapeDtypeStruct(x.shape, F32),
        scratch_shapes=[pltpu.VMEM((tm, d), F32)],
        compiler_params=_params("parallel", "parallel"),
        name="merge_ffn",
    )(*args)


def _head_slots(w, n_heads, width):
    r = w.shape[0]
    w = w.reshape(r, n_heads, width)
    return jnp.pad(w, ((0, 0), (0, 0), (0, LANES - width))).reshape(r, n_heads * LANES)


def _proj_weights(w_in, w_qb, w_kvb, q_norm, kv_norm):
    sizes = (MLA_Q_RANK, MLA_KV_RANK, MLA_ROPE, 2 * DIFF_HEADS * DIFF_QK, 2 * DIFF_HEADS * DIFF_QK,
             DIFF_HEADS * DIFF_V, RET_HEADS * RET_K, RET_HEADS * RET_K, RET_HEADS * RET_V, RET_HEADS * RET_V)
    offs = [0]
    for sz in sizes:
        offs.append(offs[-1] + sz)
    p = [w_in[:, offs[i]:offs[i + 1]] for i in range(len(sizes))]
    gates = w_in[:, offs[-1]:]
    d = w_in.shape[0]
    lead = jnp.zeros((d, MLA_NOPE), F32)
    tail = jnp.zeros((d, LANES - MLA_NOPE - MLA_ROPE), F32)
    main = jnp.concatenate([
        p[0], p[1],
        lead, p[2], tail,
        p[3], p[4], p[6], p[7], p[8], p[9]], axis=1)
    assert main.shape[1] == _W_MAIN
    hd = MLA_NOPE + MLA_ROPE
    kvb = w_kvb.reshape(MLA_KV_RANK, MLA_HEADS, MLA_NOPE + MLA_V)
    return {
        "main": main.astype(BF16),
        "qa": _head_slots(w_qb, MLA_HEADS, hd).T.astype(BF16),
        "ka": _head_slots(kvb[:, :, :MLA_NOPE].reshape(MLA_KV_RANK, -1), MLA_HEADS, MLA_NOPE).astype(BF16),
        "vt": kvb[:, :, MLA_NOPE:].reshape(MLA_KV_RANK, -1).T.astype(BF16),
        "dvt": p[5].T.astype(BF16),
        "qn": q_norm.reshape(1, -1),
        "kvn": kv_norm.reshape(1, -1),
    }, gates.astype(BF16)


def _angles(pos, dim):
    inv = ROPE_BASE ** (-jnp.arange(0, dim, 2, dtype=F32) / dim)
    ang = pos.astype(F32)[:, None] * inv[None, :]
    return jnp.cos(ang), jnp.sin(ang)


def _rope_tables(n):
    t = jnp.arange(n)
    row = jnp.repeat(jnp.arange(n // GRID_W), GRID_W)
    col = t - row * GRID_W

    def axial(dim):
        cr, sr = _angles(row, dim // 2)
        cc, sc = _angles(col, dim // 2)
        return jnp.concatenate([cr, cr, cc, cc], axis=1), jnp.concatenate([sr, sr, sc, sc], axis=1)

    cm, sm = axial(MLA_ROPE)
    ones = jnp.ones((n, MLA_NOPE), F32)
    pad = jnp.zeros((n, LANES - MLA_NOPE - MLA_ROPE), F32)
    cm = jnp.concatenate([ones, cm, pad], axis=1)
    cd, sd = axial(DIFF_QK)
    cs, ss = _angles(t, RET_K)
    sign = jnp.array([-1.0, 1.0, -1.0, 1.0], F32)
    sm = jnp.concatenate([0 * ones, sm * jnp.repeat(sign, MLA_ROPE // 4), pad], axis=1)
    sd = sd * jnp.repeat(sign, DIFF_QK // 4)
    return (cm, sm, jnp.tile(cd, (1, 2)), jnp.tile(sd, (1, 2)),
            jnp.tile(cs, (1, 4)), jnp.tile(jnp.concatenate([-ss, ss], axis=1), (1, 2)),
            cm.T, sm.T)


def _identity_tables(n):
    one, zero = jnp.ones((n, LANES), F32), jnp.zeros((n, LANES), F32)
    return one, zero, one, zero, one, zero, one.T, zero.T


def _chunked(w, axis):
    if axis == 1:
        r = w.shape[0]
        return w.reshape(r, D_FF // FF_CHUNK, FF_CHUNK).transpose(1, 0, 2).astype(BF16)
    return w.reshape(D_FF // FF_CHUNK, FF_CHUNK, w.shape[1]).astype(BF16)


def kernel(x, c, ctx, c_ctx, ada_w, ada_b, norm_gain, ffn1_w1, ffn1_w3, ffn1_w2, ffn2_w1, ffn2_w3, ffn2_w2,
           w_in, mla_q_norm, mla_w_qb, mla_kv_norm, mla_w_kvb, diff_lambda, diff_norm, ret_decay, ret_norm,
           w_branch, w_out, final_norm):
    b, n, d = x.shape
    n_ctx = ctx.shape[1]
    pad_rows = -(b + 1) % 8
    cs = jnp.concatenate([c, c_ctx[None, :], jnp.zeros((pad_rows, d), F32)], axis=0)
    mods = _ada(cs, ada_w, ada_b)
    x_tabs, c_tabs = _rope_tables(n), _identity_tables(n_ctx)
    zero_state = jnp.zeros((b, RET_HEADS * RET_K, RET_V), F32)
    n_x, n_all = n // KEY_CHUNK, (n + n_ctx) // KEY_CHUNK
    xc = ctx
    for l in range(DEPTH):
        lambda_init = 0.8 - 0.6 * math.exp(-0.3 * l)
        ctx_out = l < DEPTH - 1
        mx = mods[l, :b].reshape(b, N_MOD, d)
        mc = jnp.broadcast_to(mods[l, b].reshape(1, N_MOD, d), (b, N_MOD, d))
        gain = norm_gain[l]
        f1 = (_chunked(ffn1_w1[l], 1), _chunked(ffn1_w3[l], 1), _chunked(ffn1_w2[l], 0))
        f2 = (_chunked(ffn2_w1[l], 1), _chunked(ffn2_w3[l], 1), _chunked(ffn2_w2[l], 0))
        pw, wg = _proj_weights(w_in[l], mla_w_qb[l], mla_w_kvb[l], mla_q_norm[l], mla_kv_norm[l])
        wb, wo = w_branch[l].astype(BF16), w_out[l].astype(BF16)
        dl = diff_lambda[l].astype(F32)
        lam = jnp.exp(jnp.sum(dl[0] * dl[1])) - jnp.exp(jnp.sum(dl[2] * dl[3])) + lambda_init
        log_gamma = -jnp.exp(ret_decay[l].astype(F32))
        tab_f, tab_b = _decay_tables(log_gamma[0], False), _decay_tables(log_gamma[1], True)

        x = _ffn(x, mx, gain, *f1, row0=0, gain_row=0)
        xc = _ffn(xc, mc, gain, *f1, row0=0, gain_row=0)

        mq_t, mk, mvt, dq, dk, dvt, rq, rk, rv, rg = _proj(x, mx, gain, pw, x_tabs, n_keys=n + n_ctx)
        cmq_t, mk, mvt, cdq, dk, dvt, crq, crk, crv, crg = _proj(
            xc, mc, gain, pw, c_tabs, n_keys=n + n_ctx, key_off=n_x, into=(mk, mvt, dk, dvt))

        mla_t = _mla_attention(mq_t, mk, mvt, (0, n_all))
        dif_t = _diff_attention(dq, dk, dvt, (0, n_all), lam, diff_norm[l], lambda_init)

        oc_f, st_f = _retention(crq, crk, crv, zero_state, tab_f, reverse=False)
        ret_c, st_b = _retention(crq, crk, crv, zero_state, tab_b, reverse=True,
                                 post=(oc_f, crg, ret_norm[l]) if ctx_out else None)
        ox_f, _ = _retention(rq, rk, rv, st_f, tab_f, reverse=False)
        ret_x, _ = _retention(rq, rk, rv, st_b, tab_b, reverse=True, post=(ox_f, rg, ret_norm[l]))

        x = _merge_ffn(x, mx, gain, wg, wb, wo, mla_t, dif_t, ret_x, *f2,
                       final_gain=None if ctx_out else final_norm)
        if ctx_out:
            mla_c = _mla_attention(cmq_t, mk, mvt, (n_x, n_all))
            dif_c = _diff_attention(cdq, dk, dvt, (n_x, n_all), lam, diff_norm[l], lambda_init)
            xc = _merge_ffn(xc, mc, gain, wg, wb, wo, mla_c, dif_c, ret_c, *f2)
    return x
```

```python
import functools
import math

import jax
import jax.numpy as jnp
from jax import lax
from jax.experimental import pallas as pl
from jax.experimental.pallas import tpu as pltpu

DEPTH = 2
GRID_W = 64
MLA_HEADS = 8
MLA_Q_RANK = 384
MLA_KV_RANK = 256
MLA_NOPE = 64
MLA_ROPE = 32
MLA_V = 64
DIFF_HEADS = 4
DIFF_QK = 64
DIFF_V = 128
RET_HEADS = 4
RET_K = 64
RET_V = 128
RET_CHUNK = 128
BRANCH_W = 512
D_FF = 2816
N_MOD = 9
ROPE_BASE = 10000.0
EPS = 1e-6

LANES = 128
SUBLANES = 8
LOG2E = 1.4426950408889634
FF_CHUNK = 256
KEY_CHUNK = 256
TOKEN_TILE = 512
RET_TILE = 2048
MLA_Q_TILE = 256
DIFF_Q_TILE = 256
VMEM_LIMIT = 56 * 1024 * 1024

BF16 = jnp.bfloat16
F32 = jnp.float32
_NT = (((1,), (1,)), ((), ()))
_TN = (((0,), (0,)), ((), ()))


def _params(*sem):
    return pltpu.CompilerParams(dimension_semantics=sem, vmem_limit_bytes=VMEM_LIMIT)


def _const(shape):
    zeros = (0,) * len(shape)
    return pl.BlockSpec(shape, lambda *_: zeros, pipeline_mode=pl.Buffered(1))


def _rms(x, gain):
    return x * lax.rsqrt(jnp.mean(x * x, axis=-1, keepdims=True) + EPS) * gain


def _modulated(x, gain, shift, scale):
    return _rms(x, gain) * (1.0 + scale) + shift


def _dot(a, b):
    return jnp.dot(a, b, preferred_element_type=F32)


def _ada_kernel(c_ref, w_ref, b_ref, o_ref):
    c = c_ref[...]
    o_ref[0] = _dot(c * jax.nn.sigmoid(c), w_ref[0]) + b_ref[0]


def _ada(cs, ada_w, ada_b):
    nl, d, n = ada_w.shape
    r = cs.shape[0]
    tn = n // N_MOD
    return pl.pallas_call(
        _ada_kernel,
        grid=(nl, n // tn),
        in_specs=[pl.BlockSpec((r, d), lambda l, j: (0, 0)),
                  pl.BlockSpec((1, d, tn), lambda l, j: (l, 0, j)),
                  pl.BlockSpec((1, 1, tn), lambda l, j: (l, 0, j))],
        out_specs=pl.BlockSpec((1, r, tn), lambda l, j: (l, 0, j)),
        out_shape=jax.ShapeDtypeStruct((nl, r, n), F32),
        compiler_params=_params("parallel", "parallel"),
        name="ada",
    )(cs, ada_w, ada_b.reshape(nl, 1, n))


def _half_step(x, mod_ref, gain_ref, w1_ref, w3_ref, w2_ref, acc_ref, row0, gain_row):
    h = _modulated(x, gain_ref[gain_row:gain_row + 1, :], mod_ref[0, row0:row0 + 1, :],
                   mod_ref[0, row0 + 1:row0 + 2, :]).astype(BF16)
    n_chunks = w1_ref.shape[0]
    up = (_dot(h, w1_ref[0]), _dot(h, w3_ref[0]))
    for c in range(n_chunks):
        a, b = up
        if c + 1 < n_chunks:
            up = (_dot(h, w1_ref[c + 1]), _dot(h, w3_ref[c + 1]))
        t = _dot((a * jax.nn.sigmoid(a) * b).astype(BF16), w2_ref[c])
        if c == 0:
            acc_ref[...] = t
        else:
            acc_ref[...] += t
    return x + (mod_ref[0, row0 + 2:row0 + 3, :] * 0.5) * acc_ref[...]


_O_QLAT = 0
_O_KVLAT = _O_QLAT + MLA_Q_RANK
_O_KR = _O_KVLAT + MLA_KV_RANK
_O_DQ = _O_KR + LANES
_O_RQ = _O_DQ + 2 * BRANCH_W
_O_RV = _O_RQ + 2 * RET_HEADS * RET_K
_W_MAIN = _O_RV + 2 * BRANCH_W


def _rotary(x, cos, sin_signed, half):
    lane = lax.broadcasted_iota(jnp.int32, x.shape, 1)
    first = (lane & (2 * half - 1)) < half
    partner = jnp.where(first, pltpu.roll(x, LANES - half, 1), pltpu.roll(x, half, 1))
    return x * cos + partner * sin_signed


_KEY_OUTPUTS = (1, 2, 4, 5)


def _proj_kernel(x_ref, mod_ref, gain_ref, wm_ref, wqa_ref, wka_ref, wvt_ref, wdvt_ref,
                 qn_ref, kvn_ref, cm_ref, sm_ref, cd_ref, sd_ref, cr_ref, sr_ref, cmt_ref, smt_ref, *rest, half_step):
    weights = (wm_ref, wqa_ref, wka_ref, wvt_ref, wdvt_ref, qn_ref, kvn_ref)
    tables = (cm_ref, sm_ref, cd_ref, sd_ref, cr_ref, sr_ref, cmt_ref, smt_ref)
    x = x_ref[0]
    if half_step:
        w1_ref, w3_ref, w2_ref = rest[:3]
        xo_ref, acc_ref = rest[-12], rest[-1]
        outs = rest[-11:-1]
    else:
        outs = rest[-10:]
    pieces = []
    for t in range(x.shape[0] // KEY_CHUNK):
        rows = slice(t * KEY_CHUNK, (t + 1) * KEY_CHUNK)
        xt = x[rows]
        if half_step:
            xt = _half_step(xt, mod_ref, gain_ref, w1_ref, w3_ref, w2_ref, acc_ref.at[rows], 0, 0)
            xo_ref[0, rows, :] = xt
        pieces.append((xt, rows, t))
    for xt, rows, t in pieces:
        _project(xt, rows, t, mod_ref, gain_ref, weights, tables, outs)


def _project(x, rows, t, mod_ref, gain_ref, weights, tables, outs):
    wm_ref, wqa_ref, wka_ref, wvt_ref, wdvt_ref, qn_ref, kvn_ref = weights
    cm_ref, sm_ref, cd_ref, sd_ref, cr_ref, sr_ref, cmt_ref, smt_ref = tables
    mqt_ref, mk_ref, mvt_ref, dq_ref, dk_ref, dvt_ref, rq_ref, rk_ref, rv_ref, rg_ref = outs
    h = _modulated(x, gain_ref[1:2, :], mod_ref[0, 3:4, :], mod_ref[0, 4:5, :]).astype(BF16)

    def seg(lo, width):
        return _dot(h, wm_ref[:, lo:lo + width])

    hk = RET_HEADS * RET_K
    qlat_raw, kvlat_raw = seg(_O_QLAT, MLA_Q_RANK), seg(_O_KVLAT, MLA_KV_RANK)
    kr = seg(_O_KR, LANES)
    dqk = seg(_O_DQ, 2 * BRANCH_W)
    rqk = seg(_O_RQ, 2 * hk)
    rv_ref[0, rows, :] = seg(_O_RV, BRANCH_W).astype(BF16)
    rg_ref[0, rows, :] = seg(_O_RV + BRANCH_W, BRANCH_W).astype(BF16)
    dvt_ref[0, t] = lax.dot_general(wdvt_ref[...], h, _NT, preferred_element_type=F32).astype(BF16)

    cd, sd = cd_ref[rows, :], sd_ref[rows, :]
    d_scale = DIFF_QK ** -0.5 * LOG2E
    for j in range(BRANCH_W // LANES):
        sl = slice(j * LANES, (j + 1) * LANES)
        sk = slice(BRANCH_W + j * LANES, BRANCH_W + (j + 1) * LANES)
        dq_ref[0, rows, sl] = (_rotary(dqk[:, sl], cd, sd, DIFF_QK // 4) * d_scale).astype(BF16)
        dk_ref[0, rows, sl] = _rotary(dqk[:, sk], cd, sd, DIFF_QK // 4).astype(BF16)

    cr, sr_t = cr_ref[rows, :], sr_ref[rows, :]
    for j in range(hk // LANES):
        sl = slice(j * LANES, (j + 1) * LANES)
        sk = slice(hk + j * LANES, hk + (j + 1) * LANES)
        rq_ref[0, rows, sl] = _rotary(rqk[:, sl], cr, sr_t, RET_K // 2).astype(BF16)
        rk_ref[0, rows, sl] = (_rotary(rqk[:, sk], cr, sr_t, RET_K // 2) * RET_K ** -0.5).astype(BF16)

    qlat = _rms(qlat_raw, qn_ref[...]).astype(BF16)
    kvlat = _rms(kvlat_raw, kvn_ref[...]).astype(BF16)
    cm, sm = cm_ref[rows, :], sm_ref[rows, :]
    cmt, smt = cmt_ref[:, rows], smt_ref[:, rows]
    qat = lax.dot_general(wqa_ref[...], qlat, _NT, preferred_element_type=F32)
    ka = _dot(kvlat, wka_ref[...])
    mvt_ref[0, t] = lax.dot_general(wvt_ref[...], kvlat, _NT, preferred_element_type=F32).astype(BF16)
    krr = _rotary(kr, cm, sm, MLA_ROPE // 4)
    q_scale = (MLA_NOPE + MLA_ROPE) ** -0.5 * LOG2E
    half = MLA_ROPE // 4
    for hd in range(MLA_HEADS):
        sl = slice(hd * LANES, (hd + 1) * LANES)
        q = qat[sl, :]
        swapped = [q[MLA_NOPE + (i ^ 1) * half:MLA_NOPE + ((i ^ 1) + 1) * half] for i in range(MLA_ROPE // half)]
        partner = jnp.concatenate([q[:MLA_NOPE]] + swapped + [q[MLA_NOPE + MLA_ROPE:]], axis=0)
        mqt_ref[0, sl, rows] = ((q * cmt + partner * smt) * q_scale).astype(BF16)
        mk_ref[0, rows, sl] = (ka[:, sl] + krr).astype(BF16)


def _proj(x, mod, gain, pw, tables, *, n_keys, key_off=0, into=None, half_step=None):
    b, s, d = x.shape
    tm = min(TOKEN_TILE, s)
    nt = s // tm
    per = tm // KEY_CHUNK
    assert key_off % per == 0
    key_off //= per
    tok = lambda w, off=0: pl.BlockSpec((1, tm, w), lambda bi, i: (bi, i + off, 0))
    slab = pl.BlockSpec((1, per, BRANCH_W, KEY_CHUNK), lambda bi, i: (bi, i + key_off, 0, 0))
    tab = pl.BlockSpec((tm, LANES), lambda bi, i: (i, 0))
    tab_t = pl.BlockSpec((LANES, tm), lambda bi, i: (0, i))
    weights = [pw["main"], pw["qa"], pw["ka"], pw["vt"], pw["dvt"], pw["qn"], pw["kvn"]]
    hk = RET_HEADS * RET_K
    widths = [MLA_HEADS * LANES, MLA_HEADS * LANES, None, BRANCH_W, BRANCH_W, None, hk, hk, BRANCH_W, BRANCH_W]
    out_specs, out_shape = [], []
    for pos, w in enumerate(widths):
        if pos == 0:
            out_specs.append(pl.BlockSpec((1, w, tm), lambda bi, i: (bi, 0, i)))
            out_shape.append(jax.ShapeDtypeStruct((b, w, s), BF16))
        elif w is None:
            out_specs.append(slab)
            out_shape.append(jax.ShapeDtypeStruct((b, n_keys // KEY_CHUNK, BRANCH_W, KEY_CHUNK), BF16))
        elif pos in _KEY_OUTPUTS:
            out_specs.append(tok(w, key_off))
            out_shape.append(jax.ShapeDtypeStruct((b, n_keys, w), BF16))
        else:
            out_specs.append(tok(w))
            out_shape.append(jax.ShapeDtypeStruct((b, s, w), BF16))
    in_specs = [pl.BlockSpec((1, tm, d), lambda bi, i: (bi, i, 0)),
                pl.BlockSpec((1, N_MOD, d), lambda bi, i: (bi, 0, 0)),
                _const(gain.shape)] + [_const(w.shape) for w in weights] + [tab] * 6 + [tab_t] * 2
    args = [x, mod, gain, *weights, *tables]
    scratch = []
    if half_step is not None:
        in_specs += [_const(w.shape) for w in half_step]
        args += list(half_step)
        out_specs.insert(0, tok(d))
        out_shape.insert(0, jax.ShapeDtypeStruct(x.shape, F32))
        scratch.append(pltpu.VMEM((tm, d), F32))
    shift = 0 if half_step is None else 1
    aliases = {}
    if into is not None:
        for buf, pos in zip(into, _KEY_OUTPUTS):
            aliases[len(args)] = pos + shift
            in_specs.append(pl.BlockSpec(memory_space=pl.ANY))
            args.append(buf)
    return pl.pallas_call(
        functools.partial(_proj_kernel, half_step=half_step is not None),
        grid=(b, nt),
        in_specs=in_specs,
        out_specs=out_specs,
        out_shape=out_shape,
        scratch_shapes=scratch,
        input_output_aliases=aliases,
        compiler_params=_params("parallel", "parallel"),
        name="proj",
    )(*args)


BF16_ROWS = 2 * SUBLANES
SUM_ROWS = BF16_ROWS


def _slabs_per_chunk(n_slabs):
    return next(p for p in (3, 2, 1) if n_slabs % p == 0)


def _attend(tiles, upcoming, first, k_ref, vt_ref, s_ref, mx_ref, dv, *, q_transposed):
    n = tiles[0][0][0].shape[1 if q_transposed else 0]
    n_slabs = vt_ref.shape[1]
    per = _slabs_per_chunk(n_slabs)
    tk = per * KEY_CHUNK
    n_chunks = n_slabs // per
    ones = jnp.ones((SUM_ROWS, KEY_CHUNK), BF16)

    def score(chains, c, buf, g):
        q_rhs, lanes, _ = chains[g]
        start = c * tk if isinstance(c, int) else pl.multiple_of(c * tk, KEY_CHUNK)
        k = k_ref[0, pl.ds(start, tk), lanes]
        s = _dot(k, q_rhs) if q_transposed else lax.dot_general(k, q_rhs, _NT, preferred_element_type=F32)
        s_ref[buf, g] = s
        for j in range(per):
            part = s[j * KEY_CHUNK:(j + 1) * KEY_CHUNK]
            mx_ref[buf, g, j] = jnp.max(part.reshape(KEY_CHUNK // SUBLANES, SUBLANES, n), axis=0)

    def consume(chains, c, buf, carry, following):
        out = []
        for g, ((_, _, rows), (m, acc)) in enumerate(zip(chains, carry)):
            if following is not None:
                score(following[0], following[1], 1 - buf, g)
            mx = mx_ref[buf, g, 0]
            for j in range(1, per):
                mx = jnp.maximum(mx, mx_ref[buf, g, j])
            m_new = jnp.maximum(m, jnp.max(mx, axis=0, keepdims=True))
            alpha = jnp.exp2(m - m_new)
            pb = jnp.exp2(s_ref[buf, g] - m_new).astype(BF16)
            pv = None
            for j in range(per):
                vt = jnp.concatenate([vt_ref[0, c * per + j, rows, :], ones], axis=0)
                t = _dot(vt, pb[j * KEY_CHUNK:(j + 1) * KEY_CHUNK, :])
                pv = t if pv is None else pv + t
            out.append((m_new, alpha * acc + pv))
        return tuple(out)

    def prologue():
        for g in range(len(tiles[0])):
            score(tiles[0], 0, 0, g)

    if first is None:
        prologue()
    else:
        pl.when(first)(prologue)

    results = []
    start = 0
    for t, chains in enumerate(tiles):
        following = tiles[t + 1] if t + 1 < len(tiles) else upcoming

        def step(c, buf, carry, chains=chains):
            return consume(chains, c, buf, carry, (chains, c + 1))

        carry = tuple((jnp.full((1, n), -jnp.inf, F32), jnp.zeros((dv + SUM_ROWS, n), F32)) for _ in chains)
        n_pairs = (n_chunks - 1) // 2
        if n_pairs > 0:
            carry = lax.fori_loop(
                0, n_pairs, lambda j, cr, s0=start, step=step: step(2 * j + 1, 1 - s0, step(2 * j, s0, cr)), carry)
        if (n_chunks - 1) % 2:
            carry = step(n_chunks - 2, (start + n_chunks - 2) % 2, carry)
        last = (start + n_chunks - 1) % 2
        carry = consume(chains, n_chunks - 1, last, carry, None if following is None else (following, 0))
        results.append([(acc[:dv], acc[dv:dv + 1]) for _, acc in carry])
        start = 1 - last
    assert upcoming is None or start == 0
    return results


def _tiles_per_step(n_tiles, n_slabs):
    n_chunks = n_slabs // _slabs_per_chunk(n_slabs)
    if n_tiles % 2 == 0:
        return 2
    assert n_tiles == 1 or n_chunks % 2 == 0
    return 1


def _kv_specs(k_width, v_rows, key_range, k_buffers, v_buffers):
    lo, hi = key_range
    n_slabs = hi - lo
    assert lo % n_slabs == 0
    blk = lo // n_slabs
    return [pl.BlockSpec((1, n_slabs * KEY_CHUNK, k_width), lambda bi, i: (bi, blk, 0),
                         pipeline_mode=pl.Buffered(k_buffers)),
            pl.BlockSpec((1, n_slabs, v_rows, KEY_CHUNK), lambda bi, i: (bi, blk, 0, 0),
                         pipeline_mode=pl.Buffered(v_buffers))]


def _score_scratch(n_chains, n_slabs, n):
    per = _slabs_per_chunk(n_slabs)
    return [pltpu.VMEM((2, n_chains, per * KEY_CHUNK, n), F32),
            pltpu.VMEM((2, n_chains, per, SUBLANES, n), F32)]


def _mla_kernel(q_ref, qn_ref, k_ref, vt_ref, o_ref, s_ref, mx_ref, *, tq, chained):
    def chains_of(ref, t):
        return [(ref[0, g * LANES:(g + 1) * LANES, t * tq:(t + 1) * tq], slice(g * LANES, (g + 1) * LANES),
                 slice(g * MLA_V, (g + 1) * MLA_V)) for g in range(MLA_HEADS)]

    tiles = [chains_of(q_ref, t) for t in range(q_ref.shape[2] // tq)]
    res = _attend(tiles, chains_of(qn_ref, 0) if chained else None,
                  pl.program_id(1) == 0 if chained else None,
                  k_ref, vt_ref, s_ref, mx_ref, MLA_V, q_transposed=True)
    for t, (chains, out) in enumerate(zip(tiles, res)):
        for (_, _, rows), (acc, l) in zip(chains, out):
            o_ref[0, rows, t * tq:(t + 1) * tq] = (acc / l).astype(BF16)


def _mla_attention(q, k, vt, key_range):
    b, w, s = q.shape
    tq = min(MLA_Q_TILE, s)
    n_tiles = s // tq
    n_slabs = key_range[1] - key_range[0]
    per_step = _tiles_per_step(n_tiles, n_slabs)
    n_steps = n_tiles // per_step
    return pl.pallas_call(
        functools.partial(_mla_kernel, tq=tq, chained=n_steps > 1),
        grid=(b, n_steps),
        in_specs=[pl.BlockSpec((1, w, per_step * tq), lambda bi, i: (bi, 0, i)),
                  pl.BlockSpec((1, w, tq), lambda bi, i: (bi, 0, jnp.minimum((i + 1) * per_step, n_tiles - 1)))]
                 + _kv_specs(w, MLA_HEADS * MLA_V, key_range, 1, 2),
        out_specs=pl.BlockSpec((1, MLA_HEADS * MLA_V, per_step * tq), lambda bi, i: (bi, 0, i)),
        out_shape=jax.ShapeDtypeStruct((b, MLA_HEADS * MLA_V, s), BF16),
        scratch_shapes=_score_scratch(MLA_HEADS, n_slabs, tq),
        compiler_params=_params("parallel", "arbitrary"),
        name="mla_attn",
    )(q, q, k, vt)


def _diff_kernel(lam_ref, q_ref, qn_ref, k_ref, vt_ref, gain_ref, o_ref, s_ref, mx_ref, *, tq, chained, out_scale):
    first_half = lax.broadcasted_iota(jnp.int32, (tq, LANES), 1) < DIFF_QK
    zero = jnp.zeros((tq, LANES), BF16)

    def chains_of(ref, t):
        chains = []
        for hd in range(DIFF_HEADS):
            lanes = slice(hd * LANES, (hd + 1) * LANES)
            q = ref[0, t * tq:(t + 1) * tq, lanes]
            q_rhs = jnp.concatenate([jnp.where(first_half, q, zero), jnp.where(first_half, zero, q)], axis=0)
            chains.append((q_rhs, lanes, slice(hd * DIFF_V, (hd + 1) * DIFF_V)))
        return chains

    tiles = [chains_of(q_ref, t) for t in range(q_ref.shape[1] // tq)]
    res = _attend(tiles, chains_of(qn_ref, 0) if chained else None,
                  pl.program_id(1) == 0 if chained else None,
                  k_ref, vt_ref, s_ref, mx_ref, DIFF_V, q_transposed=False)
    for t, (chains, out) in enumerate(zip(tiles, res)):
        for (_, _, rows), (acc, l) in zip(chains, out):
            o = acc / l
            o = o[:, :tq] - lam_ref[0] * o[:, tq:]
            o = o * lax.rsqrt(jnp.mean(o * o, axis=0, keepdims=True) + EPS) * gain_ref[...]
            o_ref[0, rows, t * tq:(t + 1) * tq] = (o * out_scale).astype(BF16)


def _diff_attention(q, k, vt, key_range, lam, gain, lambda_init):
    b, s, w = q.shape
    tq = min(DIFF_Q_TILE, s)
    n_tiles = s // tq
    n_slabs = key_range[1] - key_range[0]
    per_step = _tiles_per_step(n_tiles, n_slabs)
    n_steps = n_tiles // per_step
    return pl.pallas_call(
        functools.partial(_diff_kernel, tq=tq, chained=n_steps > 1, out_scale=1.0 - lambda_init),
        grid=(b, n_steps),
        in_specs=[pl.BlockSpec(memory_space=pltpu.SMEM),
                  pl.BlockSpec((1, per_step * tq, w), lambda bi, i: (bi, i, 0)),
                  pl.BlockSpec((1, tq, w), lambda bi, i: (bi, jnp.minimum((i + 1) * per_step, n_tiles - 1), 0))]
                 + _kv_specs(w, DIFF_HEADS * DIFF_V, key_range, 2, 2)
                 + [pl.BlockSpec((DIFF_V, 1), lambda bi, i: (0, 0))],
        out_specs=pl.BlockSpec((1, DIFF_HEADS * DIFF_V, per_step * tq), lambda bi, i: (bi, 0, i)),
        out_shape=jax.ShapeDtypeStruct((b, DIFF_HEADS * DIFF_V, s), BF16),
        scratch_shapes=_score_scratch(DIFF_HEADS, n_slabs, 2 * tq),
        compiler_params=_params("parallel", "arbitrary"),
        name="diff_attn",
    )(lam.reshape(1), q, q, k, vt, gain.reshape(DIFF_V, 1))


def _ret_kernel(q_ref, k_ref, v_ref, s0_ref, intra_ref, qdec_ref, kdec_ref, cdec_ref, *rest, reverse, post):
    if post:
        of_ref, g_ref, gain_ref, o_ref, sout_ref, state = rest
    else:
        o_ref, sout_ref, state = rest
    i = pl.program_id(1)

    @pl.when(i == 0)
    def _():
        state[...] = s0_ref[0]

    n_chunks = q_ref.shape[1] // RET_CHUNK
    lane = lax.broadcasted_iota(jnp.int32, (RET_CHUNK, RET_HEADS * RET_K), 1)
    masks = [(lane >= hd * RET_K) & (lane < (hd + 1) * RET_K) for hd in range(RET_HEADS)]
    order = [n_chunks - 1 - cc if reverse else cc for cc in range(n_chunks)]
    heads = range(RET_HEADS)

    def head_v(c, hd):
        return v_ref[0, c * RET_CHUNK:(c + 1) * RET_CHUNK, hd * RET_V:(hd + 1) * RET_V]

    intra, deltas = {}, {}
    for c in order:
        rows = slice(c * RET_CHUNK, (c + 1) * RET_CHUNK)
        qc, kc = q_ref[0, rows, :], k_ref[0, rows, :]
        kdt = (kc.astype(F32) * kdec_ref[...]).T.astype(BF16)
        for hd in heads:
            a = lax.dot_general(jnp.where(masks[hd], qc, jnp.zeros_like(qc)), kc, _NT,
                                preferred_element_type=F32)
            intra[c, hd] = (a * intra_ref[hd]).astype(BF16)
            deltas[c, hd] = _dot(kdt[hd * RET_K:(hd + 1) * RET_K, :], head_v(c, hd))
    st = state[...]
    entering = {}
    for c in order:
        entering[c] = st.astype(BF16)
        st = st * cdec_ref[...] + jnp.concatenate([deltas[c, hd] for hd in heads], axis=0)
    state[...] = st
    for c in order:
        rows = slice(c * RET_CHUNK, (c + 1) * RET_CHUNK)
        qd = (q_ref[0, rows, :].astype(F32) * qdec_ref[...]).astype(BF16)
        outs = [_dot(intra[c, hd], head_v(c, hd))
                + _dot(jnp.where(masks[hd], qd, jnp.zeros_like(qd)), entering[c]) for hd in heads]
        if post:
            of = of_ref[0, rows, :]
            g = g_ref[0, rows, :].astype(F32)
            for hd in range(RET_HEADS):
                sl = slice(hd * RET_V, (hd + 1) * RET_V)
                outs[hd] = _rms(outs[hd] + of[:, sl], gain_ref[...]) * (g[:, sl] * jax.nn.sigmoid(g[:, sl]))
        o_ref[0, rows, :] = jnp.concatenate(outs, axis=1).astype(o_ref.dtype)

    @pl.when(i == pl.num_programs(1) - 1)
    def _():
        sout_ref[0] = state[...]


def _retention(q, k, v, s0, tabs, *, reverse, post=None):
    b, n, _ = q.shape
    t = min(RET_TILE, n)
    nt = n // t
    hk, hv = RET_HEADS * RET_K, RET_HEADS * RET_V
    tile = (lambda bi, i: (bi, nt - 1 - i, 0)) if reverse else (lambda bi, i: (bi, i, 0))
    st_spec = pl.BlockSpec((1, hk, RET_V), lambda bi, i: (bi, 0, 0))
    in_specs = [pl.BlockSpec((1, t, hk), tile), pl.BlockSpec((1, t, hk), tile), pl.BlockSpec((1, t, hv), tile),
                st_spec] + [_const(a.shape) for a in tabs]
    args = [q, k, v, s0, *tabs]
    if post is not None:
        o_f, g, gain = post
        in_specs += [pl.BlockSpec((1, t, hv), tile), pl.BlockSpec((1, t, hv), tile), _const((1, RET_V))]
        args += [o_f, g, gain.reshape(1, RET_V)]
    return pl.pallas_call(
        functools.partial(_ret_kernel, reverse=reverse, post=post is not None),
        grid=(b, nt),
        in_specs=in_specs,
        out_specs=[pl.BlockSpec((1, t, hv), tile), st_spec],
        out_shape=[jax.ShapeDtypeStruct((b, n, hv), F32 if post is None else BF16),
                   jax.ShapeDtypeStruct((b, hk, RET_V), F32)],
        scratch_shapes=[pltpu.VMEM((hk, RET_V), F32)],
        compiler_params=_params("parallel", "arbitrary"),
        name="ret_bwd" if reverse else "ret_fwd",
    )(*args)


def _decay_tables(log_gamma, reverse):
    idx = jnp.arange(RET_CHUNK, dtype=F32)
    lg = log_gamma.astype(F32)[:, None]
    rel = idx[:, None] - idx[None, :]
    if reverse:
        rel = -rel
        q_pow, k_pow = RET_CHUNK - idx, idx
    else:
        q_pow, k_pow = idx + 1.0, RET_CHUNK - 1.0 - idx
    intra = jnp.where(rel >= 0, jnp.exp(lg[:, :, None] * jnp.maximum(rel, 0.0)), 0.0)
    qdec = jnp.repeat(jnp.exp(lg * q_pow).T, RET_K, axis=1)
    kdec = jnp.repeat(jnp.exp(lg * k_pow).T, RET_K, axis=1)
    cdec = jnp.broadcast_to(jnp.repeat(jnp.exp(lg * RET_CHUNK), RET_K, axis=0), (RET_HEADS * RET_K, RET_V))
    return intra, qdec, kdec, cdec


def _merge_ffn_kernel(x_ref, mod_ref, gain_ref, wg_ref, wb_ref, wo_ref, mla_ref, dif_ref, ret_ref,
                      w1_ref, w3_ref, w2_ref, *rest, final):
    if final:
        fg_ref, o_ref, acc_ref = rest
    else:
        o_ref, acc_ref = rest
    tm, d = x_ref.shape[1], x_ref.shape[2]
    halves = [slice(0, tm // 2), slice(tm // 2, tm)]

    def gated(rows):
        x = x_ref[0, rows, :]
        h = _modulated(x, gain_ref[1:2, :], mod_ref[0, 3:4, :], mod_ref[0, 4:5, :]).astype(BF16)
        branches = (lax.dot_general(mla_ref[0, :, rows], wb_ref[0], _TN, preferred_element_type=F32),
                    lax.dot_general(dif_ref[0, :, rows], wb_ref[1], _TN, preferred_element_type=F32),
                    _dot(ret_ref[0, rows, :], wb_ref[2]))
        y = None
        for j, br in enumerate(branches):
            t = jax.nn.sigmoid(_dot(h, wg_ref[:, j * d:(j + 1) * d])) * br
            y = t if y is None else y + t
        return x, y.astype(BF16)

    mixed = [gated(rows) for rows in halves]
    x1 = jnp.concatenate([x + mod_ref[0, 5:6, :] * _dot(y, wo_ref[...]) for x, y in mixed], axis=0)
    y = _half_step(x1, mod_ref, gain_ref, w1_ref, w3_ref, w2_ref, acc_ref, 6, 2)
    if final:
        y = _rms(y, fg_ref[...])
    o_ref[0] = y


def _merge_ffn(x, mod, gain, wg, wb, wo, mla_t, dif_t, ret, w1, w3, w2, final_gain=None):
    b, s, d = x.shape
    tm = min(TOKEN_TILE, s)
    final = final_gain is not None
    tok = lambda w: pl.BlockSpec((1, tm, w), lambda bi, i: (bi, i, 0))
    feat = pl.BlockSpec((1, BRANCH_W, tm), lambda bi, i: (bi, 0, i))
    in_specs = [tok(d), pl.BlockSpec((1, N_MOD, d), lambda bi, i: (bi, 0, 0)), _const(gain.shape),
                _const(wg.shape), _const(wb.shape), _const(wo.shape), feat, feat, tok(BRANCH_W),
                _const(w1.shape), _const(w3.shape), _const(w2.shape)]
    args = [x, mod, gain, wg, wb, wo, mla_t, dif_t, ret, w1, w3, w2]
    if final:
        in_specs.append(_const((1, d)))
        args.append(final_gain.reshape(1, d))
    return pl.pallas_call(
        functools.partial(_merge_ffn_kernel, final=final),
        grid=(b, s // tm),
        in_specs=in_specs,
        out_specs=tok(d),
        out_shape=jax.ShapeDtypeStruct(x.shape, F32),
        scratch_shapes=[pltpu.VMEM((tm, d), F32)],
        compiler_params=_params("parallel", "parallel"),
        name="merge_ffn",
    )(*args)


def _head_slots(w, n_heads, width):
    r = w.shape[0]
    w = w.reshape(r, n_heads, width)
    return jnp.pad(w, ((0, 0), (0, 0), (0, LANES - width))).reshape(r, n_heads * LANES)


def _proj_weights(w_in, w_qb, w_kvb, q_norm, kv_norm):
    sizes = (MLA_Q_RANK, MLA_KV_RANK, MLA_ROPE, 2 * DIFF_HEADS * DIFF_QK, 2 * DIFF_HEADS * DIFF_QK,
             DIFF_HEADS * DIFF_V, RET_HEADS * RET_K, RET_HEADS * RET_K, RET_HEADS * RET_V, RET_HEADS * RET_V)
    offs = [0]
    for sz in sizes:
        offs.append(offs[-1] + sz)
    p = [w_in[:, offs[i]:offs[i + 1]] for i in range(len(sizes))]
    gates = w_in[:, offs[-1]:]
    d = w_in.shape[0]
    lead = jnp.zeros((d, MLA_NOPE), F32)
    tail = jnp.zeros((d, LANES - MLA_NOPE - MLA_ROPE), F32)
    main = jnp.concatenate([
        p[0], p[1],
        lead, p[2], tail,
        p[3], p[4], p[6], p[7], p[8], p[9]], axis=1)
    assert main.shape[1] == _W_MAIN
    hd = MLA_NOPE + MLA_ROPE
    kvb = w_kvb.reshape(MLA_KV_RANK, MLA_HEADS, MLA_NOPE + MLA_V)
    return {
        "main": main.astype(BF16),
        "qa": _head_slots(w_qb, MLA_HEADS, hd).T.astype(BF16),
        "ka": _head_slots(kvb[:, :, :MLA_NOPE].reshape(MLA_KV_RANK, -1), MLA_HEADS, MLA_NOPE).astype(BF16),
        "vt": kvb[:, :, MLA_NOPE:].reshape(MLA_KV_RANK, -1).T.astype(BF16),
        "dvt": p[5].T.astype(BF16),
        "qn": q_norm.reshape(1, -1),
        "kvn": kv_norm.reshape(1, -1),
    }, gates.astype(BF16)


def _angles(pos, dim):
    inv = ROPE_BASE ** (-jnp.arange(0, dim, 2, dtype=F32) / dim)
    ang = pos.astype(F32)[:, None] * inv[None, :]
    return jnp.cos(ang), jnp.sin(ang)


def _rope_tables(n):
    t = jnp.arange(n)
    row = jnp.repeat(jnp.arange(n // GRID_W), GRID_W)
    col = t - row * GRID_W

    def axial(dim):
        cr, sr = _angles(row, dim // 2)
        cc, sc = _angles(col, dim // 2)
        return jnp.concatenate([cr, cr, cc, cc], axis=1), jnp.concatenate([sr, sr, sc, sc], axis=1)

    cm, sm = axial(MLA_ROPE)
    ones = jnp.ones((n, MLA_NOPE), F32)
    pad = jnp.zeros((n, LANES - MLA_NOPE - MLA_ROPE), F32)
    cm = jnp.concatenate([ones, cm, pad], axis=1)
    cd, sd = axial(DIFF_QK)
    cs, ss = _angles(t, RET_K)
    sign = jnp.array([-1.0, 1.0, -1.0, 1.0], F32)
    sm = jnp.concatenate([0 * ones, sm * jnp.repeat(sign, MLA_ROPE // 4), pad], axis=1)
    sd = sd * jnp.repeat(sign, DIFF_QK // 4)
    return (cm, sm, jnp.tile(cd, (1, 2)), jnp.tile(sd, (1, 2)),
            jnp.tile(cs, (1, 4)), jnp.tile(jnp.concatenate([-ss, ss], axis=1), (1, 2)),
            cm.T, sm.T)


def _identity_tables(n):
    one, zero = jnp.ones((n, LANES), F32), jnp.zeros((n, LANES), F32)
    return one, zero, one, zero, one, zero, one.T, zero.T


def _chunked(w, axis):
    if axis == 1:
        r = w.shape[0]
        return w.reshape(r, D_FF // FF_CHUNK, FF_CHUNK).transpose(1, 0, 2).astype(BF16)
    return w.reshape(D_FF // FF_CHUNK, FF_CHUNK, w.shape[1]).astype(BF16)


def kernel(x, c, ctx, c_ctx, ada_w, ada_b, norm_gain, ffn1_w1, ffn1_w3, ffn1_w2, ffn2_w1, ffn2_w3, ffn2_w2,
           w_in, mla_q_norm, mla_w_qb, mla_kv_norm, mla_w_kvb, diff_lambda, diff_norm, ret_decay, ret_norm,
           w_branch, w_out, final_norm):
    b, n, d = x.shape
    n_ctx = ctx.shape[1]
    pad_rows = -(b + 1) % 8
    cs = jnp.concatenate([c, c_ctx[None, :], jnp.zeros((pad_rows, d), F32)], axis=0)
    mods = _ada(cs, ada_w, ada_b)
    x_tabs, c_tabs = _rope_tables(n), _identity_tables(n_ctx)
    zero_state = jnp.zeros((b, RET_HEADS * RET_K, RET_V), F32)
    n_x, n_all = n // KEY_CHUNK, (n + n_ctx) // KEY_CHUNK
    xc = ctx
    for l in range(DEPTH):
        lambda_init = 0.8 - 0.6 * math.exp(-0.3 * l)
        ctx_out = l < DEPTH - 1
        mx = mods[l, :b].reshape(b, N_MOD, d)
        mc = jnp.broadcast_to(mods[l, b].reshape(1, N_MOD, d), (b, N_MOD, d))
        gain = norm_gain[l]
        f1 = (_chunked(ffn1_w1[l], 1), _chunked(ffn1_w3[l], 1), _chunked(ffn1_w2[l], 0))
        f2 = (_chunked(ffn2_w1[l], 1), _chunked(ffn2_w3[l], 1), _chunked(ffn2_w2[l], 0))
        pw, wg = _proj_weights(w_in[l], mla_w_qb[l], mla_w_kvb[l], mla_q_norm[l], mla_kv_norm[l])
        wb, wo = w_branch[l].astype(BF16), w_out[l].astype(BF16)
        dl = diff_lambda[l].astype(F32)
        lam = jnp.exp(jnp.sum(dl[0] * dl[1])) - jnp.exp(jnp.sum(dl[2] * dl[3])) + lambda_init
        log_gamma = -jnp.exp(ret_decay[l].astype(F32))
        tab_f, tab_b = _decay_tables(log_gamma[0], False), _decay_tables(log_gamma[1], True)

        x, mq_t, mk, mvt, dq, dk, dvt, rq, rk, rv, rg = _proj(
            x, mx, gain, pw, x_tabs, n_keys=n + n_ctx, half_step=f1)
        xc, cmq_t, mk, mvt, cdq, dk, dvt, crq, crk, crv, crg = _proj(
            xc, mc, gain, pw, c_tabs, n_keys=n + n_ctx, key_off=n_x, into=(mk, mvt, dk, dvt), half_step=f1)

        mla_t = _mla_attention(mq_t, mk, mvt, (0, n_all))
        dif_t = _diff_attention(dq, dk, dvt, (0, n_all), lam, diff_norm[l], lambda_init)

        oc_f, st_f = _retention(crq, crk, crv, zero_state, tab_f, reverse=False)
        ret_c, st_b = _retention(crq, crk, crv, zero_state, tab_b, reverse=True,
                                 post=(oc_f, crg, ret_norm[l]) if ctx_out else None)
        ox_f, _ = _retention(rq, rk, rv, st_f, tab_f, reverse=False)
        ret_x, _ = _retention(rq, rk, rv, st_b, tab_b, reverse=True, post=(ox_f, rg, ret_norm[l]))

        x = _merge_ffn(x, mx, gain, wg, wb, wo, mla_t, dif_t, ret_x, *f2,
                       final_gain=None if ctx_out else final_norm)
        if ctx_out:
            mla_c = _mla_attention(cmq_t, mk, mvt, (n_x, n_all))
            dif_c = _diff_attention(cdq, dk, dvt, (n_x, n_all), lam, diff_norm[l], lambda_init)
            xc = _merge_ffn(xc, mc, gain, wg, wb, wo, mla_c, dif_c, ret_c, *f2)
    return x
```
